```python
import jax
import jax.numpy as jnp
from jax import lax
import numpy as np

D_MODEL = 2048
BATCH = 32
SEQ = 256
DEPTH = 4
DEC_BATCH = 2
DEC_SEQ = 4096
PAST_LEN = 256

GRID_W = 64
HEAD_DIM = 128
A_WIDTH = D_MODEL // 2
N_HEADS = A_WIDTH // HEAD_DIM
KV_HEADS = 2
GQA_GROUP = N_HEADS // KV_HEADS
KV_WIDTH = KV_HEADS * HEAD_DIM
WINDOW = 128
Q_BLOCK = 128
ROPE_BASE = 10000.0
ROPE_AXIS = HEAD_DIM // 2
ATTN_SCALE = HEAD_DIM ** -0.5
NEG_INF = -1e30
M_WIDTH = D_MODEL // 4
M_HEADS = 4
M_DK = M_WIDTH // M_HEADS
M_DV = M_DK
M_GATES = 2 * 2 * M_HEADS
M_CHUNK = 64
R_WIDTH = D_MODEL // 4
R_BLOCKS = 4
R_BW = R_WIDTH // R_BLOCKS
CONV_W = 4
CONV_LEFT = 2
RG_C = 8.0
D_MIX = A_WIDTH + M_WIDTH + R_WIDTH
IN_WIDTHS = (A_WIDTH, KV_WIDTH, KV_WIDTH, M_WIDTH, M_WIDTH, M_WIDTH, M_WIDTH, M_GATES, R_WIDTH, R_WIDTH)
D_IN = A_WIDTH + 2 * KV_WIDTH + 4 * M_WIDTH + M_GATES + 2 * R_WIDTH
N_GROUPS = 4
EXPERTS_PER_GROUP = 8
N_EXPERTS = N_GROUPS * EXPERTS_PER_GROUP
TOP_K = 2
D_EXPERT = D_MODEL // 2
MOE_BLOCK = 128
NORM_EPS = 1e-6

kernel_name = 'hybrid_diffusion_prefix_trunk_step'


def rmsnorm(x, g):
    xf = x.astype(jnp.float32)
    y = xf * lax.rsqrt(jnp.mean(xf * xf, axis=-1, keepdims=True) + NORM_EPS)
    return (y * g.astype(jnp.float32)).astype(x.dtype)


def adaln(cvec, w, b):
    mod = jax.nn.silu(cvec) @ w + b
    return jnp.split(mod[:, None, :], 6, axis=-1)


def modulate(h, shift, scale):
    return h * (1 + scale) + shift


def split_in(z):
    cuts = []
    acc = 0
    for w in IN_WIDTHS[:-1]:
        acc += w
        cuts.append(acc)
    return jnp.split(z, cuts, axis=-1)


def _flip(a, d):
    return a if d == 0 else jnp.flip(a, axis=1)


def rope_table(pos):
    inv = ROPE_BASE ** (-jnp.arange(0, ROPE_AXIS, 2, dtype=jnp.float32) / ROPE_AXIS)
    ang = pos.astype(jnp.float32)[:, None] * inv[None, :]
    return jnp.cos(ang), jnp.sin(ang)


def _rotate(x, cos, sin):
    half = x.shape[-1] // 2
    x1, x2 = x[..., :half], x[..., half:]
    return jnp.concatenate([x1 * cos - x2 * sin, x1 * sin + x2 * cos], axis=-1)


def apply_rope_2d(x, rope):
    cos_r, sin_r, cos_c, sin_c = rope
    ex = (1,) * (x.ndim - 3)

    def shp(a):
        return a.reshape((a.shape[0],) + ex + (a.shape[1],))

    xf = x.astype(jnp.float32)
    out = jnp.concatenate([_rotate(xf[..., :ROPE_AXIS], shp(cos_r), shp(sin_r)),
                           _rotate(xf[..., ROPE_AXIS:], shp(cos_c), shp(sin_c))], axis=-1)
    return out.astype(x.dtype)


def attend_block(q, k, v, mask, sink):
    s = jnp.einsum('bqngd,bknd->bngqk', q, k).astype(jnp.float32) * ATTN_SCALE
    if mask is not None:
        s = jnp.where(mask, s, NEG_INF)
    sk = jnp.broadcast_to(sink.astype(jnp.float32)[None, :, :, None, None], s.shape[:-1] + (1,))
    p = jax.nn.softmax(jnp.concatenate([s, sk], axis=-1), axis=-1)[..., :-1]
    return jnp.einsum('bngqk,bknd->bqngd', p.astype(v.dtype), v)


def context_attention(q, k, v, sink):
    b, t = q.shape[:2]
    nb = t // Q_BLOCK
    qb = jnp.moveaxis(q.reshape(b, nb, Q_BLOCK, KV_HEADS, GQA_GROUP, HEAD_DIM), 1, 0)
    ob = lax.map(lambda qq: attend_block(qq, k, v, None, sink), qb)
    return jnp.moveaxis(ob, 0, 1).reshape(b, t, A_WIDTH)


def latent_attention(q, k, v, k_ctx, v_ctx, sink):
    b, t = q.shape[:2]
    nb = t // Q_BLOCK
    span = Q_BLOCK + 2 * WINDOW
    pad = ((0, 0), (WINDOW, WINDOW), (0, 0), (0, 0))
    kp = jnp.pad(k, pad)
    vp = jnp.pad(v, pad)
    k_ctx = k_ctx.astype(k.dtype)
    v_ctx = v_ctx.astype(v.dtype)
    ctx_mask = jnp.ones((Q_BLOCK, k_ctx.shape[1]), dtype=bool)
    qb = jnp.moveaxis(q.reshape(b, nb, Q_BLOCK, KV_HEADS, GQA_GROUP, HEAD_DIM), 1, 0)

    def one(args):
        qq, bi = args
        start = bi * Q_BLOCK
        kw = lax.dynamic_slice_in_dim(kp, start, span, axis=1)
        vw = lax.dynamic_slice_in_dim(vp, start, span, axis=1)
        qpos = start + jnp.arange(Q_BLOCK)
        kpos = start - WINDOW + jnp.arange(span)
        win = (jnp.abs(qpos[:, None] - kpos[None, :]) <= WINDOW) & ((kpos >= 0) & (kpos < t))[None, :]
        mask = jnp.concatenate([win, ctx_mask], axis=1)
        return attend_block(qq, jnp.concatenate([kw, k_ctx], axis=1), jnp.concatenate([vw, v_ctx], axis=1), mask, sink)

    ob = lax.map(one, (qb, jnp.arange(nb)))
    return jnp.moveaxis(ob, 0, 1).reshape(b, t, A_WIDTH)


def mlstm_chunkwise(q, k, v, ig, lf, C0, n0, m0):
    b, t, h, dk = q.shape
    dv = v.shape[-1]
    nc = t // M_CHUNK

    def chunks(a):
        return jnp.moveaxis(a.reshape((b, nc, M_CHUNK) + a.shape[2:]), 1, 0)

    tri = jnp.tril(jnp.ones((M_CHUNK, M_CHUNK), dtype=bool))[None, :, :, None]

    def step(carry, xs):
        C, n, m = carry
        qc, kc, vc, ic, fc = xs
        cb = jnp.cumsum(fc, axis=1)
        log_d = jnp.where(tri, cb[:, :, None, :] - cb[:, None, :, :] + ic[:, None, :, :], -jnp.inf)
        log_p = cb + m[:, None, :]
        mt = jnp.maximum(log_p, jnp.max(log_d, axis=2))
        dmat = jnp.exp(log_d - mt[:, :, None, :])
        wp = jnp.exp(log_p - mt)
        s = jnp.einsum('bthd,bshd->btsh', qc, kc) * dmat
        num = jnp.einsum('btsh,bshv->bthv', s, vc) + wp[..., None] * jnp.einsum('bthd,bhdv->bthv', qc, C)
        den = jnp.sum(s, axis=2) + wp * jnp.einsum('bthd,bhd->bth', qc, n)
        hc = num / jnp.maximum(jnp.abs(den), jnp.exp(-mt))[..., None]
        bl = cb[:, -1]
        log_e = bl[:, None, :] - cb + ic
        m_new = jnp.maximum(bl + m, jnp.max(log_e, axis=1))
        we = jnp.exp(log_e - m_new[:, None, :])
        dec = jnp.exp(bl + m - m_new)
        kw = kc * we[..., None]
        C_new = dec[..., None, None] * C + jnp.einsum('bshd,bshv->bhdv', kw, vc)
        n_new = dec[..., None] * n + jnp.sum(kw, axis=1)
        return (C_new, n_new, m_new), hc

    xs = (chunks(q), chunks(k), chunks(v), chunks(ig), chunks(lf))
    (C, n, m), hs = lax.scan(step, (C0, n0, m0), xs)
    return jnp.moveaxis(hs, 0, 1).reshape(b, t, h, dv), C, n, m


def mlstm_mix(qm, km, vm, om, gm, gate_b, norm_g, C0, n0, m0):
    f32 = jnp.float32
    b, t, _ = qm.shape
    q = qm.astype(f32).reshape(b, t, M_HEADS, M_DK)
    k = km.astype(f32).reshape(b, t, M_HEADS, M_DK) * (M_DK ** -0.5)
    v = vm.astype(f32).reshape(b, t, M_HEADS, M_DV)
    g = gm.astype(f32).reshape(b, t, 2, 2, M_HEADS) + gate_b.astype(f32)
    outs, Cs, ns, ms = [], [], [], []
    for d in range(2):
        ig = g[:, :, d, 0]
        lf = jax.nn.log_sigmoid(g[:, :, d, 1])
        hd, Cd, nd, md = mlstm_chunkwise(_flip(q, d), _flip(k, d), _flip(v, d), _flip(ig, d), _flip(lf, d),
                                         C0[d].astype(f32), n0[d].astype(f32), m0[d].astype(f32))
        outs.append(_flip(hd, d))
        Cs.append(Cd)
        ns.append(nd)
        ms.append(md)
    hsum = outs[0] + outs[1]
    hsum = hsum * lax.rsqrt(jnp.mean(hsum * hsum, axis=-1, keepdims=True) + NORM_EPS) * norm_g.astype(f32).reshape(M_HEADS, M_DV)
    out = hsum.reshape(b, t, M_WIDTH) * jax.nn.sigmoid(om.astype(f32))
    return out, jnp.stack(Cs, axis=0), jnp.stack(ns, axis=0), jnp.stack(ms, axis=0)


def conv_centred(x, w, b):
    y = lax.conv_general_dilated(x, w[:, None, :].astype(x.dtype), window_strides=(1,),
                                 padding=[(CONV_LEFT, CONV_W - 1 - CONV_LEFT)],
                                 dimension_numbers=('NWC', 'WIO', 'NWC'), feature_group_count=x.shape[-1])
    return y + b.astype(x.dtype)


def rglru_coeffs(x, wr, br, wi, bi, lam):
    f32 = jnp.float32
    b, t, _ = x.shape
    xb = x.reshape(b, t, R_BLOCKS, R_BW)
    r = jax.nn.sigmoid(jnp.einsum('btnc,ncd->btnd', xb, wr.astype(f32)).reshape(b, t, R_WIDTH) + br.astype(f32))
    i = jax.nn.sigmoid(jnp.einsum('btnc,ncd->btnd', xb, wi.astype(f32)).reshape(b, t, R_WIDTH) + bi.astype(f32))
    log_a = -RG_C * r * jax.nn.softplus(-lam.astype(f32))
    a = jnp.exp(log_a)
    u = jnp.sqrt(-jnp.expm1(2.0 * log_a)) * (i * x)
    return a, u


def _lin_combine(p, q):
    a1, b1 = p
    a2, b2 = q
    return a1 * a2, a2 * b1 + b2


def linear_scan(a, u, h0):
    a_cum, b_cum = lax.associative_scan(_lin_combine, (a, u), axis=1)
    h = a_cum * h0[:, None, :] + b_cum
    return h, h[:, -1]


def rglru_mix(xr, yr, conv_w, conv_b, wr, br, wi, bi, lam, h0):
    f32 = jnp.float32
    x = conv_centred(xr.astype(f32), conv_w.astype(f32), conv_b)
    outs, finals = [], []
    for d in range(2):
        a, u = rglru_coeffs(_flip(x, d), wr[d], br[d], wi[d], bi[d], lam[d])
        hd, h_last = linear_scan(a, u, h0[d].astype(f32))
        outs.append(_flip(hd, d))
        finals.append(h_last)
    out = (outs[0] + outs[1]) * jax.nn.gelu(yr.astype(f32))
    return out, jnp.stack(finals, axis=0)


def mixer_context(z, sink, gate_b, mnorm_g, conv_w, conv_b, wr, br, wi, bi, lam):
    b, t, _ = z.shape
    f32 = jnp.float32
    qa, ka, va, qm, km, vm, om, gm, xr, yr = split_in(z)
    q = qa.reshape(b, t, KV_HEADS, GQA_GROUP, HEAD_DIM)
    k = ka.reshape(b, t, KV_HEADS, HEAD_DIM)
    v = va.reshape(b, t, KV_HEADS, HEAD_DIM)
    a_out = context_attention(q, k, v, sink.reshape(KV_HEADS, GQA_GROUP))
    C0 = jnp.zeros((2, b, M_HEADS, M_DK, M_DV), f32)
    n0 = jnp.zeros((2, b, M_HEADS, M_DK), f32)
    m0 = jnp.zeros((2, b, M_HEADS), f32)
    m_out, C, n, m = mlstm_mix(qm, km, vm, om, gm, gate_b, mnorm_g, C0, n0, m0)
    r_out, h = rglru_mix(xr, yr, conv_w, conv_b, wr, br, wi, bi, lam, jnp.zeros((2, b, R_WIDTH), f32))
    out = jnp.concatenate([a_out, m_out.astype(z.dtype), r_out.astype(z.dtype)], axis=-1)
    return out, k, v, C, n, m, h


def mixer_latent(z, k_ctx, v_ctx, C0, n0, m0, h0, rope, sink, gate_b, mnorm_g, conv_w, conv_b, wr, br, wi, bi, lam):
    b, t, _ = z.shape
    qa, ka, va, qm, km, vm, om, gm, xr, yr = split_in(z)
    q = apply_rope_2d(qa.reshape(b, t, KV_HEADS, GQA_GROUP, HEAD_DIM), rope)
    k = apply_rope_2d(ka.reshape(b, t, KV_HEADS, HEAD_DIM), rope)
    v = va.reshape(b, t, KV_HEADS, HEAD_DIM)
    a_out = latent_attention(q, k, v, k_ctx, v_ctx, sink.reshape(KV_HEADS, GQA_GROUP))
    m_out, _, _, _ = mlstm_mix(qm, km, vm, om, gm, gate_b, mnorm_g, C0, n0, m0)
    r_out, _ = rglru_mix(xr, yr, conv_w, conv_b, wr, br, wi, bi, lam, h0)
    return jnp.concatenate([a_out, m_out.astype(z.dtype), r_out.astype(z.dtype)], axis=-1)


def hier_moe(x, wg, bg, we, be, w_gate, w_up, w_down):
    f32 = jnp.float32
    b, t, d = x.shape
    xf = x.reshape(b * t, d)
    n = xf.shape[0]
    xr = xf.astype(f32)
    g_logits = xr @ wg.astype(f32) + bg.astype(f32)
    _, g_sel = lax.top_k(g_logits, 1)
    g_w = jnp.take_along_axis(jax.nn.softmax(g_logits, axis=-1), g_sel, axis=1)
    e_logits = (xr @ we.astype(f32) + be.astype(f32)).reshape(n, N_GROUPS, EXPERTS_PER_GROUP)
    idx = jnp.broadcast_to(g_sel[:, :, None], (n, 1, EXPERTS_PER_GROUP))
    e_in = jnp.take_along_axis(e_logits, idx, axis=1)[:, 0]
    e_top, e_sel = lax.top_k(e_in, TOP_K)
    gate = jax.nn.softmax(e_top, axis=-1) * g_w
    eid = (g_sel * EXPERTS_PER_GROUP + e_sel).astype(jnp.int32)
    s_len = n * TOP_K
    flat_e = eid.reshape(s_len)
    order = jnp.argsort(flat_e)
    sorted_e = flat_e[order]
    counts = jnp.bincount(flat_e, length=N_EXPERTS)
    padded = (counts + MOE_BLOCK - 1) // MOE_BLOCK * MOE_BLOCK
    pad_end = jnp.cumsum(padded)
    pad_start = pad_end - padded
    start = jnp.cumsum(counts) - counts
    dest = pad_start[sorted_e] + jnp.arange(s_len) - start[sorted_e]
    n_blocks = s_len // MOE_BLOCK + N_EXPERTS
    buf_len = n_blocks * MOE_BLOCK
    slot_tok = jnp.full((buf_len,), n, dtype=jnp.int32).at[dest].set((order // TOP_K).astype(jnp.int32))
    blk_e = jnp.minimum(jnp.searchsorted(pad_end, jnp.arange(n_blocks) * MOE_BLOCK, side='right'), N_EXPERTS - 1)
    xpad = jnp.concatenate([xf, jnp.zeros((1, d), xf.dtype)], axis=0)
    xb = xpad[slot_tok].reshape(n_blocks, MOE_BLOCK, d)

    def run(args):
        xblk, e = args
        hid = jax.nn.silu(xblk @ w_gate[e]) * (xblk @ w_up[e])
        return hid @ w_down[e]

    yb = lax.map(run, (xb, blk_e)).reshape(buf_len, d)
    y_slot = jnp.zeros((s_len, d), yb.dtype).at[order].set(yb[dest])
    y = jnp.einsum('nkd,nk->nd', y_slot.reshape(n, TOP_K, d), gate.astype(yb.dtype))
    return y.reshape(b, t, d)


def setup_inputs(seed: int = 0) -> dict:
    key = jax.random.key(seed)
    ks = iter(jax.random.split(key, 48))
    f32 = jnp.float32

    def nrm(shape, scale):
        return jax.random.normal(next(ks), shape, f32) * scale

    def gain(shape):
        return 1.0 + nrm(shape, 0.02)

    d = D_MODEL
    inp = {}
    inp['x_prompt'] = nrm((BATCH, SEQ, d), 1.0)
    inp['x_sample'] = nrm((DEC_BATCH, DEC_SEQ, d), 1.0)
    inp['cache_k'] = nrm((DEC_BATCH, DEPTH, PAST_LEN, KV_HEADS, HEAD_DIM), 1.0)
    inp['cache_v'] = nrm((DEC_BATCH, DEPTH, PAST_LEN, KV_HEADS, HEAD_DIM), 1.0)
    inp['state_mlstm_C'] = nrm((DEC_BATCH, DEPTH, 2, M_HEADS, M_DK, M_DV), 0.5)
    inp['state_mlstm_n'] = nrm((DEC_BATCH, DEPTH, 2, M_HEADS, M_DK), 0.5)
    inp['state_mlstm_m'] = jax.random.uniform(next(ks), (DEC_BATCH, DEPTH, 2, M_HEADS), f32, 0.5, 3.0)
    inp['state_rglru_h'] = nrm((DEC_BATCH, DEPTH, 2, R_WIDTH), 0.5)
    inp['c'] = nrm((DEC_BATCH, d), 1.0)
    inp['c_ctx'] = nrm((d,), 1.0)
    inp['ada_w'] = nrm((DEPTH, d, 6 * d), 0.5 * d ** -0.5)
    inp['ada_b'] = nrm((DEPTH, 6 * d), 0.02)
    inp['norm1_g'] = gain((DEPTH, d))
    inp['w_in'] = nrm((DEPTH, d, D_IN), d ** -0.5)
    inp['attn_sink'] = nrm((DEPTH, N_HEADS), 0.5)
    i_b = nrm((DEPTH, 2, M_HEADS), 0.1)
    f_b = jnp.linspace(3.0, 6.0, M_HEADS, dtype=f32) + nrm((DEPTH, 2, M_HEADS), 0.1)
    inp['mlstm_gate_b'] = jnp.stack([i_b, f_b], axis=2)
    inp['mlstm_norm_g'] = gain((DEPTH, M_WIDTH))
    inp['rg_conv_w'] = nrm((DEPTH, CONV_W, R_WIDTH), CONV_W ** -0.5)
    inp['rg_conv_b'] = nrm((DEPTH, R_WIDTH), 0.02)
    inp['rg_wr'] = nrm((DEPTH, 2, R_BLOCKS, R_BW, R_BW), R_BW ** -0.5)
    inp['rg_br'] = nrm((DEPTH, 2, R_WIDTH), 0.02)
    inp['rg_wi'] = nrm((DEPTH, 2, R_BLOCKS, R_BW, R_BW), R_BW ** -0.5)
    inp['rg_bi'] = nrm((DEPTH, 2, R_WIDTH), 0.02)
    a0 = jax.random.uniform(next(ks), (DEPTH, 2, R_WIDTH), f32, 0.9, 0.999)
    s0 = a0 ** (1.0 / RG_C)
    inp['rg_lam'] = jnp.log(s0) - jnp.log1p(-s0)
    inp['w_out'] = nrm((DEPTH, D_MIX, d), D_MIX ** -0.5)
    inp['norm2_g'] = gain((DEPTH, d))
    inp['router_wg'] = nrm((DEPTH, d, N_GROUPS), d ** -0.5)
    inp['router_bg'] = nrm((DEPTH, N_GROUPS), 0.01)
    inp['router_we'] = nrm((DEPTH, d, N_EXPERTS), d ** -0.5)
    inp['router_be'] = nrm((DEPTH, N_EXPERTS), 0.01)
    inp['exp_w_gate'] = nrm((DEPTH, N_EXPERTS, d, D_EXPERT), d ** -0.5)
    inp['exp_w_up'] = nrm((DEPTH, N_EXPERTS, d, D_EXPERT), d ** -0.5)
    inp['exp_w_down'] = nrm((DEPTH, N_EXPERTS, D_EXPERT, d), D_EXPERT ** -0.5)
    inp['final_norm_g'] = gain((d,))
    return inp


def reference(x_prompt, x_sample, cache_k, cache_v, state_mlstm_C, state_mlstm_n, state_mlstm_m, state_rglru_h,
              c, c_ctx, ada_w, ada_b, norm1_g, w_in, attn_sink, mlstm_gate_b, mlstm_norm_g,
              rg_conv_w, rg_conv_b, rg_wr, rg_br, rg_wi, rg_bi, rg_lam, w_out, norm2_g,
              router_wg, router_bg, router_we, router_be, exp_w_gate, exp_w_up, exp_w_down, final_norm_g):
    t_lat = x_sample.shape[1]
    rows = t_lat // GRID_W
    row_pos = jnp.repeat(jnp.arange(rows), GRID_W)
    col_pos = jnp.tile(jnp.arange(GRID_W), rows)
    cos_r, sin_r = rope_table(row_pos)
    cos_c, sin_c = rope_table(col_pos)
    rope = (cos_r, sin_r, cos_c, sin_c)

    xp = x_prompt
    xs = x_sample
    new_k, new_v, new_C, new_n, new_m, new_h = [], [], [], [], [], []
    for l in range(DEPTH):
        mix_p = (attn_sink[l], mlstm_gate_b[l], mlstm_norm_g[l], rg_conv_w[l], rg_conv_b[l],
                 rg_wr[l], rg_br[l], rg_wi[l], rg_bi[l], rg_lam[l])
        moe_p = (router_wg[l], router_bg[l], router_we[l], router_be[l], exp_w_gate[l], exp_w_up[l], exp_w_down[l])
        mod_c = adaln(c_ctx[None, :], ada_w[l], ada_b[l])
        mod_s = adaln(c, ada_w[l], ada_b[l])

        zp = modulate(rmsnorm(xp, norm1_g[l]), mod_c[0], mod_c[1]) @ w_in[l]
        op, kc, vc, Cc, nc, mc, hc = mixer_context(zp, *mix_p)
        xp = xp + mod_c[2] * (op @ w_out[l])
        xp = xp + mod_c[5] * hier_moe(modulate(rmsnorm(xp, norm2_g[l]), mod_c[3], mod_c[4]), *moe_p)
        new_k.append(kc)
        new_v.append(vc)
        new_C.append(jnp.moveaxis(Cc, 0, 1))
        new_n.append(jnp.moveaxis(nc, 0, 1))
        new_m.append(jnp.moveaxis(mc, 0, 1))
        new_h.append(jnp.moveaxis(hc, 0, 1))

        zs = modulate(rmsnorm(xs, norm1_g[l]), mod_s[0], mod_s[1]) @ w_in[l]
        os_ = mixer_latent(zs, cache_k[:, l], cache_v[:, l],
                           jnp.moveaxis(state_mlstm_C[:, l], 1, 0), jnp.moveaxis(state_mlstm_n[:, l], 1, 0),
                           jnp.moveaxis(state_mlstm_m[:, l], 1, 0), jnp.moveaxis(state_rglru_h[:, l], 1, 0),
                           rope, *mix_p)
        xs = xs + mod_s[2] * (os_ @ w_out[l])
        xs = xs + mod_s[5] * hier_moe(modulate(rmsnorm(xs, norm2_g[l]), mod_s[3], mod_s[4]), *moe_p)

    y_prompt = rmsnorm(xp, final_norm_g)
    y_sample = rmsnorm(xs, final_norm_g)
    new_cache_k = jnp.stack(new_k, axis=1)
    new_cache_v = jnp.stack(new_v, axis=1)
    new_state_mlstm_C = jnp.stack(new_C, axis=1)
    new_state_mlstm_n = jnp.stack(new_n, axis=1)
    new_state_mlstm_m = jnp.stack(new_m, axis=1)
    new_state_rglru_h = jnp.stack(new_h, axis=1)
    return (y_prompt, y_sample, new_cache_k, new_cache_v, new_state_mlstm_C, new_state_mlstm_n, new_state_mlstm_m, new_state_rglru_h)
```

```python
import functools
import math

import jax
import jax.numpy as jnp
from jax import lax
from jax.experimental import pallas as pl
from jax.experimental.pallas import tpu as pltpu

F32 = jnp.float32
BF16 = jnp.bfloat16

D_MODEL = 2048
HEAD_DIM = 128
A_WIDTH = D_MODEL // 2
N_HEADS = A_WIDTH // HEAD_DIM
KV_HEADS = 2
GQA_GROUP = N_HEADS // KV_HEADS
KV_WIDTH = KV_HEADS * HEAD_DIM
WINDOW = 128
Q_BLOCK = 128
GRID_W = 64
ROPE_BASE = 10000.0
ROPE_AXIS = HEAD_DIM // 2
ATTN_SCALE = HEAD_DIM ** -0.5
NEG_INF = -1e30
M_WIDTH = D_MODEL // 4
M_HEADS = 4
M_DK = M_WIDTH // M_HEADS
M_GATES = 2 * 2 * M_HEADS
R_WIDTH = D_MODEL // 4
R_BLOCKS = 4
R_BW = R_WIDTH // R_BLOCKS
CONV_W = 4
RG_C = 8.0
N_GROUPS = 4
EXPERTS_PER_GROUP = 8
N_EXPERTS = N_GROUPS * EXPERTS_PER_GROUP
TOP_K = 2
D_EXPERT = D_MODEL // 2
NORM_EPS = 1e-6

LANES = 128
SUBLANES = 8
Z_MAIN = A_WIDTH + 2 * KV_WIDTH + 4 * M_WIDTH + 2 * R_WIDTH
_C_Q, _C_K, _C_V = 0, A_WIDTH, A_WIDTH + KV_WIDTH
_C_QM = A_WIDTH + 2 * KV_WIDTH
_C_KM, _C_VM, _C_OM = _C_QM + M_WIDTH, _C_QM + 2 * M_WIDTH, _C_QM + 3 * M_WIDTH
_C_XR, _C_YR = _C_QM + 4 * M_WIDTH, _C_QM + 4 * M_WIDTH + R_WIDTH

M_CHUNK = 128
R_CHUNK = 256
MOE_TB = 512
VMEM_LIMIT = 56 * 1024 * 1024


def _cparams(sem):
    return pltpu.CompilerParams(dimension_semantics=sem, vmem_limit_bytes=VMEM_LIMIT)


def _group_of_tile(row0, nc, tl):
    return jnp.where(row0 < nc, 0, 1 + (jnp.maximum(row0 - nc, 0)) // tl)


def _adaln_kernel(c_ref, w_ref, b_ref, o_ref):
    c = c_ref[...]
    s = (c * jax.nn.sigmoid(c)).astype(BF16)
    o_ref[...] = jnp.dot(s, w_ref[...].astype(BF16), preferred_element_type=F32) + b_ref[...]


def adaln_all(cvec, ada_w, ada_b, tn=1024):
    depth, d, d6 = ada_w.shape
    return pl.pallas_call(
        _adaln_kernel,
        out_shape=jax.ShapeDtypeStruct((depth, SUBLANES, d6), F32),
        grid=(depth, d6 // tn),
        in_specs=[
            pl.BlockSpec((SUBLANES, d), lambda l, j: (0, 0)),
            pl.BlockSpec((None, d, tn), lambda l, j: (l, 0, j)),
            pl.BlockSpec((None, 1, tn), lambda l, j: (l, 0, j)),
        ],
        out_specs=pl.BlockSpec((None, SUBLANES, tn), lambda l, j: (l, 0, j)),
        compiler_params=_cparams(("arbitrary", "arbitrary")),
        name="adaln",
    )(cvec, ada_w, ada_b.reshape(depth, 1, d6))


def _in_kernel(x_ref, g_ref, sh_ref, sc_ref, w_ref, wg_ref, z_ref, zg_ref, xn_ref):
    @pl.when(pl.program_id(1) == 0)
    def _():
        x = x_ref[...]
        y = x * lax.rsqrt(jnp.mean(x * x, axis=-1, keepdims=True) + NORM_EPS) * g_ref[...]
        xn = (y * (1.0 + sc_ref[...]) + sh_ref[...]).astype(BF16)
        xn_ref[...] = xn
        zg_ref[...] = jnp.dot(xn, wg_ref[...], preferred_element_type=F32)

    z_ref[...] = jnp.dot(xn_ref[...], w_ref[...], preferred_element_type=F32)


def in_proj(x, norm_g, mod, w_main, w_gate, nc, tl, tm=1024, tn=768):
    n, d = x.shape
    zw = w_main.shape[1]
    grp = lambda i: _group_of_tile(i * tm, nc, tl)
    return pl.pallas_call(
        _in_kernel,
        out_shape=(jax.ShapeDtypeStruct((n, zw), F32), jax.ShapeDtypeStruct((n, LANES), F32)),
        grid=(n // tm, zw // tn),
        in_specs=[
            pl.BlockSpec((tm, d), lambda i, j: (i, 0)),
            pl.BlockSpec((1, d), lambda i, j: (0, 0)),
            pl.BlockSpec((None, None, 1, d), lambda i, j: (grp(i), 0, 0, 0)),
            pl.BlockSpec((None, None, 1, d), lambda i, j: (grp(i), 1, 0, 0)),
            pl.BlockSpec((d, tn), lambda i, j: (0, j)),
            pl.BlockSpec((d, LANES), lambda i, j: (0, 0)),
        ],
        out_specs=(pl.BlockSpec((tm, tn), lambda i, j: (i, j)), pl.BlockSpec((tm, LANES), lambda i, j: (i, 0))),
        scratch_shapes=[pltpu.VMEM((tm, d), BF16)],
        compiler_params=_cparams(("arbitrary", "arbitrary")),
        name="in_proj",
    )(x, norm_g.reshape(1, d), mod, mod, w_main, w_gate)


def _rope(x, c, s):
    lane = lax.broadcasted_iota(jnp.int32, x.shape, 1)
    half = ROPE_AXIS // 2
    partner = jnp.where((lane & (ROPE_AXIS - 1)) < half, pltpu.roll(x, HEAD_DIM - half, 1), pltpu.roll(x, half, 1))
    return x * c + partner * s


def _softmax_pv(parts, sink_col):
    m = sink_col
    for s, _ in parts:
        m = jnp.maximum(m, jnp.max(s, axis=-1, keepdims=True))
    den = jnp.exp(sink_col - m)
    acc = None
    for s, v in parts:
        p = jnp.exp(s - m)
        den = den + jnp.sum(p, axis=-1, keepdims=True)
        pv = jnp.dot(p.astype(BF16), v, preferred_element_type=F32)
        acc = pv if acc is None else acc + pv
    return acc / den


def _qk(q, k):
    return lax.dot_general(q, k, (((1,), (1,)), ((), ())), preferred_element_type=F32) * ATTN_SCALE


def _sink_column(sink_ref, n, rows):
    ridx = lax.broadcasted_iota(jnp.int32, (GQA_GROUP * rows, 1), 0)
    col = jnp.full((GQA_GROUP * rows, 1), sink_ref[n * GQA_GROUP], F32)
    for g in range(1, GQA_GROUP):
        col = jnp.where(ridx >= g * rows, sink_ref[n * GQA_GROUP + g], col)
    return col


def _attn_ctx_kernel(sink_ref, q_ref, k_ref, v_ref, o_ref):
    t = q_ref.shape[0]
    for n in range(KV_HEADS):
        k = k_ref[:, n * HEAD_DIM:(n + 1) * HEAD_DIM].astype(BF16)
        v = v_ref[:, n * HEAD_DIM:(n + 1) * HEAD_DIM].astype(BF16)
        q = jnp.concatenate(
            [q_ref[:, (n * GQA_GROUP + g) * HEAD_DIM:(n * GQA_GROUP + g + 1) * HEAD_DIM] for g in range(GQA_GROUP)],
            axis=0).astype(BF16)
        out = _softmax_pv([(_qk(q, k), v)], _sink_column(sink_ref, n, t))
        for g in range(GQA_GROUP):
            h = n * GQA_GROUP + g
            o_ref[:, h * HEAD_DIM:(h + 1) * HEAD_DIM] = out[g * t:(g + 1) * t].astype(o_ref.dtype)


def attention_context(z, sink, b, t):
    return pl.pallas_call(
        _attn_ctx_kernel,
        out_shape=jax.ShapeDtypeStruct((b * t, A_WIDTH), BF16),
        grid=(b,),
        in_specs=[
            pl.BlockSpec(memory_space=pltpu.SMEM),
            pl.BlockSpec((t, A_WIDTH), lambda i: (i, _C_Q // A_WIDTH)),
            pl.BlockSpec((t, KV_WIDTH), lambda i: (i, _C_K // KV_WIDTH)),
            pl.BlockSpec((t, KV_WIDTH), lambda i: (i, _C_V // KV_WIDTH)),
        ],
        out_specs=pl.BlockSpec((t, A_WIDTH), lambda i: (i, 0)),
        compiler_params=_cparams(("arbitrary",)),
        name="attn_ctx",
    )(sink, z, z, z)


def _attn_lat_kernel(sink_ref, q_ref, k_ref, v_ref, kc_ref, vc_ref, cq_ref, sq_ref, ck_ref, sk_ref, o_ref,
                     kr_ref, vp_ref, *, t):
    i = pl.program_id(1)
    rope_rows = 512

    @pl.when(i == 0)
    def _():
        zpad = jnp.zeros((WINDOW, KV_WIDTH), BF16)
        kr_ref[0:WINDOW, :] = zpad
        kr_ref[WINDOW + t:2 * WINDOW + t, :] = zpad
        vp_ref[0:WINDOW, :] = zpad
        vp_ref[WINDOW + t:2 * WINDOW + t, :] = zpad

        def body(c, carry):
            r0 = pl.multiple_of(c * rope_rows, rope_rows)
            cs, sn = ck_ref[pl.ds(r0, rope_rows), :], sk_ref[pl.ds(r0, rope_rows), :]
            for n in range(KV_HEADS):
                kk = k_ref[pl.ds(r0, rope_rows), n * HEAD_DIM:(n + 1) * HEAD_DIM]
                kr_ref[pl.ds(WINDOW + r0, rope_rows), n * HEAD_DIM:(n + 1) * HEAD_DIM] = _rope(kk, cs, sn).astype(BF16)
            vp_ref[pl.ds(WINDOW + r0, rope_rows), :] = v_ref[pl.ds(r0, rope_rows), :].astype(BF16)
            return carry

        lax.fori_loop(0, t // rope_rows, body, 0)

    span = Q_BLOCK + 2 * WINDOW
    rows = GQA_GROUP * Q_BLOCK
    r = lax.broadcasted_iota(jnp.int32, (rows, span), 0) & (Q_BLOCK - 1)
    c = lax.broadcasted_iota(jnp.int32, (rows, span), 1)
    kpos = (i - 1) * Q_BLOCK + c
    mask = (c >= r) & (c <= r + 2 * WINDOW) & (kpos >= 0) & (kpos < t)
    w0 = pl.multiple_of(i * Q_BLOCK, Q_BLOCK)
    cq, sq = cq_ref[...], sq_ref[...]
    for n in range(KV_HEADS):
        hs = slice(n * HEAD_DIM, (n + 1) * HEAD_DIM)
        q = jnp.concatenate(
            [_rope(q_ref[:, (n * GQA_GROUP + g) * HEAD_DIM:(n * GQA_GROUP + g + 1) * HEAD_DIM], cq, sq)
             for g in range(GQA_GROUP)], axis=0).astype(BF16)
        s_win = jnp.where(mask, _qk(q, kr_ref[pl.ds(w0, span), hs]), NEG_INF)
        s_ctx = _qk(q, kc_ref[:, hs].astype(BF16))
        out = _softmax_pv([(s_win, vp_ref[pl.ds(w0, span), hs]), (s_ctx, vc_ref[:, hs].astype(BF16))],
                          _sink_column(sink_ref, n, Q_BLOCK))
        for g in range(GQA_GROUP):
            h = n * GQA_GROUP + g
            o_ref[:, h * HEAD_DIM:(h + 1) * HEAD_DIM] = out[g * Q_BLOCK:(g + 1) * Q_BLOCK].astype(o_ref.dtype)


def attention_latent(z, cache_k, cache_v, sink, rope_c, rope_s, row0, b, t):
    nqb = t // Q_BLOCK
    past = cache_k.shape[1]
    qb0 = row0 // Q_BLOCK
    tb0 = row0 // t
    kern = functools.partial(_attn_lat_kernel, t=t)
    return pl.pallas_call(
        kern,
        out_shape=jax.ShapeDtypeStruct((b * t, A_WIDTH), BF16),
        grid=(b, nqb),
        in_specs=[
            pl.BlockSpec(memory_space=pltpu.SMEM),
            pl.BlockSpec((Q_BLOCK, A_WIDTH), lambda bi, i: (qb0 + bi * nqb + i, _C_Q // A_WIDTH)),
            pl.BlockSpec((t, KV_WIDTH), lambda bi, i: (tb0 + bi, _C_K // KV_WIDTH)),
            pl.BlockSpec((t, KV_WIDTH), lambda bi, i: (tb0 + bi, _C_V // KV_WIDTH)),
            pl.BlockSpec((None, past, KV_WIDTH), lambda bi, i: (bi, 0, 0)),
            pl.BlockSpec((None, past, KV_WIDTH), lambda bi, i: (bi, 0, 0)),
            pl.BlockSpec((Q_BLOCK, HEAD_DIM), lambda bi, i: (i, 0)),
            pl.BlockSpec((Q_BLOCK, HEAD_DIM), lambda bi, i: (i, 0)),
            pl.BlockSpec((t, HEAD_DIM), lambda bi, i: (0, 0)),
            pl.BlockSpec((t, HEAD_DIM), lambda bi, i: (0, 0)),
        ],
        out_specs=pl.BlockSpec((Q_BLOCK, A_WIDTH), lambda bi, i: (bi * nqb + i, 0)),
        scratch_shapes=[pltpu.VMEM((t + 2 * WINDOW, KV_WIDTH), BF16), pltpu.VMEM((t + 2 * WINDOW, KV_WIDTH), BF16)],
        compiler_params=_cparams(("arbitrary", "arbitrary")),
        name="attn_lat",
    )(sink, z, z, z, cache_k, cache_v, rope_c, rope_s, rope_c, rope_s)


def rope_tables(t):
    pos = jnp.arange(t)
    inv = ROPE_BASE ** (-jnp.arange(0, ROPE_AXIS, 2, dtype=F32) / ROPE_AXIS)

    def cs(p):
        ang = p.astype(F32)[:, None] * inv[None, :]
        return jnp.cos(ang), jnp.sin(ang)

    cr, sr = cs(pos // GRID_W)
    cc, sc = cs(pos % GRID_W)
    return (jnp.concatenate([cr, cr, cc, cc], axis=-1), jnp.concatenate([-sr, sr, -sc, sc], axis=-1))


class _Seqs:
    def __init__(self, n_ctx, t_ctx, n_lat, t_lat, chunk, reverse):
        self.n_ctx, self.t_ctx, self.n_lat, self.t_lat = n_ctx, t_ctx, n_lat, t_lat
        self.chunk, self.reverse = chunk, reverse
        self.cpc, self.cpl = t_ctx // chunk, t_lat // chunk
        self.ctx_chunks = n_ctx * self.cpc
        self.n_chunks = self.ctx_chunks + n_lat * self.cpl

    def chunk_of_step(self, s):
        return (self.n_chunks - 1 - s) if self.reverse else s

    def info(self, g):
        is_ctx = g < self.ctx_chunks
        gl = jnp.maximum(g - self.ctx_chunks, 0)
        gc = jnp.minimum(g, self.ctx_chunks - 1)
        pos = jnp.where(is_ctx, gc % self.cpc, gl % self.cpl)
        per = jnp.where(is_ctx, self.cpc, self.cpl)
        lat = gl // self.cpl
        seq = jnp.where(is_ctx, gc // self.cpc, self.n_ctx + lat)
        head, tail = pos == 0, pos == per - 1
        return is_ctx, seq, lat, (tail if self.reverse else head), (head if self.reverse else tail)


def _lane_scan(x, op, reverse):
    n = x.shape[1]
    lane = lax.broadcasted_iota(jnp.int32, x.shape, 1)
    s = 1
    while s < n:
        if reverse:
            x = jnp.where(lane < n - s, op(x, pltpu.roll(x, n - s, 1)), x)
        else:
            x = jnp.where(lane >= s, op(x, pltpu.roll(x, s, 1)), x)
        s *= 2
    return x


def _mlstm_kernel(q_ref, k_ref, v_ref, gm_ref, gb_ref, c0_ref, n0_ref, m0_ref,
                  h_ref, cf_ref, nf_ref, mf_ref, c_s, n_s, m_s, *, seqs, direction):
    L = seqs.chunk
    rev = seqs.reverse
    g = seqs.chunk_of_step(pl.program_id(0))
    is_ctx, _, _, first, last = seqs.info(g)

    @pl.when(first)
    def _():
        keep = jnp.where(is_ctx, 0.0, 1.0).astype(F32)
        c_s[...] = c0_ref[...] * keep
        n_s[...] = n0_ref[...] * keep
        m_s[...] = m0_ref[...] * keep

    gt = (gm_ref[...] + gb_ref[...]).T
    slab = gt[direction * SUBLANES:(direction + 1) * SUBLANES]
    lf = jax.nn.log_sigmoid(slab)
    cb = pltpu.roll(_lane_scan(lf, jnp.add, rev), M_HEADS, 0)
    r = slab - cb
    m_prev = m_s[...]
    big_m = jnp.maximum(m_prev, _lane_scan(r, jnp.maximum, rev))
    stack = jnp.concatenate([r, big_m, cb, jnp.zeros((LANES - 3 * SUBLANES, L), F32)], axis=0)
    cols = stack.T
    end = 0 if rev else L - 1
    ti = lax.broadcasted_iota(jnp.int32, (L, L), 0)
    si = lax.broadcasted_iota(jnp.int32, (L, L), 1)
    causal = (si >= ti) if rev else (si <= ti)
    scale = M_DK ** -0.5
    for h in range(M_HEADS):
        hs = slice(h * M_DK, (h + 1) * M_DK)
        q = q_ref[:, hs].astype(BF16)
        kf = k_ref[:, hs] * scale
        v = v_ref[:, hs].astype(BF16)
        r_row = r[h:h + 1, :]
        r_col = cols[:, h:h + 1]
        m_col = cols[:, SUBLANES + h:SUBLANES + h + 1]
        cb_col = cols[:, 2 * SUBLANES + h:2 * SUBLANES + h + 1]
        m_old = m_prev[h:h + 1, 0:1]
        dmat = jnp.where(causal, jnp.exp(jnp.where(causal, r_row - m_col, 0.0)), 0.0)
        s = lax.dot_general(q, kf.astype(BF16), (((1,), (1,)), ((), ())), preferred_element_type=F32) * dmat
        wp = jnp.exp(m_old - m_col)
        c_old = c_s[h]
        n_old = n_s[h:h + 1, :]
        num = jnp.dot(s.astype(BF16), v, preferred_element_type=F32) + wp * jnp.dot(
            q, c_old.astype(BF16), preferred_element_type=F32)
        qn = jnp.sum(q.astype(F32) * n_old.astype(BF16).astype(F32), axis=-1, keepdims=True)
        den = jnp.sum(s, axis=-1, keepdims=True) + wp * qn
        h_ref[:, hs] = num / jnp.maximum(jnp.abs(den), jnp.exp(-(cb_col + m_col)))
        m_end = m_col[end:end + 1, :]
        bl = cb_col[end:end + 1, :]
        kw = kf * jnp.exp(r_col - m_end)
        dec = jnp.exp(m_old - m_end)
        c_s[h] = dec * c_old + lax.dot_general(kw.astype(BF16), v, (((0,), (0,)), ((), ())),
                                               preferred_element_type=F32)
        n_s[h:h + 1, :] = dec * n_old + jnp.sum(kw, axis=0, keepdims=True)
        m_s[h:h + 1, :] = jnp.broadcast_to(bl + m_end, (1, LANES))

    @pl.when(last)
    def _():
        cf_ref[...] = c_s[...]
        nf_ref[...] = n_s[...]
        mf_ref[...] = m_s[...]


def mlstm_direction(z, zg, gate_b, c0, n0, m0, direction, n_ctx, t_ctx, n_lat, t_lat):
    seqs = _Seqs(n_ctx, t_ctx, n_lat, t_lat, M_CHUNK, direction == 1)
    n_seq = n_ctx + n_lat
    L = M_CHUNK
    n_rows = seqs.n_chunks * L
    cm = lambda s: seqs.chunk_of_step(s)
    lat_of = lambda s: seqs.info(cm(s))[2]
    seq_of = lambda s: seqs.info(cm(s))[1]
    kern = functools.partial(_mlstm_kernel, seqs=seqs, direction=direction)
    wcol = M_WIDTH
    return pl.pallas_call(
        kern,
        out_shape=(
            jax.ShapeDtypeStruct((n_rows, M_WIDTH), F32),
            jax.ShapeDtypeStruct((n_seq, M_HEADS, M_DK, M_DK), F32),
            jax.ShapeDtypeStruct((n_seq, SUBLANES, M_DK), F32),
            jax.ShapeDtypeStruct((n_seq, SUBLANES, LANES), F32),
        ),
        grid=(seqs.n_chunks,),
        in_specs=[
            pl.BlockSpec((L, wcol), lambda s: (cm(s), _C_QM // wcol)),
            pl.BlockSpec((L, wcol), lambda s: (cm(s), _C_KM // wcol)),
            pl.BlockSpec((L, wcol), lambda s: (cm(s), _C_VM // wcol)),
            pl.BlockSpec((L, LANES), lambda s: (cm(s), 0)),
            pl.BlockSpec((1, LANES), lambda s: (0, 0)),
            pl.BlockSpec((None, M_HEADS, M_DK, M_DK), lambda s: (lat_of(s), 0, 0, 0)),
            pl.BlockSpec((None, SUBLANES, M_DK), lambda s: (lat_of(s), 0, 0)),
            pl.BlockSpec((None, SUBLANES, LANES), lambda s: (lat_of(s), 0, 0)),
        ],
        out_specs=(
            pl.BlockSpec((L, wcol), lambda s: (cm(s), 0)),
            pl.BlockSpec((None, M_HEADS, M_DK, M_DK), lambda s: (seq_of(s), 0, 0, 0)),
            pl.BlockSpec((None, SUBLANES, M_DK), lambda s: (seq_of(s), 0, 0)),
            pl.BlockSpec((None, SUBLANES, LANES), lambda s: (seq_of(s), 0, 0)),
        ),
        scratch_shapes=[pltpu.VMEM((M_HEADS, M_DK, M_DK), F32), pltpu.VMEM((SUBLANES, M_DK), F32),
                        pltpu.VMEM((SUBLANES, LANES), F32)],
        compiler_params=_cparams(("arbitrary",)),
        name=f"mlstm_d{direction}",
    )(z, z, z, zg, gate_b, c0, n0, m0)


def _rglru_kernel(x_ref, xp_ref, xn_ref, cw_ref, cb_ref, wr_ref, br_ref, wi_ref, bi_ref, lam_ref, h0_ref,
                  h_ref, hf_ref, xpad, carry, *, seqs):
    L = seqs.chunk
    rev = seqs.reverse
    g = seqs.chunk_of_step(pl.program_id(0))
    is_ctx, _, _, first, last = seqs.info(g)
    head = last if rev else first
    tail = first if rev else last
    halo = SUBLANES
    xpad[0:halo, :] = xp_ref[...] * jnp.where(head, 0.0, 1.0).astype(F32)
    xpad[halo:halo + L, :] = x_ref[...]
    xpad[halo + L:2 * halo + L, :] = xn_ref[...] * jnp.where(tail, 0.0, 1.0).astype(F32)
    x = cb_ref[...]
    for j in range(CONV_W):
        x = x + cw_ref[j:j + 1, :] * xpad[halo - 2 + j:halo - 2 + j + L, :]

    rs, is_ = [], []
    for n in range(R_BLOCKS):
        xb = x[:, n * R_BW:(n + 1) * R_BW].astype(BF16)
        rs.append(jnp.dot(xb, wr_ref[n].astype(BF16), preferred_element_type=F32))
        is_.append(jnp.dot(xb, wi_ref[n].astype(BF16), preferred_element_type=F32))
    rg = jax.nn.sigmoid(jnp.concatenate(rs, axis=-1) + br_ref[...])
    ig = jax.nn.sigmoid(jnp.concatenate(is_, axis=-1) + bi_ref[...])
    log_a = -RG_C * rg * jax.nn.softplus(-lam_ref[...])
    a = jnp.exp(log_a)
    u = jnp.sqrt(1.0 - jnp.exp(2.0 * log_a)) * (ig * x)

    row = lax.broadcasted_iota(jnp.int32, (L, R_WIDTH), 0)
    s = 1
    while s < L:
        if rev:
            ok = row < L - s
            a_sh, u_sh = pltpu.roll(a, L - s, 0), pltpu.roll(u, L - s, 0)
        else:
            ok = row >= s
            a_sh, u_sh = pltpu.roll(a, s, 0), pltpu.roll(u, s, 0)
        u = jnp.where(ok, a * u_sh + u, u)
        a = jnp.where(ok, a * a_sh, a)
        s *= 2

    @pl.when(first)
    def _():
        carry[...] = h0_ref[...] * jnp.where(is_ctx, 0.0, 1.0).astype(F32)

    h = a * carry[...] + u
    h_ref[...] = h
    end = 0 if rev else L - 1
    carry[...] = h[end:end + 1, :]

    @pl.when(last)
    def _():
        hf_ref[...] = h[end:end + 1, :]


def rglru_direction(z, conv_w, conv_b, wr, br, wi, bi, lam, h0, direction, n_ctx, t_ctx, n_lat, t_lat):
    seqs = _Seqs(n_ctx, t_ctx, n_lat, t_lat, R_CHUNK, direction == 1)
    n_seq = n_ctx + n_lat
    L = R_CHUNK
    n_rows = seqs.n_chunks * L
    hb = L // SUBLANES
    nb8 = n_rows // SUBLANES
    cm = lambda s: seqs.chunk_of_step(s)
    lat_of = lambda s: seqs.info(cm(s))[2]
    seq_of = lambda s: seqs.info(cm(s))[1]
    xcol = _C_XR // R_WIDTH
    vec = lambda: pl.BlockSpec((1, R_WIDTH), lambda s: (0, 0))
    kern = functools.partial(_rglru_kernel, seqs=seqs)
    return pl.pallas_call(
        kern,
        out_shape=(jax.ShapeDtypeStruct((n_rows, R_WIDTH), F32), jax.ShapeDtypeStruct((n_seq, 1, R_WIDTH), F32)),
        grid=(seqs.n_chunks,),
        in_specs=[
            pl.BlockSpec((L, R_WIDTH), lambda s: (cm(s), xcol)),
            pl.BlockSpec((SUBLANES, R_WIDTH), lambda s: (jnp.maximum(cm(s) * hb - 1, 0), xcol)),
            pl.BlockSpec((SUBLANES, R_WIDTH), lambda s: (jnp.minimum((cm(s) + 1) * hb, nb8 - 1), xcol)),
            pl.BlockSpec((CONV_W, R_WIDTH), lambda s: (0, 0)),
            vec(),
            pl.BlockSpec((R_BLOCKS, R_BW, R_BW), lambda s: (0, 0, 0)),
            vec(),
            pl.BlockSpec((R_BLOCKS, R_BW, R_BW), lambda s: (0, 0, 0)),
            vec(),
            vec(),
            pl.BlockSpec((None, 1, R_WIDTH), lambda s: (lat_of(s), 0, 0)),
        ],
        out_specs=(pl.BlockSpec((L, R_WIDTH), lambda s: (cm(s), 0)),
                   pl.BlockSpec((None, 1, R_WIDTH), lambda s: (seq_of(s), 0, 0))),
        scratch_shapes=[pltpu.VMEM((L + 2 * SUBLANES, R_WIDTH), F32), pltpu.VMEM((1, R_WIDTH), F32)],
        compiler_params=_cparams(("arbitrary",)),
        name=f"rglru_d{direction}",
    )(z, z, z, conv_w, conv_b.reshape(1, -1), wr, br.reshape(1, -1), wi, bi.reshape(1, -1), lam.reshape(1, -1), h0)


def _gelu_tanh(x):
    return 0.5 * x * (1.0 + jnp.tanh(math.sqrt(2.0 / math.pi) * (x + 0.044715 * (x * x * x))))


def _out_kernel(x_ref, a_ref, mf_ref, mb_ref, om_ref, rf_ref, rb_ref, yr_ref, mg_ref, gate_ref, w_ref,
                o_ref, mix_ref):
    @pl.when(pl.program_id(1) == 0)
    def _():
        mix_ref[:, 0:A_WIDTH] = a_ref[...]
        hs = mf_ref[...] + mb_ref[...]
        for h in range(M_HEADS):
            cs = slice(h * M_DK, (h + 1) * M_DK)
            hh = hs[:, cs]
            hn = hh * lax.rsqrt(jnp.mean(hh * hh, axis=-1, keepdims=True) + NORM_EPS) * mg_ref[:, cs]
            mix_ref[:, A_WIDTH + h * M_DK:A_WIDTH + (h + 1) * M_DK] = (
                hn * jax.nn.sigmoid(om_ref[:, cs])).astype(BF16)
        mix_ref[:, A_WIDTH + M_WIDTH:] = ((rf_ref[...] + rb_ref[...]) * _gelu_tanh(yr_ref[...])).astype(BF16)

    o_ref[...] = x_ref[...] + gate_ref[...] * jnp.dot(mix_ref[...], w_ref[...], preferred_element_type=F32)


def out_proj(x, a, mf, mb, rf, rb, z, mnorm_g, mod, w_out, nc, tl, tm=1024, tn=512):
    n, d = x.shape
    grp = lambda i: _group_of_tile(i * tm, nc, tl)
    rowblk = lambda w, col: pl.BlockSpec((tm, w), lambda i, j: (i, col))
    return pl.pallas_call(
        _out_kernel,
        out_shape=jax.ShapeDtypeStruct((n, d), F32),
        grid=(n // tm, d // tn),
        in_specs=[
            pl.BlockSpec((tm, tn), lambda i, j: (i, j)),
            rowblk(A_WIDTH, 0),
            rowblk(M_WIDTH, 0), rowblk(M_WIDTH, 0), rowblk(M_WIDTH, _C_OM // M_WIDTH),
            rowblk(R_WIDTH, 0), rowblk(R_WIDTH, 0), rowblk(R_WIDTH, _C_YR // R_WIDTH),
            pl.BlockSpec((1, M_WIDTH), lambda i, j: (0, 0)),
            pl.BlockSpec((None, None, 1, tn), lambda i, j: (grp(i), 2, 0, j)),
            pl.BlockSpec((d, tn), lambda i, j: (0, j)),
        ],
        out_specs=pl.BlockSpec((tm, tn), lambda i, j: (i, j)),
        scratch_shapes=[pltpu.VMEM((tm, d), BF16)],
        compiler_params=_cparams(("arbitrary", "arbitrary")),
        name="out_proj",
    )(x, a, mf, mb, z, rf, rb, z, mnorm_g.reshape(1, -1), mod, w_out)


def _moe_pre_kernel(x_ref, g_ref, sh_ref, sc_ref, wr_ref, br_ref, xn_ref, lg_ref):
    x = x_ref[...]
    y = x * lax.rsqrt(jnp.mean(x * x, axis=-1, keepdims=True) + NORM_EPS) * g_ref[...]
    xn = y * (1.0 + sc_ref[...]) + sh_ref[...]
    xn_ref[...] = xn.astype(BF16)
    lg_ref[...] = jnp.dot(xn, wr_ref[...], preferred_element_type=F32, precision=lax.Precision.HIGHEST) + br_ref[...]


def moe_pre(x, norm_g, mod, w_router, b_router, nc, tl, tm=512):
    n, d = x.shape
    grp = lambda i: _group_of_tile(i * tm, nc, tl)
    return pl.pallas_call(
        _moe_pre_kernel,
        out_shape=(jax.ShapeDtypeStruct((n, d), BF16), jax.ShapeDtypeStruct((n, LANES), F32)),
        grid=(n // tm,),
        in_specs=[
            pl.BlockSpec((tm, d), lambda i: (i, 0)),
            pl.BlockSpec((1, d), lambda i: (0, 0)),
            pl.BlockSpec((None, None, 1, d), lambda i: (grp(i), 3, 0, 0)),
            pl.BlockSpec((None, None, 1, d), lambda i: (grp(i), 4, 0, 0)),
            pl.BlockSpec((d, LANES), lambda i: (0, 0)),
            pl.BlockSpec((1, LANES), lambda i: (0, 0)),
        ],
        out_specs=(pl.BlockSpec((tm, d), lambda i: (i, 0)), pl.BlockSpec((tm, LANES), lambda i: (i, 0))),
        compiler_params=_cparams(("arbitrary",)),
        name="moe_pre",
    )(x, norm_g.reshape(1, d), mod, mod, w_router, b_router)


def _moe_up_kernel(be_ref, bs_ref, nu_ref, x_ref, wg_ref, wu_ref, h_ref):
    @pl.when(pl.program_id(1) < nu_ref[0])
    def _():
        x = x_ref[...]
        gt = jnp.dot(x, wg_ref[...].astype(BF16), preferred_element_type=F32)
        up = jnp.dot(x, wu_ref[...].astype(BF16), preferred_element_type=F32)
        h_ref[...] = (gt * jax.nn.sigmoid(gt) * up).astype(BF16)


def _moe_down_kernel(be_ref, bs_ref, nu_ref, h_ref, wd_ref, y_ref):
    @pl.when(pl.program_id(1) < nu_ref[0])
    def _():
        y_ref[...] = jnp.dot(h_ref[...], wd_ref[...].astype(BF16), preferred_element_type=F32)


def moe_experts(xb, blk_e, blk_src, n_used, w_gate, w_up, w_down, tb=MOE_TB, tc=512):
    rows, d = xb.shape
    n_blocks = rows // tb
    de = w_gate.shape[-1]
    h = pl.pallas_call(
        _moe_up_kernel,
        out_shape=jax.ShapeDtypeStruct((rows, de), BF16),
        grid_spec=pltpu.PrefetchScalarGridSpec(
            num_scalar_prefetch=3,
            grid=(de // tc, n_blocks),
            in_specs=[
                pl.BlockSpec((tb, d), lambda c, j, be, bs, nu: (bs[j], 0)),
                pl.BlockSpec((None, d, tc), lambda c, j, be, bs, nu: (be[j], 0, c)),
                pl.BlockSpec((None, d, tc), lambda c, j, be, bs, nu: (be[j], 0, c)),
            ],
            out_specs=pl.BlockSpec((tb, tc), lambda c, j, be, bs, nu: (bs[j], c)),
        ),
        compiler_params=_cparams(("arbitrary", "arbitrary")),
        name="moe_up",
    )(blk_e, blk_src, n_used, xb, w_gate, w_up)
    tc2 = 1024
    return pl.pallas_call(
        _moe_down_kernel,
        out_shape=jax.ShapeDtypeStruct((rows, d), F32),
        grid_spec=pltpu.PrefetchScalarGridSpec(
            num_scalar_prefetch=3,
            grid=(d // tc2, n_blocks),
            in_specs=[
                pl.BlockSpec((tb, de), lambda c, j, be, bs, nu: (bs[j], 0)),
                pl.BlockSpec((None, de, tc2), lambda c, j, be, bs, nu: (be[j], 0, c)),
            ],
            out_specs=pl.BlockSpec((tb, tc2), lambda c, j, be, bs, nu: (bs[j], c)),
        ),
        compiler_params=_cparams(("arbitrary", "arbitrary")),
        name="moe_down",
    )(blk_e, blk_src, n_used, h, w_down)


def _moe_combine_kernel(x_ref, y0_ref, y1_ref, gt_ref, gate_ref, o_ref):
    y = gt_ref[:, 0:1] * y0_ref[...] + gt_ref[:, 1:2] * y1_ref[...]
    o_ref[...] = x_ref[...] + gate_ref[...] * y


def moe_combine(x, y0, y1, gates, mod, nc, tl, tm=512):
    n, d = x.shape
    grp = lambda i: _group_of_tile(i * tm, nc, tl)
    blk = lambda: pl.BlockSpec((tm, d), lambda i: (i, 0))
    return pl.pallas_call(
        _moe_combine_kernel,
        out_shape=jax.ShapeDtypeStruct((n, d), F32),
        grid=(n // tm,),
        in_specs=[blk(), blk(), blk(), pl.BlockSpec((tm, LANES), lambda i: (i, 0)),
                  pl.BlockSpec((None, None, 1, d), lambda i: (grp(i), 5, 0, 0))],
        out_specs=blk(),
        compiler_params=_cparams(("arbitrary",)),
        name="moe_combine",
    )(x, y0, y1, gates, mod)


def moe_route(logits, tb=MOE_TB):
    n = logits.shape[0]
    g_logits = logits[:, :N_GROUPS]
    e_logits = logits[:, N_GROUPS:N_GROUPS + N_EXPERTS].reshape(n, N_GROUPS, EXPERTS_PER_GROUP)
    _, g_sel = lax.top_k(g_logits, 1)
    g_w = jnp.take_along_axis(jax.nn.softmax(g_logits, axis=-1), g_sel, axis=1)
    e_in = jnp.take_along_axis(e_logits, jnp.broadcast_to(g_sel[:, :, None], (n, 1, EXPERTS_PER_GROUP)), axis=1)[:, 0]
    e_top, e_sel = lax.top_k(e_in, TOP_K)
    gate = jax.nn.softmax(e_top, axis=-1) * g_w
    eid = (g_sel * EXPERTS_PER_GROUP + e_sel).astype(jnp.int32)
    s_len = n * TOP_K
    flat_e = eid.reshape(s_len)
    onehot = (flat_e[:, None] == jnp.arange(N_EXPERTS, dtype=jnp.int32)[None, :]).astype(jnp.int32)
    rank = jnp.take_along_axis(jnp.cumsum(onehot, axis=0) - onehot, flat_e[:, None], axis=1)[:, 0]
    counts = jnp.sum(onehot, axis=0)
    nblk_e = (counts + tb - 1) // tb
    blk_end = jnp.cumsum(nblk_e)
    blk_start = blk_end - nblk_e
    dest = blk_start[flat_e] * tb + rank
    n_blocks = s_len // tb + N_EXPERTS
    n_used = blk_end[-1]
    slot_tok = jnp.full((n_blocks * tb,), n, dtype=jnp.int32).at[dest].set(jnp.arange(s_len, dtype=jnp.int32) // TOP_K)
    j = jnp.arange(n_blocks, dtype=jnp.int32)
    jc = jnp.minimum(j, n_used - 1)
    blk_e = jnp.minimum(jnp.searchsorted(blk_end, jc, side='right'), N_EXPERTS - 1).astype(jnp.int32)
    return gate, dest.reshape(n, TOP_K), slot_tok, blk_e, jc.astype(jnp.int32), n_used.reshape(1).astype(jnp.int32)


def _final_norm_kernel(x_ref, g_ref, o_ref):
    x = x_ref[...]
    o_ref[...] = x * lax.rsqrt(jnp.mean(x * x, axis=-1, keepdims=True) + NORM_EPS) * g_ref[...]


def final_norm(x, g, tm=512):
    n, d = x.shape
    return pl.pallas_call(
        _final_norm_kernel,
        out_shape=jax.ShapeDtypeStruct((n, d), F32),
        grid=(n // tm,),
        in_specs=[pl.BlockSpec((tm, d), lambda i: (i, 0)), pl.BlockSpec((1, d), lambda i: (0, 0))],
        out_specs=pl.BlockSpec((tm, d), lambda i: (i, 0)),
        compiler_params=_cparams(("arbitrary",)),
        name="final_norm",
    )(x, g.reshape(1, d))


def _pad_rows(a, rows):
    return jnp.concatenate([a, jnp.zeros((rows - a.shape[0],) + a.shape[1:], a.dtype)], axis=0)


def kernel(x_prompt, x_sample, cache_k, cache_v, state_mlstm_C, state_mlstm_n, state_mlstm_m, state_rglru_h, c, c_ctx, ada_w, ada_b, norm1_g, w_in, attn_sink, mlstm_gate_b, mlstm_norm_g, rg_conv_w, rg_conv_b, rg_wr, rg_br, rg_wi, rg_bi, rg_lam, w_out, norm2_g, router_wg, router_bg, router_we, router_be, exp_w_gate, exp_w_up, exp_w_down, final_norm_g):
    bc, tc, d = x_prompt.shape
    bl, tl, _ = x_sample.shape
    depth = w_in.shape[0]
    past = cache_k.shape[2]
    nc, nl = bc * tc, bl * tl
    n = nc + nl
    tm = math.gcd(1024, math.gcd(tl, nc))

    x = jnp.concatenate([x_prompt.reshape(nc, d), x_sample.reshape(nl, d)], axis=0)
    cvec = _pad_rows(jnp.concatenate([c_ctx[None, :], c], axis=0), SUBLANES)
    mods = adaln_all(cvec, ada_w, ada_b).reshape(depth, SUBLANES, 6, 1, d)
    rope_c, rope_s = rope_tables(tl)
    gsplit = Z_MAIN - 2 * R_WIDTH

    new_k, new_v, new_c, new_n, new_m, new_h = [], [], [], [], [], []
    for l in range(depth):
        mod = mods[l]
        w_main = jnp.concatenate([w_in[l, :, :gsplit], w_in[l, :, gsplit + M_GATES:]], axis=1).astype(BF16)
        w_gcol = jnp.pad(w_in[l, :, gsplit:gsplit + M_GATES], ((0, 0), (0, LANES - M_GATES))).astype(BF16)
        z, zg = in_proj(x, norm1_g[l], mod, w_main, w_gcol, nc, tl, tm=tm)

        a_ctx = attention_context(z, attn_sink[l], bc, tc)
        a_lat = attention_latent(z, cache_k[:, l].reshape(bl, past, KV_WIDTH), cache_v[:, l].reshape(bl, past, KV_WIDTH),
                                 attn_sink[l], rope_c, rope_s, nc, bl, tl)
        a = jnp.concatenate([a_ctx, a_lat], axis=0)

        gate_b = jnp.pad(mlstm_gate_b[l].reshape(1, M_GATES), ((0, 0), (0, LANES - M_GATES)))
        mh, rh = [], []
        for dr in range(2):
            c0 = state_mlstm_C[:, l, dr]
            n0 = jnp.pad(state_mlstm_n[:, l, dr], ((0, 0), (0, SUBLANES - M_HEADS), (0, 0)))
            m0 = jnp.broadcast_to(jnp.pad(state_mlstm_m[:, l, dr], ((0, 0), (0, SUBLANES - M_HEADS)))[:, :, None],
                                  (bl, SUBLANES, LANES))
            hd, cf, nf, mf = mlstm_direction(z, zg, gate_b, c0, n0, m0, dr, bc, tc, bl, tl)
            mh.append(hd)
            new_c.append(cf[:bc])
            new_n.append(nf[:bc, :M_HEADS])
            new_m.append(mf[:bc, :M_HEADS, 0])
            h0 = state_rglru_h[:, l, dr].reshape(bl, 1, R_WIDTH)
            rd, hf = rglru_direction(z, rg_conv_w[l], rg_conv_b[l], rg_wr[l, dr], rg_br[l, dr], rg_wi[l, dr],
                                     rg_bi[l, dr], rg_lam[l, dr], h0, dr, bc, tc, bl, tl)
            rh.append(rd)
            new_h.append(hf[:bc, 0])
        new_k.append(z[:nc, _C_K:_C_K + KV_WIDTH].reshape(bc, tc, KV_HEADS, HEAD_DIM))
        new_v.append(z[:nc, _C_V:_C_V + KV_WIDTH].reshape(bc, tc, KV_HEADS, HEAD_DIM))

        x = out_proj(x, a, mh[0], mh[1], rh[0], rh[1], z, mlstm_norm_g[l], mod, w_out[l].astype(BF16), nc, tl, tm=tm)

        w_router = jnp.pad(jnp.concatenate([router_wg[l], router_we[l]], axis=1),
                           ((0, 0), (0, LANES - N_GROUPS - N_EXPERTS)))
        b_router = jnp.pad(jnp.concatenate([router_bg[l], router_be[l]])[None, :],
                           ((0, 0), (0, LANES - N_GROUPS - N_EXPERTS)))
        xn, logits = moe_pre(x, norm2_g[l], mod, w_router, b_router, nc, tl)
        gate, dest, slot_tok, blk_e, blk_src, n_used = moe_route(logits)
        xb = _pad_rows(xn, n + 1)[slot_tok]
        yb = moe_experts(xb, blk_e, blk_src, n_used, exp_w_gate[l], exp_w_up[l], exp_w_down[l])
        gates = jnp.pad(gate, ((0, 0), (0, LANES - TOP_K)))
        x = moe_combine(x, yb[dest[:, 0]], yb[dest[:, 1]], gates, mod, nc, tl)

    y = final_norm(x, final_norm_g)
    stack2 = lambda parts: jnp.stack([jnp.stack(parts[2 * l:2 * l + 2], axis=1) for l in range(depth)], axis=1)
    return (y[:nc].reshape(bc, tc, d), y[nc:].reshape(bl, tl, d),
            jnp.stack(new_k, axis=1), jnp.stack(new_v, axis=1),
            stack2(new_c), stack2(new_n), stack2(new_m), stack2(new_h))
```

```python
import functools
import math

import jax
import jax.numpy as jnp
from jax import lax
from jax.experimental import pallas as pl
from jax.experimental.pallas import tpu as pltpu

F32 = jnp.float32
BF16 = jnp.bfloat16
U32 = jnp.uint32
I32 = jnp.int32

D_MODEL = 2048
HEAD_DIM = 128
A_WIDTH = D_MODEL // 2
N_HEADS = A_WIDTH // HEAD_DIM
KV_HEADS = 2
GQA_GROUP = N_HEADS // KV_HEADS
KV_WIDTH = KV_HEADS * HEAD_DIM
WINDOW = 128
Q_BLOCK = 128
GRID_W = 64
ROPE_BASE = 10000.0
ROPE_AXIS = HEAD_DIM // 2
ATTN_SCALE = HEAD_DIM ** -0.5
NEG_INF = -1e30
M_WIDTH = D_MODEL // 4
M_HEADS = 4
M_DK = M_WIDTH // M_HEADS
M_GATES = 2 * 2 * M_HEADS
R_WIDTH = D_MODEL // 4
R_BLOCKS = 4
R_BW = R_WIDTH // R_BLOCKS
CONV_W = 4
RG_C = 8.0
N_GROUPS = 4
EXPERTS_PER_GROUP = 8
N_EXPERTS = N_GROUPS * EXPERTS_PER_GROUP
TOP_K = 2
D_EXPERT = D_MODEL // 2
NORM_EPS = 1e-6

LANES = 128
SUBLANES = 8
Z_MAIN = A_WIDTH + 2 * KV_WIDTH + 4 * M_WIDTH + 2 * R_WIDTH
_C_Q, _C_K, _C_V = 0, A_WIDTH, A_WIDTH + KV_WIDTH
_C_QM = A_WIDTH + 2 * KV_WIDTH
_C_KM, _C_VM, _C_OM = _C_QM + M_WIDTH, _C_QM + 2 * M_WIDTH, _C_QM + 3 * M_WIDTH
_C_XR, _C_YR = _C_QM + 4 * M_WIDTH, _C_QM + 4 * M_WIDTH + R_WIDTH

M_CHUNK = LANES
R_CHUNK = 256
MOE_TB = 512
VMEM_LIMIT = 56 * 1024 * 1024
HALF = D_MODEL // 2


def _cparams(sem):
    return pltpu.CompilerParams(dimension_semantics=sem, vmem_limit_bytes=VMEM_LIMIT)


def _group_of_tile(row0, nc, tl):
    return jnp.where(row0 < nc, 0, 1 + (jnp.maximum(row0 - nc, 0)) // tl)


def _adaln_kernel(c_ref, w_ref, b_ref, o_ref):
    c = c_ref[...]
    s = (c * jax.nn.sigmoid(c)).astype(BF16)
    o_ref[...] = jnp.dot(s, w_ref[...].astype(BF16), preferred_element_type=F32) + b_ref[...]


def adaln_all(cvec, ada_w, ada_b, tn=1024):
    depth, d, d6 = ada_w.shape
    return pl.pallas_call(
        _adaln_kernel,
        out_shape=jax.ShapeDtypeStruct((depth, SUBLANES, d6), F32),
        grid=(depth, d6 // tn),
        in_specs=[
            pl.BlockSpec((SUBLANES, d), lambda l, j: (0, 0)),
            pl.BlockSpec((None, d, tn), lambda l, j: (l, 0, j)),
            pl.BlockSpec((None, 1, tn), lambda l, j: (l, 0, j)),
        ],
        out_specs=pl.BlockSpec((None, SUBLANES, tn), lambda l, j: (l, 0, j)),
        compiler_params=_cparams(("arbitrary", "arbitrary")),
        name="adaln",
    )(cvec, ada_w, ada_b.reshape(depth, 1, d6))


def _in_kernel(x_ref, g_ref, sh_ref, sc_ref, w_ref, wg_ref, z_ref, zg_ref, xn_ref):
    @pl.when(pl.program_id(1) == 0)
    def _():
        x = x_ref[...]
        y = x * lax.rsqrt(jnp.mean(x * x, axis=-1, keepdims=True) + NORM_EPS) * g_ref[...]
        xn = (y * (1.0 + sc_ref[...]) + sh_ref[...]).astype(BF16)
        xn_ref[...] = xn
        zg_ref[...] = jnp.dot(xn, wg_ref[...], preferred_element_type=F32)

    z_ref[...] = jnp.dot(xn_ref[...], w_ref[...], preferred_element_type=F32)


def in_proj(x, norm_g, mod, w_main, w_gate, nc, tl, tm=1024, tn=768):
    n, d = x.shape
    zw = w_main.shape[1]
    grp = lambda i: _group_of_tile(i * tm, nc, tl)
    return pl.pallas_call(
        _in_kernel,
        out_shape=(jax.ShapeDtypeStruct((n, zw), F32), jax.ShapeDtypeStruct((n, LANES), F32)),
        grid=(n // tm, zw // tn),
        in_specs=[
            pl.BlockSpec((tm, d), lambda i, j: (i, 0)),
            pl.BlockSpec((1, d), lambda i, j: (0, 0)),
            pl.BlockSpec((None, None, 1, d), lambda i, j: (grp(i), 0, 0, 0)),
            pl.BlockSpec((None, None, 1, d), lambda i, j: (grp(i), 1, 0, 0)),
            pl.BlockSpec((d, tn), lambda i, j: (0, j)),
            pl.BlockSpec((d, LANES), lambda i, j: (0, 0)),
        ],
        out_specs=(pl.BlockSpec((tm, tn), lambda i, j: (i, j)), pl.BlockSpec((tm, LANES), lambda i, j: (i, 0))),
        scratch_shapes=[pltpu.VMEM((tm, d), BF16)],
        compiler_params=_cparams(("arbitrary", "arbitrary")),
        name="in_proj",
    )(x, norm_g.reshape(1, d), mod, mod, w_main, w_gate)


def _rope(x, c, s):
    lane = lax.broadcasted_iota(I32, x.shape, 1)
    half = ROPE_AXIS // 2
    partner = jnp.where((lane & (ROPE_AXIS - 1)) < half, pltpu.roll(x, HEAD_DIM - half, 1), pltpu.roll(x, half, 1))
    return x * c + partner * s


def _softmax_pv(parts, sink_col):
    m = sink_col
    for s, _ in parts:
        m = jnp.maximum(m, jnp.max(s, axis=-1, keepdims=True))
    den = jnp.exp(sink_col - m)
    acc = None
    for s, v in parts:
        p = jnp.exp(s - m)
        den = den + jnp.sum(p, axis=-1, keepdims=True)
        pv = jnp.dot(p.astype(BF16), v, preferred_element_type=F32)
        acc = pv if acc is None else acc + pv
    return acc / den


def _qk(q, k):
    return lax.dot_general(q, k, (((1,), (1,)), ((), ())), preferred_element_type=F32) * ATTN_SCALE


def _sink_column(sink_ref, n, rows):
    ridx = lax.broadcasted_iota(I32, (GQA_GROUP * rows, 1), 0)
    col = jnp.full((GQA_GROUP * rows, 1), sink_ref[n * GQA_GROUP], F32)
    for g in range(1, GQA_GROUP):
        col = jnp.where(ridx >= g * rows, sink_ref[n * GQA_GROUP + g], col)
    return col


def _attn_ctx_kernel(sink_ref, q_ref, k_ref, v_ref, o_ref):
    t = q_ref.shape[0]
    for n in range(KV_HEADS):
        k = k_ref[:, n * HEAD_DIM:(n + 1) * HEAD_DIM].astype(BF16)
        v = v_ref[:, n * HEAD_DIM:(n + 1) * HEAD_DIM].astype(BF16)
        q = jnp.concatenate(
            [q_ref[:, (n * GQA_GROUP + g) * HEAD_DIM:(n * GQA_GROUP + g + 1) * HEAD_DIM] for g in range(GQA_GROUP)],
            axis=0).astype(BF16)
        out = _softmax_pv([(_qk(q, k), v)], _sink_column(sink_ref, n, t))
        for g in range(GQA_GROUP):
            h = n * GQA_GROUP + g
            o_ref[:, h * HEAD_DIM:(h + 1) * HEAD_DIM] = out[g * t:(g + 1) * t].astype(o_ref.dtype)


def attention_context(z, sink, b, t):
    return pl.pallas_call(
        _attn_ctx_kernel,
        out_shape=jax.ShapeDtypeStruct((b * t, A_WIDTH), BF16),
        grid=(b,),
        in_specs=[
            pl.BlockSpec(memory_space=pltpu.SMEM),
            pl.BlockSpec((t, A_WIDTH), lambda i: (i, _C_Q // A_WIDTH)),
            pl.BlockSpec((t, KV_WIDTH), lambda i: (i, _C_K // KV_WIDTH)),
            pl.BlockSpec((t, KV_WIDTH), lambda i: (i, _C_V // KV_WIDTH)),
        ],
        out_specs=pl.BlockSpec((t, A_WIDTH), lambda i: (i, 0)),
        compiler_params=_cparams(("arbitrary",)),
        name="attn_ctx",
    )(sink, z, z, z)


def _attn_lat_kernel(sink_ref, q_ref, k_ref, v_ref, kc_ref, vc_ref, cq_ref, sq_ref, ck_ref, sk_ref, o_ref,
                     kr_ref, vp_ref, *, t):
    i = pl.program_id(1)
    rope_rows = 512

    @pl.when(i == 0)
    def _():
        zpad = jnp.zeros((WINDOW, KV_WIDTH), BF16)
        kr_ref[0:WINDOW, :] = zpad
        kr_ref[WINDOW + t:2 * WINDOW + t, :] = zpad
        vp_ref[0:WINDOW, :] = zpad
        vp_ref[WINDOW + t:2 * WINDOW + t, :] = zpad

        def body(c, carry):
            r0 = pl.multiple_of(c * rope_rows, rope_rows)
            cs, sn = ck_ref[pl.ds(r0, rope_rows), :], sk_ref[pl.ds(r0, rope_rows), :]
            for n in range(KV_HEADS):
                kk = k_ref[pl.ds(r0, rope_rows), n * HEAD_DIM:(n + 1) * HEAD_DIM]
                kr_ref[pl.ds(WINDOW + r0, rope_rows), n * HEAD_DIM:(n + 1) * HEAD_DIM] = _rope(kk, cs, sn).astype(BF16)
            vp_ref[pl.ds(WINDOW + r0, rope_rows), :] = v_ref[pl.ds(r0, rope_rows), :].astype(BF16)
            return carry

        lax.fori_loop(0, t // rope_rows, body, 0)

    span = Q_BLOCK + 2 * WINDOW
    rows = GQA_GROUP * Q_BLOCK
    r = lax.broadcasted_iota(I32, (rows, span), 0) & (Q_BLOCK - 1)
    c = lax.broadcasted_iota(I32, (rows, span), 1)
    kpos = (i - 1) * Q_BLOCK + c
    mask = (c >= r) & (c <= r + 2 * WINDOW) & (kpos >= 0) & (kpos < t)
    w0 = pl.multiple_of(i * Q_BLOCK, Q_BLOCK)
    cq, sq = cq_ref[...], sq_ref[...]
    for n in range(KV_HEADS):
        hs = slice(n * HEAD_DIM, (n + 1) * HEAD_DIM)
        q = jnp.concatenate(
            [_rope(q_ref[:, (n * GQA_GROUP + g) * HEAD_DIM:(n * GQA_GROUP + g + 1) * HEAD_DIM], cq, sq)
             for g in range(GQA_GROUP)], axis=0).astype(BF16)
        s_win = jnp.where(mask, _qk(q, kr_ref[pl.ds(w0, span), hs]), NEG_INF)
        s_ctx = _qk(q, kc_ref[:, hs].astype(BF16))
        out = _softmax_pv([(s_win, vp_ref[pl.ds(w0, span), hs]), (s_ctx, vc_ref[:, hs].astype(BF16))],
                          _sink_column(sink_ref, n, Q_BLOCK))
        for g in range(GQA_GROUP):
            h = n * GQA_GROUP + g
            o_ref[:, h * HEAD_DIM:(h + 1) * HEAD_DIM] = out[g * Q_BLOCK:(g + 1) * Q_BLOCK].astype(o_ref.dtype)


def attention_latent(z, cache_k, cache_v, layer, sink, rope_c, rope_s, row0, b, t):
    nqb = t // Q_BLOCK
    past = cache_k.shape[2]
    qb0 = row0 // Q_BLOCK
    tb0 = row0 // t
    kern = functools.partial(_attn_lat_kernel, t=t)
    return pl.pallas_call(
        kern,
        out_shape=jax.ShapeDtypeStruct((b * t, A_WIDTH), BF16),
        grid=(b, nqb),
        in_specs=[
            pl.BlockSpec(memory_space=pltpu.SMEM),
            pl.BlockSpec((Q_BLOCK, A_WIDTH), lambda bi, i: (qb0 + bi * nqb + i, _C_Q // A_WIDTH)),
            pl.BlockSpec((t, KV_WIDTH), lambda bi, i: (tb0 + bi, _C_K // KV_WIDTH)),
            pl.BlockSpec((t, KV_WIDTH), lambda bi, i: (tb0 + bi, _C_V // KV_WIDTH)),
            pl.BlockSpec((None, None, past, KV_WIDTH), lambda bi, i: (bi, layer, 0, 0)),
            pl.BlockSpec((None, None, past, KV_WIDTH), lambda bi, i: (bi, layer, 0, 0)),
            pl.BlockSpec((Q_BLOCK, HEAD_DIM), lambda bi, i: (i, 0)),
            pl.BlockSpec((Q_BLOCK, HEAD_DIM), lambda bi, i: (i, 0)),
            pl.BlockSpec((t, HEAD_DIM), lambda bi, i: (0, 0)),
            pl.BlockSpec((t, HEAD_DIM), lambda bi, i: (0, 0)),
        ],
        out_specs=pl.BlockSpec((Q_BLOCK, A_WIDTH), lambda bi, i: (bi * nqb + i, 0)),
        scratch_shapes=[pltpu.VMEM((t + 2 * WINDOW, KV_WIDTH), BF16), pltpu.VMEM((t + 2 * WINDOW, KV_WIDTH), BF16)],
        compiler_params=_cparams(("arbitrary", "arbitrary")),
        name="attn_lat",
    )(sink, z, z, z, cache_k, cache_v, rope_c, rope_s, rope_c, rope_s)


def rope_tables(t):
    pos = jnp.arange(t)
    inv = ROPE_BASE ** (-jnp.arange(0, ROPE_AXIS, 2, dtype=F32) / ROPE_AXIS)

    def cs(p):
        ang = p.astype(F32)[:, None] * inv[None, :]
        return jnp.cos(ang), jnp.sin(ang)

    cr, sr = cs(pos // GRID_W)
    cc, sc = cs(pos % GRID_W)
    return (jnp.concatenate([cr, cr, cc, cc], axis=-1), jnp.concatenate([-sr, sr, -sc, sc], axis=-1))


class _Seqs:
    def __init__(self, n_ctx, t_ctx, n_lat, t_lat, chunk, reverse):
        self.n_ctx, self.t_ctx, self.n_lat, self.t_lat = n_ctx, t_ctx, n_lat, t_lat
        self.chunk, self.reverse = chunk, reverse
        self.cpc, self.cpl = t_ctx // chunk, t_lat // chunk
        self.ctx_chunks = n_ctx * self.cpc
        self.n_chunks = self.ctx_chunks + n_lat * self.cpl

    def chunk_of_step(self, s):
        return (self.n_chunks - 1 - s) if self.reverse else s

    def info(self, g):
        is_ctx = g < self.ctx_chunks
        gl = jnp.maximum(g - self.ctx_chunks, 0)
        gc = jnp.minimum(g, self.ctx_chunks - 1)
        pos = jnp.where(is_ctx, gc % self.cpc, gl % self.cpl)
        per = jnp.where(is_ctx, self.cpc, self.cpl)
        lat = gl // self.cpl
        seq = jnp.where(is_ctx, gc // self.cpc, self.n_ctx + lat)
        head, tail = pos == 0, pos == per - 1
        return is_ctx, seq, lat, (tail if self.reverse else head), (head if self.reverse else tail)


def _lane_scan(x, op, reverse):
    n = x.shape[1]
    lane = lax.broadcasted_iota(I32, x.shape, 1)
    s = 1
    while s < n:
        if reverse:
            x = jnp.where(lane < n - s, op(x, pltpu.roll(x, n - s, 1)), x)
        else:
            x = jnp.where(lane >= s, op(x, pltpu.roll(x, s, 1)), x)
        s *= 2
    return x


_GROWS = 2 * SUBLANES


def _mlstm_gate_kernel(g_ref, gb_ref, rows_ref, cols_ref):
    L = M_CHUNK
    nck = g_ref.shape[0] // L
    g = g_ref[...] + gb_ref[...]
    s = jnp.concatenate([g[c * L:(c + 1) * L].T[0:_GROWS] for c in range(nck)], axis=0)
    row = lax.broadcasted_iota(I32, s.shape, 0)
    is_rev = (row & SUBLANES) != 0
    lf = jax.nn.log_sigmoid(s)
    f = jnp.where(is_rev, _lane_scan(lf, jnp.add, True), _lane_scan(lf, jnp.add, False))
    f = pltpu.roll(f, s.shape[0] - M_HEADS, 0)
    r = s - f
    cm = jnp.where(is_rev, _lane_scan(r, jnp.maximum, True), _lane_scan(r, jnp.maximum, False))
    rows_ref[...] = r
    pad = jnp.zeros((LANES - 3 * _GROWS, L), F32)
    for c in range(nck):
        sl = slice(c * _GROWS, (c + 1) * _GROWS)
        cols_ref[c * L:(c + 1) * L, :] = jnp.concatenate([r[sl], cm[sl], f[sl], pad], axis=0).T


def mlstm_gates(zg, gate_b, tile=1024):
    n = zg.shape[0]
    nck = tile // M_CHUNK
    return pl.pallas_call(
        _mlstm_gate_kernel,
        out_shape=(jax.ShapeDtypeStruct((n // M_CHUNK * _GROWS, M_CHUNK), F32), jax.ShapeDtypeStruct((n, LANES), F32)),
        grid=(n // tile,),
        in_specs=[pl.BlockSpec((tile, LANES), lambda i: (i, 0)), pl.BlockSpec((1, LANES), lambda i: (0, 0))],
        out_specs=(pl.BlockSpec((nck * _GROWS, M_CHUNK), lambda i: (i, 0)), pl.BlockSpec((tile, LANES), lambda i: (i, 0))),
        compiler_params=_cparams(("arbitrary",)),
        name="mlstm_gates",
    )(zg, gate_b)


def _mlstm_kernel(m0_ref, qf_ref, kf_ref, vf_ref, rf_ref, cf_ref, qb_ref, kb_ref, vb_ref, rb_ref, cb_ref,
                  c0f_ref, n0f_ref, c0b_ref, n0b_ref,
                  hf_ref, hb_ref, ocf_ref, onf_ref, omf_ref, ocb_ref, onb_ref, omb_ref,
                  c_s, n_s, m_s, *, seqs, layer, depth):
    L = M_CHUNK
    step = pl.program_id(0)
    ti = lax.broadcasted_iota(I32, (L, L), 0)
    si = lax.broadcasted_iota(I32, (L, L), 1)
    scale = M_DK ** -0.5
    streams = (
        (0, seqs[0], qf_ref, kf_ref, vf_ref, rf_ref, cf_ref, c0f_ref, n0f_ref, hf_ref, ocf_ref, onf_ref, omf_ref),
        (1, seqs[1], qb_ref, kb_ref, vb_ref, rb_ref, cb_ref, c0b_ref, n0b_ref, hb_ref, ocb_ref, onb_ref, omb_ref),
    )
    for d, sq, q_ref, k_ref, v_ref, rows_ref, cols_ref, c0_ref, n0_ref, h_ref, oc_ref, on_ref, om_ref in streams:
        rev = d == 1
        is_ctx, _, lat, first, last = sq.info(sq.chunk_of_step(step))

        @pl.when(first)
        def _(d=d, is_ctx=is_ctx, lat=lat, c0_ref=c0_ref, n0_ref=n0_ref):
            keep = jnp.where(is_ctx, 0.0, 1.0).astype(F32)
            c_s[d] = c0_ref[...] * keep
            n_s[d] = jnp.concatenate([n0_ref[...] * keep, jnp.zeros((SUBLANES - M_HEADS, M_DK), F32)], axis=0)
            m_s[d] = jnp.zeros((SUBLANES, LANES), F32)
            for h in range(M_HEADS):
                m0 = m0_ref[((lat * depth + layer) * 2 + d) * M_HEADS + h]
                m_s[d, h:h + 1, :] = jnp.full((1, LANES), m0, F32) * keep

        end = 0 if rev else L - 1
        causal = (si >= ti) if rev else (si <= ti)
        for h in range(M_HEADS):
            hs = slice(h * M_DK, (h + 1) * M_DK)
            gi = d * SUBLANES + h
            q = q_ref[:, hs].astype(BF16)
            kf = k_ref[:, hs] * scale
            v = v_ref[:, hs].astype(BF16)
            r_row = rows_ref[gi:gi + 1, :]
            r_col = cols_ref[:, gi:gi + 1]
            cm_col = cols_ref[:, _GROWS + gi:_GROWS + gi + 1]
            f_col = cols_ref[:, 2 * _GROWS + gi:2 * _GROWS + gi + 1]
            m_old = m_s[d, h:h + 1, 0:1]
            m_col = jnp.maximum(m_old, cm_col)
            dmat = jnp.where(causal, jnp.exp(jnp.where(causal, r_row - m_col, 0.0)), 0.0)
            s = lax.dot_general(q, kf.astype(BF16), (((1,), (1,)), ((), ())), preferred_element_type=F32) * dmat
            wp = jnp.exp(m_old - m_col)
            c_old = c_s[d, h]
            n_old = n_s[d, h:h + 1, :]
            num = jnp.dot(s.astype(BF16), v, preferred_element_type=F32) + wp * jnp.dot(
                q, c_old.astype(BF16), preferred_element_type=F32)
            qn = jnp.sum(q.astype(F32) * n_old.astype(BF16).astype(F32), axis=-1, keepdims=True)
            den = jnp.sum(s, axis=-1, keepdims=True) + wp * qn
            h_ref[:, hs] = num / jnp.maximum(jnp.abs(den), jnp.exp(-(f_col + m_col)))
            m_end = m_col[end:end + 1, :]
            f_end = f_col[end:end + 1, :]
            kw = kf * jnp.exp(r_col - m_end)
            dec = jnp.exp(m_old - m_end)
            c_s[d, h] = dec * c_old + lax.dot_general(kw.astype(BF16), v, (((0,), (0,)), ((), ())),
                                                      preferred_element_type=F32)
            n_s[d, h:h + 1, :] = dec * n_old + jnp.sum(kw, axis=0, keepdims=True)
            m_s[d, h:h + 1, :] = jnp.broadcast_to(f_end + m_end, (1, LANES))

        @pl.when(last)
        def _(d=d, oc_ref=oc_ref, on_ref=on_ref, om_ref=om_ref):
            oc_ref[...] = c_s[d]
            on_ref[...] = n_s[d]
            om_ref[...] = m_s[d]


def mlstm_scan(z, rows, cols, st_c, st_n, st_m, layer, n_ctx, t_ctx, n_lat, t_lat):
    L = M_CHUNK
    seqs = (_Seqs(n_ctx, t_ctx, n_lat, t_lat, L, False), _Seqs(n_ctx, t_ctx, n_lat, t_lat, L, True))
    n_seq = n_ctx + n_lat
    n_chunks = seqs[0].n_chunks
    n_rows = n_chunks * L
    kern = functools.partial(_mlstm_kernel, seqs=seqs, layer=layer, depth=st_m.shape[1])
    in_specs = [pl.BlockSpec(memory_space=pltpu.SMEM)]
    for sq in seqs:
        cm = lambda s, sq=sq: sq.chunk_of_step(s)
        in_specs += [
            pl.BlockSpec((L, M_WIDTH), lambda s, cm=cm: (cm(s), _C_QM // M_WIDTH)),
            pl.BlockSpec((L, M_WIDTH), lambda s, cm=cm: (cm(s), _C_KM // M_WIDTH)),
            pl.BlockSpec((L, M_WIDTH), lambda s, cm=cm: (cm(s), _C_VM // M_WIDTH)),
            pl.BlockSpec((_GROWS, L), lambda s, cm=cm: (cm(s), 0)),
            pl.BlockSpec((L, LANES), lambda s, cm=cm: (cm(s), 0)),
        ]
    for d, sq in enumerate(seqs):
        lat_of = lambda s, sq=sq: sq.info(sq.chunk_of_step(s))[2]
        in_specs += [
            pl.BlockSpec((None, None, None, M_HEADS, M_DK, M_DK), lambda s, f=lat_of, d=d: (f(s), layer, d, 0, 0, 0)),
            pl.BlockSpec((None, None, None, M_HEADS, M_DK), lambda s, f=lat_of, d=d: (f(s), layer, d, 0, 0)),
        ]
    out_shape = [jax.ShapeDtypeStruct((n_rows, M_WIDTH), F32)] * 2
    out_specs = [pl.BlockSpec((L, M_WIDTH), lambda s, sq=sq: (sq.chunk_of_step(s), 0)) for sq in seqs]
    for sq in seqs:
        seq_of = lambda s, sq=sq: sq.info(sq.chunk_of_step(s))[1]
        out_shape += [jax.ShapeDtypeStruct((n_seq, M_HEADS, M_DK, M_DK), F32),
                      jax.ShapeDtypeStruct((n_seq, SUBLANES, M_DK), F32),
                      jax.ShapeDtypeStruct((n_seq, SUBLANES, LANES), F32)]
        out_specs += [pl.BlockSpec((None, M_HEADS, M_DK, M_DK), lambda s, f=seq_of: (f(s), 0, 0, 0)),
                      pl.BlockSpec((None, SUBLANES, M_DK), lambda s, f=seq_of: (f(s), 0, 0)),
                      pl.BlockSpec((None, SUBLANES, LANES), lambda s, f=seq_of: (f(s), 0, 0))]
    args = [st_m.reshape(-1), z, z, z, rows, cols, z, z, z, rows, cols, st_c, st_n, st_c, st_n]
    return pl.pallas_call(
        kern,
        out_shape=tuple(out_shape),
        grid=(n_chunks,),
        in_specs=in_specs,
        out_specs=tuple(out_specs),
        scratch_shapes=[pltpu.VMEM((2, M_HEADS, M_DK, M_DK), F32), pltpu.VMEM((2, SUBLANES, M_DK), F32),
                        pltpu.VMEM((2, SUBLANES, LANES), F32)],
        compiler_params=_cparams(("arbitrary",)),
        name="mlstm_scan",
    )(*args)


def _rglru_kernel(x_ref, xp_ref, xn_ref, cw_ref, cb_ref, wr_ref, br_ref, wi_ref, bi_ref, lam_ref, h0_ref,
                  h_ref, hf_ref, xpad, carry, *, seqs):
    L = seqs.chunk
    rev = seqs.reverse
    g = seqs.chunk_of_step(pl.program_id(0))
    is_ctx, _, _, first, last = seqs.info(g)
    head = last if rev else first
    tail = first if rev else last
    halo = SUBLANES
    xpad[0:halo, :] = xp_ref[...] * jnp.where(head, 0.0, 1.0).astype(F32)
    xpad[halo:halo + L, :] = x_ref[...]
    xpad[halo + L:2 * halo + L, :] = xn_ref[...] * jnp.where(tail, 0.0, 1.0).astype(F32)
    x = cb_ref[...]
    for j in range(CONV_W):
        x = x + cw_ref[j:j + 1, :] * xpad[halo - 2 + j:halo - 2 + j + L, :]

    rs, is_ = [], []
    for n in range(R_BLOCKS):
        xb = x[:, n * R_BW:(n + 1) * R_BW].astype(BF16)
        rs.append(jnp.dot(xb, wr_ref[n].astype(BF16), preferred_element_type=F32))
        is_.append(jnp.dot(xb, wi_ref[n].astype(BF16), preferred_element_type=F32))
    rg = jax.nn.sigmoid(jnp.concatenate(rs, axis=-1) + br_ref[...])
    ig = jax.nn.sigmoid(jnp.concatenate(is_, axis=-1) + bi_ref[...])
    log_a = -RG_C * rg * jax.nn.softplus(-lam_ref[...])
    a = jnp.exp(log_a)
    u = jnp.sqrt(1.0 - jnp.exp(2.0 * log_a)) * (ig * x)

    row = lax.broadcasted_iota(I32, (L, R_WIDTH), 0)
    s = 1
    while s < L:
        if rev:
            ok = row < L - s
            a_sh, u_sh = pltpu.roll(a, L - s, 0), pltpu.roll(u, L - s, 0)
        else:
            ok = row >= s
            a_sh, u_sh = pltpu.roll(a, s, 0), pltpu.roll(u, s, 0)
        u = jnp.where(ok, a * u_sh + u, u)
        a = jnp.where(ok, a * a_sh, a)
        s *= 2

    @pl.when(first)
    def _():
        carry[...] = h0_ref[...] * jnp.where(is_ctx, 0.0, 1.0).astype(F32)

    h = a * carry[...] + u
    h_ref[...] = h
    end = 0 if rev else L - 1
    carry[...] = h[end:end + 1, :]

    @pl.when(last)
    def _():
        hf_ref[...] = h[end:end + 1, :]


def rglru_direction(z, conv_w, conv_b, wr, br, wi, bi, lam, h0, layer, direction, n_ctx, t_ctx, n_lat, t_lat):
    seqs = _Seqs(n_ctx, t_ctx, n_lat, t_lat, R_CHUNK, direction == 1)
    n_seq = n_ctx + n_lat
    L = R_CHUNK
    n_rows = seqs.n_chunks * L
    hb = L // SUBLANES
    nb8 = n_rows // SUBLANES
    cm = lambda s: seqs.chunk_of_step(s)
    lat_of = lambda s: seqs.info(cm(s))[2]
    seq_of = lambda s: seqs.info(cm(s))[1]
    xcol = _C_XR // R_WIDTH
    vec_ld = lambda: pl.BlockSpec((None, None, 1, R_WIDTH), lambda s: (layer, direction, 0, 0))
    mat_ld = lambda: pl.BlockSpec((None, None, R_BLOCKS, R_BW, R_BW), lambda s: (layer, direction, 0, 0, 0))
    kern = functools.partial(_rglru_kernel, seqs=seqs)
    depth = conv_w.shape[0]
    r4 = lambda a: a.reshape(depth, 2, 1, R_WIDTH)
    return pl.pallas_call(
        kern,
        out_shape=(jax.ShapeDtypeStruct((n_rows, R_WIDTH), F32), jax.ShapeDtypeStruct((n_seq, 1, R_WIDTH), F32)),
        grid=(seqs.n_chunks,),
        in_specs=[
            pl.BlockSpec((L, R_WIDTH), lambda s: (cm(s), xcol)),
            pl.BlockSpec((SUBLANES, R_WIDTH), lambda s: (jnp.maximum(cm(s) * hb - 1, 0), xcol)),
            pl.BlockSpec((SUBLANES, R_WIDTH), lambda s: (jnp.minimum((cm(s) + 1) * hb, nb8 - 1), xcol)),
            pl.BlockSpec((None, CONV_W, R_WIDTH), lambda s: (layer, 0, 0)),
            pl.BlockSpec((None, 1, R_WIDTH), lambda s: (layer, 0, 0)),
            mat_ld(), vec_ld(), mat_ld(), vec_ld(), vec_ld(),
            pl.BlockSpec((None, None, None, 1, R_WIDTH), lambda s: (lat_of(s), layer, direction, 0, 0)),
        ],
        out_specs=(pl.BlockSpec((L, R_WIDTH), lambda s: (cm(s), 0)),
                   pl.BlockSpec((None, 1, R_WIDTH), lambda s: (seq_of(s), 0, 0))),
        scratch_shapes=[pltpu.VMEM((L + 2 * SUBLANES, R_WIDTH), F32), pltpu.VMEM((1, R_WIDTH), F32)],
        compiler_params=_cparams(("arbitrary",)),
        name=f"rglru_d{direction}",
    )(z, z, z, conv_w, conv_b.reshape(depth, 1, R_WIDTH), wr, r4(br), wi, r4(bi), r4(lam),
      h0.reshape(h0.shape[0], depth, 2, 1, R_WIDTH))


def _gelu_tanh(x):
    return 0.5 * x * (1.0 + jnp.tanh(math.sqrt(2.0 / math.pi) * (x + 0.044715 * (x * x * x))))


def _out_kernel(x_ref, a_ref, mf_ref, mb_ref, om_ref, rf_ref, rb_ref, yr_ref, mg_ref, gate_ref, w_ref,
                o_ref, mix_ref):
    @pl.when(pl.program_id(1) == 0)
    def _():
        mix_ref[:, 0:A_WIDTH] = a_ref[...]
        hs = mf_ref[...] + mb_ref[...]
        for h in range(M_HEADS):
            cs = slice(h * M_DK, (h + 1) * M_DK)
            hh = hs[:, cs]
            hn = hh * lax.rsqrt(jnp.mean(hh * hh, axis=-1, keepdims=True) + NORM_EPS) * mg_ref[:, cs]
            mix_ref[:, A_WIDTH + h * M_DK:A_WIDTH + (h + 1) * M_DK] = (
                hn * jax.nn.sigmoid(om_ref[:, cs])).astype(BF16)
        mix_ref[:, A_WIDTH + M_WIDTH:] = ((rf_ref[...] + rb_ref[...]) * _gelu_tanh(yr_ref[...])).astype(BF16)

    o_ref[...] = x_ref[...] + gate_ref[...] * jnp.dot(mix_ref[...], w_ref[...], preferred_element_type=F32)


def out_proj(x, a, mf, mb, rf, rb, z, mnorm_g, mod, w_out, nc, tl, tm=1024, tn=512):
    n, d = x.shape
    grp = lambda i: _group_of_tile(i * tm, nc, tl)
    rowblk = lambda w, col: pl.BlockSpec((tm, w), lambda i, j: (i, col))
    return pl.pallas_call(
        _out_kernel,
        out_shape=jax.ShapeDtypeStruct((n, d), F32),
        grid=(n // tm, d // tn),
        in_specs=[
            pl.BlockSpec((tm, tn), lambda i, j: (i, j)),
            rowblk(A_WIDTH, 0),
            rowblk(M_WIDTH, 0), rowblk(M_WIDTH, 0), rowblk(M_WIDTH, _C_OM // M_WIDTH),
            rowblk(R_WIDTH, 0), rowblk(R_WIDTH, 0), rowblk(R_WIDTH, _C_YR // R_WIDTH),
            pl.BlockSpec((1, M_WIDTH), lambda i, j: (0, 0)),
            pl.BlockSpec((None, None, 1, tn), lambda i, j: (grp(i), 2, 0, j)),
            pl.BlockSpec((d, tn), lambda i, j: (0, j)),
        ],
        out_specs=pl.BlockSpec((tm, tn), lambda i, j: (i, j)),
        scratch_shapes=[pltpu.VMEM((tm, d), BF16)],
        compiler_params=_cparams(("arbitrary", "arbitrary")),
        name="out_proj",
    )(x, a, mf, mb, z, rf, rb, z, mnorm_g.reshape(1, -1), mod, w_out)


def _bits(x):
    return lax.bitcast_convert_type(x, U32)


def _moe_pre_kernel(x_ref, g_ref, sh_ref, sc_ref, wr_ref, br_ref, xp_ref, rg_ref, re_ref):
    x = x_ref[...]
    y = x * lax.rsqrt(jnp.mean(x * x, axis=-1, keepdims=True) + NORM_EPS) * g_ref[...]
    xn = y * (1.0 + sc_ref[...]) + sh_ref[...]
    lo = _bits(xn[:, :HALF].astype(BF16).astype(F32))
    hi = _bits(xn[:, HALF:].astype(BF16).astype(F32))
    xp_ref[...] = (hi & jnp.uint32(0xFFFF0000)) | (lo >> 16)

    lg = jnp.dot(xn, wr_ref[...], preferred_element_type=F32, precision=lax.Precision.HIGHEST) + br_ref[...]
    lane = lax.broadcasted_iota(I32, lg.shape, 1).astype(F32)
    ninf = jnp.float32(-jnp.inf)

    def top(mask):
        val = jnp.max(jnp.where(mask, lg, ninf), axis=-1, keepdims=True)
        idx = jnp.min(jnp.where(mask & (lg == val), lane, float(LANES)), axis=-1, keepdims=True)
        return val, idx

    is_g = lane < N_GROUPS
    g_val, g_idx = top(is_g)
    g_w = 1.0 / jnp.sum(jnp.where(is_g, jnp.exp(lg - g_val), 0.0), axis=-1, keepdims=True)
    e_lo = N_GROUPS + g_idx * EXPERTS_PER_GROUP
    in_grp = (lane >= e_lo) & (lane < e_lo + EXPERTS_PER_GROUP)
    v1, i1 = top(in_grp)
    v2, i2 = top(in_grp & (lane != i1))
    t = jnp.exp(v2 - v1)
    w1 = g_w / (1.0 + t)
    rg_ref[...] = jnp.where(lane == 0, w1, jnp.where(lane == 1, w1 * t, 0.0))
    re_ref[...] = jnp.where(lane == 0, i1 - N_GROUPS, jnp.where(lane == 1, i2 - N_GROUPS, 0.0)).astype(I32)


def moe_pre(x, norm_g, mod, w_router, b_router, nc, tl, tm=512):
    n, d = x.shape
    grp = lambda i: _group_of_tile(i * tm, nc, tl)
    lane_out = lambda: pl.BlockSpec((tm, LANES), lambda i: (i, 0))
    return pl.pallas_call(
        _moe_pre_kernel,
        out_shape=(jax.ShapeDtypeStruct((n, HALF), U32), jax.ShapeDtypeStruct((n, LANES), F32),
                   jax.ShapeDtypeStruct((n, LANES), I32)),
        grid=(n // tm,),
        in_specs=[
            pl.BlockSpec((tm, d), lambda i: (i, 0)),
            pl.BlockSpec((1, d), lambda i: (0, 0)),
            pl.BlockSpec((None, None, 1, d), lambda i: (grp(i), 3, 0, 0)),
            pl.BlockSpec((None, None, 1, d), lambda i: (grp(i), 4, 0, 0)),
            pl.BlockSpec((d, LANES), lambda i: (0, 0)),
            pl.BlockSpec((1, LANES), lambda i: (0, 0)),
        ],
        out_specs=(pl.BlockSpec((tm, HALF), lambda i: (i, 0)), lane_out(), lane_out()),
        compiler_params=_cparams(("arbitrary",)),
        name="moe_pre",
    )(x, norm_g.reshape(1, d), mod, mod, w_router, b_router)


def _expert_changed(be_ref, j):
    return (j == 0) | (be_ref[j] != be_ref[jnp.maximum(j - 1, 0)])


def _moe_up_kernel(be_ref, bs_ref, nu_ref, x_ref, wg_ref, wu_ref, h_ref, wg_s, wu_s):
    j = pl.program_id(1)

    @pl.when((j < nu_ref[0]) & _expert_changed(be_ref, j))
    def _():
        wg_s[...] = wg_ref[...].astype(BF16)
        wu_s[...] = wu_ref[...].astype(BF16)

    @pl.when(j < nu_ref[0])
    def _():
        w = x_ref[...]
        lo = lax.bitcast_convert_type(w << 16, F32).astype(BF16)
        hi = lax.bitcast_convert_type(w & jnp.uint32(0xFFFF0000), F32).astype(BF16)

        def mm(w_s):
            return (jnp.dot(lo, w_s[0:HALF, :], preferred_element_type=F32)
                    + jnp.dot(hi, w_s[HALF:, :], preferred_element_type=F32))

        gt, up = mm(wg_s), mm(wu_s)
        h_ref[...] = (gt * jax.nn.sigmoid(gt) * up).astype(BF16)


def _moe_down_kernel(be_ref, bs_ref, nu_ref, h_ref, wd_ref, y_ref, wd_s):
    j = pl.program_id(1)

    @pl.when((j < nu_ref[0]) & _expert_changed(be_ref, j))
    def _():
        wd_s[...] = wd_ref[...].astype(BF16)

    @pl.when(j < nu_ref[0])
    def _():
        y_ref[...] = jnp.dot(h_ref[...], wd_s[...], preferred_element_type=F32)


def moe_experts(xb, blk_e, blk_src, n_used, w_gate, w_up, w_down, layer, tb=MOE_TB, tc=512, tc2=1024):
    rows = xb.shape[0]
    n_blocks = rows // tb
    d, de = w_gate.shape[-2:]
    h = pl.pallas_call(
        _moe_up_kernel,
        out_shape=jax.ShapeDtypeStruct((rows, de), BF16),
        grid_spec=pltpu.PrefetchScalarGridSpec(
            num_scalar_prefetch=3,
            grid=(de // tc, n_blocks),
            in_specs=[
                pl.BlockSpec((tb, HALF), lambda c, j, be, bs, nu: (bs[j], 0)),
                pl.BlockSpec((None, None, d, tc), lambda c, j, be, bs, nu: (layer, be[j], 0, c)),
                pl.BlockSpec((None, None, d, tc), lambda c, j, be, bs, nu: (layer, be[j], 0, c)),
            ],
            out_specs=pl.BlockSpec((tb, tc), lambda c, j, be, bs, nu: (bs[j], c)),
            scratch_shapes=[pltpu.VMEM((d, tc), BF16), pltpu.VMEM((d, tc), BF16)],
        ),
        compiler_params=_cparams(("arbitrary", "arbitrary")),
        name="moe_up",
    )(blk_e, blk_src, n_used, xb, w_gate, w_up)
    return pl.pallas_call(
        _moe_down_kernel,
        out_shape=jax.ShapeDtypeStruct((rows, d), F32),
        grid_spec=pltpu.PrefetchScalarGridSpec(
            num_scalar_prefetch=3,
            grid=(d // tc2, n_blocks),
            in_specs=[
                pl.BlockSpec((tb, de), lambda c, j, be, bs, nu: (bs[j], 0)),
                pl.BlockSpec((None, None, de, tc2), lambda c, j, be, bs, nu: (layer, be[j], 0, c)),
            ],
            out_specs=pl.BlockSpec((tb, tc2), lambda c, j, be, bs, nu: (bs[j], c)),
            scratch_shapes=[pltpu.VMEM((de, tc2), BF16)],
        ),
        compiler_params=_cparams(("arbitrary", "arbitrary")),
        name="moe_down",
    )(blk_e, blk_src, n_used, h, w_down)


def _moe_combine_kernel(x_ref, y0_ref, y1_ref, gt_ref, gate_ref, o_ref):
    y = gt_ref[:, 0:1] * y0_ref[...] + gt_ref[:, 1:2] * y1_ref[...]
    o_ref[...] = x_ref[...] + gate_ref[...] * y


def moe_combine(x, y0, y1, gates, mod, nc, tl, tm=512):
    n, d = x.shape
    grp = lambda i: _group_of_tile(i * tm, nc, tl)
    blk = lambda: pl.BlockSpec((tm, d), lambda i: (i, 0))
    return pl.pallas_call(
        _moe_combine_kernel,
        out_shape=jax.ShapeDtypeStruct((n, d), F32),
        grid=(n // tm,),
        in_specs=[blk(), blk(), blk(), pl.BlockSpec((tm, LANES), lambda i: (i, 0)),
                  pl.BlockSpec((None, None, 1, d), lambda i: (grp(i), 5, 0, 0))],
        out_specs=blk(),
        compiler_params=_cparams(("arbitrary",)),
        name="moe_combine",
    )(x, y0, y1, gates, mod)


def moe_layout(eid, tb=MOE_TB):
    n = eid.shape[0]
    s_len = n * TOP_K
    flat_e = eid.reshape(s_len)
    onehot = (flat_e[:, None] == jnp.arange(N_EXPERTS, dtype=I32)[None, :]).astype(I32)
    rank = jnp.sum((jnp.cumsum(onehot, axis=0) - onehot) * onehot, axis=1)
    counts = jnp.sum(onehot, axis=0)
    nblk_e = (counts + tb - 1) // tb
    blk_end = jnp.cumsum(nblk_e)
    blk_start = blk_end - nblk_e
    dest = jnp.sum(onehot * blk_start[None, :], axis=1) * tb + rank
    n_blocks = s_len // tb + N_EXPERTS
    n_used = blk_end[-1]
    slot_tok = jnp.zeros((n_blocks * tb,), I32).at[dest].set(jnp.arange(s_len, dtype=I32) // TOP_K)
    jc = jnp.minimum(jnp.arange(n_blocks, dtype=I32), n_used - 1)
    blk_e = jnp.minimum(jnp.sum((blk_end[None, :] <= jc[:, None]).astype(I32), axis=1), N_EXPERTS - 1)
    return dest.reshape(n, TOP_K), slot_tok, blk_e.astype(I32), jc, n_used.reshape(1).astype(I32)


def _final_norm_kernel(x_ref, g_ref, o_ref):
    x = x_ref[...]
    o_ref[...] = x * lax.rsqrt(jnp.mean(x * x, axis=-1, keepdims=True) + NORM_EPS) * g_ref[...]


def final_norm(x, g, tm=512):
    n, d = x.shape
    return pl.pallas_call(
        _final_norm_kernel,
        out_shape=jax.ShapeDtypeStruct((n, d), F32),
        grid=(n // tm,),
        in_specs=[pl.BlockSpec((tm, d), lambda i: (i, 0)), pl.BlockSpec((1, d), lambda i: (0, 0))],
        out_specs=pl.BlockSpec((tm, d), lambda i: (i, 0)),
        compiler_params=_cparams(("arbitrary",)),
        name="final_norm",
    )(x, g.reshape(1, d))


def _pad_lanes(a):
    return jnp.pad(a, ((0, 0), (0, LANES - a.shape[1])))


def kernel(x_prompt, x_sample, cache_k, cache_v, state_mlstm_C, state_mlstm_n, state_mlstm_m, state_rglru_h, c, c_ctx, ada_w, ada_b, norm1_g, w_in, attn_sink, mlstm_gate_b, mlstm_norm_g, rg_conv_w, rg_conv_b, rg_wr, rg_br, rg_wi, rg_bi, rg_lam, w_out, norm2_g, router_wg, router_bg, router_we, router_be, exp_w_gate, exp_w_up, exp_w_down, final_norm_g):
    bc, tc, d = x_prompt.shape
    bl, tl, _ = x_sample.shape
    depth = w_in.shape[0]
    past = cache_k.shape[2]
    nc, nl = bc * tc, bl * tl
    tm = math.gcd(1024, math.gcd(tl, nc))

    x = jnp.concatenate([x_prompt.reshape(nc, d), x_sample.reshape(nl, d)], axis=0)
    cvec = jnp.concatenate([c_ctx[None, :], c, jnp.zeros((SUBLANES - 1 - bl, d), F32)], axis=0)
    mods = adaln_all(cvec, ada_w, ada_b).reshape(depth, SUBLANES, 6, 1, d)
    rope_c, rope_s = rope_tables(tl)
    gsplit = Z_MAIN - 2 * R_WIDTH
    w_main = jnp.concatenate([w_in[:, :, :gsplit], w_in[:, :, gsplit + M_GATES:]], axis=2).astype(BF16)
    w_gcol = jnp.pad(w_in[:, :, gsplit:gsplit + M_GATES], ((0, 0), (0, 0), (0, LANES - M_GATES))).astype(BF16)
    w_out_b = w_out.astype(BF16)
    ck = cache_k.reshape(bl, depth, past, KV_WIDTH)
    cv = cache_v.reshape(bl, depth, past, KV_WIDTH)

    new_k, new_v, new_c, new_n, new_m, new_h = [], [], [], [], [], []
    for l in range(depth):
        mod = mods[l]
        z, zg = in_proj(x, norm1_g[l], mod, w_main[l], w_gcol[l], nc, tl, tm=tm)

        a_ctx = attention_context(z, attn_sink[l], bc, tc)
        a_lat = attention_latent(z, ck, cv, l, attn_sink[l], rope_c, rope_s, nc, bl, tl)
        a = jnp.concatenate([a_ctx, a_lat], axis=0)

        rows, cols = mlstm_gates(zg, _pad_lanes(mlstm_gate_b[l].reshape(1, M_GATES)), tile=tm)
        res = mlstm_scan(z, rows, cols, state_mlstm_C, state_mlstm_n, state_mlstm_m, l, bc, tc, bl, tl)
        mh = res[0:2]
        for dr in range(2):
            cf, nf, mf = res[2 + 3 * dr:5 + 3 * dr]
            new_c.append(cf[:bc])
            new_n.append(nf[:bc, :M_HEADS])
            new_m.append(mf[:bc, :M_HEADS, 0])
        rh = []
        for dr in range(2):
            rd, hf = rglru_direction(z, rg_conv_w, rg_conv_b, rg_wr, rg_br, rg_wi, rg_bi, rg_lam, state_rglru_h,
                                     l, dr, bc, tc, bl, tl)
            rh.append(rd)
            new_h.append(hf[:bc, 0])
        new_k.append(z[:nc, _C_K:_C_K + KV_WIDTH].reshape(bc, tc, KV_HEADS, HEAD_DIM))
        new_v.append(z[:nc, _C_V:_C_V + KV_WIDTH].reshape(bc, tc, KV_HEADS, HEAD_DIM))

        x = out_proj(x, a, mh[0], mh[1], rh[0], rh[1], z, mlstm_norm_g[l], mod, w_out_b[l], nc, tl, tm=tm)

        w_router = _pad_lanes(jnp.concatenate([router_wg[l], router_we[l]], axis=1))
        b_router = _pad_lanes(jnp.concatenate([router_bg[l], router_be[l]])[None, :])
        xp, route_g, route_e = moe_pre(x, norm2_g[l], mod, w_router, b_router, nc, tl)
        dest, slot_tok, blk_e, blk_src, n_used = moe_layout(route_e[:, :TOP_K])
        yb = moe_experts(xp[slot_tok], blk_e, blk_src, n_used, exp_w_gate, exp_w_up, exp_w_down, l)
        x = moe_combine(x, yb[dest[:, 0]], yb[dest[:, 1]], route_g, mod, nc, tl)

    y = final_norm(x, final_norm_g)
    stack2 = lambda parts: jnp.stack([jnp.stack(parts[2 * l:2 * l + 2], axis=1) for l in range(depth)], axis=1)
    return (y[:nc].reshape(bc, tc, d), y[nc:].reshape(bl, tl, d),
            jnp.stack(new_k, axis=1), jnp.stack(new_v, axis=1),
            stack2(new_c), stack2(new_n), stack2(new_m), stack2(new_h))
```

```python
import functools
import math

import jax
import jax.numpy as jnp
from jax import lax
from jax.experimental import pallas as pl
from jax.experimental.pallas import tpu as pltpu

F32 = jnp.float32
BF16 = jnp.bfloat16
U32 = jnp.uint32
I32 = jnp.int32

D_MODEL = 2048
HEAD_DIM = 128
A_WIDTH = D_MODEL // 2
N_HEADS = A_WIDTH // HEAD_DIM
KV_HEADS = 2
GQA_GROUP = N_HEADS // KV_HEADS
KV_WIDTH = KV_HEADS * HEAD_DIM
WINDOW = 128
Q_BLOCK = 128
GRID_W = 64
ROPE_BASE = 10000.0
ROPE_AXIS = HEAD_DIM // 2
ATTN_SCALE = HEAD_DIM ** -0.5
NEG_INF = -1e30
M_WIDTH = D_MODEL // 4
M_HEADS = 4
M_DK = M_WIDTH // M_HEADS
M_GATES = 2 * 2 * M_HEADS
R_WIDTH = D_MODEL // 4
R_BLOCKS = 4
R_BW = R_WIDTH // R_BLOCKS
CONV_W = 4
RG_C = 8.0
N_GROUPS = 4
EXPERTS_PER_GROUP = 8
N_EXPERTS = N_GROUPS * EXPERTS_PER_GROUP
TOP_K = 2
D_EXPERT = D_MODEL // 2
NORM_EPS = 1e-6

LANES = 128
SUBLANES = 8
Z_MAIN = A_WIDTH + 2 * KV_WIDTH + 4 * M_WIDTH + 2 * R_WIDTH
_C_Q, _C_K, _C_V = 0, A_WIDTH, A_WIDTH + KV_WIDTH
_C_QM = A_WIDTH + 2 * KV_WIDTH
_C_KM, _C_VM, _C_OM = _C_QM + M_WIDTH, _C_QM + 2 * M_WIDTH, _C_QM + 3 * M_WIDTH
_C_XR, _C_YR = _C_QM + 4 * M_WIDTH, _C_QM + 4 * M_WIDTH + R_WIDTH

M_CHUNK = LANES
R_CHUNK = 256
MOE_TB = 512
VMEM_LIMIT = 56 * 1024 * 1024
HALF = D_MODEL // 2


def _cparams(sem):
    return pltpu.CompilerParams(dimension_semantics=sem, vmem_limit_bytes=VMEM_LIMIT)


def _group_of_tile(row0, nc, tl):
    return jnp.where(row0 < nc, 0, 1 + (jnp.maximum(row0 - nc, 0)) // tl)


def _adaln_kernel(c_ref, w_ref, b_ref, o_ref):
    c = c_ref[...]
    s = (c * jax.nn.sigmoid(c)).astype(BF16)
    o_ref[...] = jnp.dot(s, w_ref[...].astype(BF16), preferred_element_type=F32) + b_ref[...]


def adaln_all(cvec, ada_w, ada_b, tn=1024):
    depth, d, d6 = ada_w.shape
    return pl.pallas_call(
        _adaln_kernel,
        out_shape=jax.ShapeDtypeStruct((depth, SUBLANES, d6), F32),
        grid=(depth, d6 // tn),
        in_specs=[
            pl.BlockSpec((SUBLANES, d), lambda l, j: (0, 0)),
            pl.BlockSpec((None, d, tn), lambda l, j: (l, 0, j)),
            pl.BlockSpec((None, 1, tn), lambda l, j: (l, 0, j)),
        ],
        out_specs=pl.BlockSpec((None, SUBLANES, tn), lambda l, j: (l, 0, j)),
        compiler_params=_cparams(("arbitrary", "arbitrary")),
        name="adaln",
    )(cvec, ada_w, ada_b.reshape(depth, 1, d6))


def _in_kernel(x_ref, g_ref, sh_ref, sc_ref, w_ref, wg_ref, z_ref, zg_ref, xn_ref):
    @pl.when(pl.program_id(1) == 0)
    def _():
        x = x_ref[...]
        y = x * lax.rsqrt(jnp.mean(x * x, axis=-1, keepdims=True) + NORM_EPS) * g_ref[...]
        xn = (y * (1.0 + sc_ref[...]) + sh_ref[...]).astype(BF16)
        xn_ref[...] = xn
        zg_ref[...] = jnp.dot(xn, wg_ref[...], preferred_element_type=F32)

    z_ref[...] = jnp.dot(xn_ref[...], w_ref[...], preferred_element_type=F32)


def in_proj(x, norm_g, mod, w_main, w_gate, nc, tl, tm=1024, tn=768):
    n, d = x.shape
    zw = w_main.shape[1]
    grp = lambda i: _group_of_tile(i * tm, nc, tl)
    return pl.pallas_call(
        _in_kernel,
        out_shape=(jax.ShapeDtypeStruct((n, zw), F32), jax.ShapeDtypeStruct((n, LANES), F32)),
        grid=(n // tm, zw // tn),
        in_specs=[
            pl.BlockSpec((tm, d), lambda i, j: (i, 0)),
            pl.BlockSpec((1, d), lambda i, j: (0, 0)),
            pl.BlockSpec((None, None, 1, d), lambda i, j: (grp(i), 0, 0, 0)),
            pl.BlockSpec((None, None, 1, d), lambda i, j: (grp(i), 1, 0, 0)),
            pl.BlockSpec((d, tn), lambda i, j: (0, j)),
            pl.BlockSpec((d, LANES), lambda i, j: (0, 0)),
        ],
        out_specs=(pl.BlockSpec((tm, tn), lambda i, j: (i, j)), pl.BlockSpec((tm, LANES), lambda i, j: (i, 0))),
        scratch_shapes=[pltpu.VMEM((tm, d), BF16)],
        compiler_params=_cparams(("arbitrary", "arbitrary")),
        name="in_proj",
    )(x, norm_g.reshape(1, d), mod, mod, w_main, w_gate)


def _rope(x, c, s):
    lane = lax.broadcasted_iota(I32, x.shape, 1)
    half = ROPE_AXIS // 2
    partner = jnp.where((lane & (ROPE_AXIS - 1)) < half, pltpu.roll(x, HEAD_DIM - half, 1), pltpu.roll(x, half, 1))
    return x * c + partner * s


def _softmax_pv(parts, sink_col):
    m = sink_col
    for s, _ in parts:
        m = jnp.maximum(m, jnp.max(s, axis=-1, keepdims=True))
    den = jnp.exp(sink_col - m)
    acc = None
    for s, v in parts:
        p = jnp.exp(s - m)
        den = den + jnp.sum(p, axis=-1, keepdims=True)
        pv = jnp.dot(p.astype(BF16), v, preferred_element_type=F32)
        acc = pv if acc is None else acc + pv
    return acc / den


def _qk(q, k):
    return lax.dot_general(q, k, (((1,), (1,)), ((), ())), preferred_element_type=F32) * ATTN_SCALE


def _sink_column(sink_ref, n, rows):
    ridx = lax.broadcasted_iota(I32, (GQA_GROUP * rows, 1), 0)
    col = jnp.full((GQA_GROUP * rows, 1), sink_ref[n * GQA_GROUP], F32)
    for g in range(1, GQA_GROUP):
        col = jnp.where(ridx >= g * rows, sink_ref[n * GQA_GROUP + g], col)
    return col


def _attn_ctx_kernel(sink_ref, q_ref, k_ref, v_ref, o_ref):
    t = q_ref.shape[0]
    for n in range(KV_HEADS):
        k = k_ref[:, n * HEAD_DIM:(n + 1) * HEAD_DIM].astype(BF16)
        v = v_ref[:, n * HEAD_DIM:(n + 1) * HEAD_DIM].astype(BF16)
        q = jnp.concatenate(
            [q_ref[:, (n * GQA_GROUP + g) * HEAD_DIM:(n * GQA_GROUP + g + 1) * HEAD_DIM] for g in range(GQA_GROUP)],
            axis=0).astype(BF16)
        out = _softmax_pv([(_qk(q, k), v)], _sink_column(sink_ref, n, t))
        for g in range(GQA_GROUP):
            h = n * GQA_GROUP + g
            o_ref[:, h * HEAD_DIM:(h + 1) * HEAD_DIM] = out[g * t:(g + 1) * t].astype(o_ref.dtype)


def attention_context(z, sink, b, t):
    return pl.pallas_call(
        _attn_ctx_kernel,
        out_shape=jax.ShapeDtypeStruct((b * t, A_WIDTH), BF16),
        grid=(b,),
        in_specs=[
            pl.BlockSpec(memory_space=pltpu.SMEM),
            pl.BlockSpec((t, A_WIDTH), lambda i: (i, _C_Q // A_WIDTH)),
            pl.BlockSpec((t, KV_WIDTH), lambda i: (i, _C_K // KV_WIDTH)),
            pl.BlockSpec((t, KV_WIDTH), lambda i: (i, _C_V // KV_WIDTH)),
        ],
        out_specs=pl.BlockSpec((t, A_WIDTH), lambda i: (i, 0)),
        compiler_params=_cparams(("arbitrary",)),
        name="attn_ctx",
    )(sink, z, z, z)


def _attn_lat_kernel(sink_ref, q_ref, k_ref, v_ref, kc_ref, vc_ref, cq_ref, sq_ref, ck_ref, sk_ref, o_ref,
                     kr_ref, vp_ref, *, t):
    i = pl.program_id(1)
    rope_rows = 512

    @pl.when(i == 0)
    def _():
        zpad = jnp.zeros((WINDOW, KV_WIDTH), BF16)
        kr_ref[0:WINDOW, :] = zpad
        kr_ref[WINDOW + t:2 * WINDOW + t, :] = zpad
        vp_ref[0:WINDOW, :] = zpad
        vp_ref[WINDOW + t:2 * WINDOW + t, :] = zpad

        def body(c, carry):
            r0 = pl.multiple_of(c * rope_rows, rope_rows)
            cs, sn = ck_ref[pl.ds(r0, rope_rows), :], sk_ref[pl.ds(r0, rope_rows), :]
            for n in range(KV_HEADS):
                kk = k_ref[pl.ds(r0, rope_rows), n * HEAD_DIM:(n + 1) * HEAD_DIM]
                kr_ref[pl.ds(WINDOW + r0, rope_rows), n * HEAD_DIM:(n + 1) * HEAD_DIM] = _rope(kk, cs, sn).astype(BF16)
            vp_ref[pl.ds(WINDOW + r0, rope_rows), :] = v_ref[pl.ds(r0, rope_rows), :].astype(BF16)
            return carry

        lax.fori_loop(0, t // rope_rows, body, 0)

    span = Q_BLOCK + 2 * WINDOW
    rows = GQA_GROUP * Q_BLOCK
    r = lax.broadcasted_iota(I32, (rows, span), 0) & (Q_BLOCK - 1)
    c = lax.broadcasted_iota(I32, (rows, span), 1)
    kpos = (i - 1) * Q_BLOCK + c
    mask = (c >= r) & (c <= r + 2 * WINDOW) & (kpos >= 0) & (kpos < t)
    w0 = pl.multiple_of(i * Q_BLOCK, Q_BLOCK)
    cq, sq = cq_ref[...], sq_ref[...]
    for n in range(KV_HEADS):
        hs = slice(n * HEAD_DIM, (n + 1) * HEAD_DIM)
        q = jnp.concatenate(
            [_rope(q_ref[:, (n * GQA_GROUP + g) * HEAD_DIM:(n * GQA_GROUP + g + 1) * HEAD_DIM], cq, sq)
             for g in range(GQA_GROUP)], axis=0).astype(BF16)
        s_win = jnp.where(mask, _qk(q, kr_ref[pl.ds(w0, span), hs]), NEG_INF)
        s_ctx = _qk(q, kc_ref[:, hs].astype(BF16))
        out = _softmax_pv([(s_win, vp_ref[pl.ds(w0, span), hs]), (s_ctx, vc_ref[:, hs].astype(BF16))],
                          _sink_column(sink_ref, n, Q_BLOCK))
        for g in range(GQA_GROUP):
            h = n * GQA_GROUP + g
            o_ref[:, h * HEAD_DIM:(h + 1) * HEAD_DIM] = out[g * Q_BLOCK:(g + 1) * Q_BLOCK].astype(o_ref.dtype)


def attention_latent(z, cache_k, cache_v, layer, sink, rope_c, rope_s, row0, b, t):
    nqb = t // Q_BLOCK
    past = cache_k.shape[2]
    qb0 = row0 // Q_BLOCK
    tb0 = row0 // t
    kern = functools.partial(_attn_lat_kernel, t=t)
    return pl.pallas_call(
        kern,
        out_shape=jax.ShapeDtypeStruct((b * t, A_WIDTH), BF16),
        grid=(b, nqb),
        in_specs=[
            pl.BlockSpec(memory_space=pltpu.SMEM),
            pl.BlockSpec((Q_BLOCK, A_WIDTH), lambda bi, i: (qb0 + bi * nqb + i, _C_Q // A_WIDTH)),
            pl.BlockSpec((t, KV_WIDTH), lambda bi, i: (tb0 + bi, _C_K // KV_WIDTH)),
            pl.BlockSpec((t, KV_WIDTH), lambda bi, i: (tb0 + bi, _C_V // KV_WIDTH)),
            pl.BlockSpec((None, None, past, KV_WIDTH), lambda bi, i: (bi, layer, 0, 0)),
            pl.BlockSpec((None, None, past, KV_WIDTH), lambda bi, i: (bi, layer, 0, 0)),
            pl.BlockSpec((Q_BLOCK, HEAD_DIM), lambda bi, i: (i, 0)),
            pl.BlockSpec((Q_BLOCK, HEAD_DIM), lambda bi, i: (i, 0)),
            pl.BlockSpec((t, HEAD_DIM), lambda bi, i: (0, 0)),
            pl.BlockSpec((t, HEAD_DIM), lambda bi, i: (0, 0)),
        ],
        out_specs=pl.BlockSpec((Q_BLOCK, A_WIDTH), lambda bi, i: (bi * nqb + i, 0)),
        scratch_shapes=[pltpu.VMEM((t + 2 * WINDOW, KV_WIDTH), BF16), pltpu.VMEM((t + 2 * WINDOW, KV_WIDTH), BF16)],
        compiler_params=_cparams(("arbitrary", "arbitrary")),
        name="attn_lat",
    )(sink, z, z, z, cache_k, cache_v, rope_c, rope_s, rope_c, rope_s)


def rope_tables(t):
    pos = jnp.arange(t)
    inv = ROPE_BASE ** (-jnp.arange(0, ROPE_AXIS, 2, dtype=F32) / ROPE_AXIS)

    def cs(p):
        ang = p.astype(F32)[:, None] * inv[None, :]
        return jnp.cos(ang), jnp.sin(ang)

    cr, sr = cs(pos // GRID_W)
    cc, sc = cs(pos % GRID_W)
    return (jnp.concatenate([cr, cr, cc, cc], axis=-1), jnp.concatenate([-sr, sr, -sc, sc], axis=-1))


class _Seqs:
    def __init__(self, n_ctx, t_ctx, n_lat, t_lat, chunk, reverse):
        self.n_ctx, self.t_ctx, self.n_lat, self.t_lat = n_ctx, t_ctx, n_lat, t_lat
        self.chunk, self.reverse = chunk, reverse
        self.cpc, self.cpl = t_ctx // chunk, t_lat // chunk
        self.ctx_chunks = n_ctx * self.cpc
        self.n_chunks = self.ctx_chunks + n_lat * self.cpl

    def chunk_of_step(self, s):
        return (self.n_chunks - 1 - s) if self.reverse else s

    def info(self, g):
        is_ctx = g < self.ctx_chunks
        gl = jnp.maximum(g - self.ctx_chunks, 0)
        gc = jnp.minimum(g, self.ctx_chunks - 1)
        pos = jnp.where(is_ctx, gc % self.cpc, gl % self.cpl)
        per = jnp.where(is_ctx, self.cpc, self.cpl)
        lat = gl // self.cpl
        seq = jnp.where(is_ctx, gc // self.cpc, self.n_ctx + lat)
        head, tail = pos == 0, pos == per - 1
        return is_ctx, seq, lat, (tail if self.reverse else head), (head if self.reverse else tail)


def _lane_scan(x, op, reverse):
    n = x.shape[1]
    lane = lax.broadcasted_iota(I32, x.shape, 1)
    s = 1
    while s < n:
        if reverse:
            x = jnp.where(lane < n - s, op(x, pltpu.roll(x, n - s, 1)), x)
        else:
            x = jnp.where(lane >= s, op(x, pltpu.roll(x, s, 1)), x)
        s *= 2
    return x


_GROWS = 2 * SUBLANES


def _mlstm_gate_kernel(g_ref, gb_ref, k_ref, rows_ref, tiles_ref, kt_ref):
    L = M_CHUNK
    nck = g_ref.shape[0] // L
    g = g_ref[...] + gb_ref[...]
    s = jnp.concatenate([g[c * L:(c + 1) * L].T[0:_GROWS] for c in range(nck)], axis=0)
    row = lax.broadcasted_iota(I32, s.shape, 0)
    is_rev = (row & SUBLANES) != 0
    lf = jax.nn.log_sigmoid(s)
    f = jnp.where(is_rev, _lane_scan(lf, jnp.add, True), _lane_scan(lf, jnp.add, False))
    f = pltpu.roll(f, s.shape[0] - M_HEADS, 0)
    r = s - f
    cm = jnp.where(is_rev, _lane_scan(r, jnp.maximum, True), _lane_scan(r, jnp.maximum, False))
    rows_ref[...] = r
    pad = jnp.zeros((LANES - 2 * _GROWS, L), F32)
    scale = M_DK ** -0.5
    for c in range(nck):
        sl = slice(c * _GROWS, (c + 1) * _GROWS)
        cols = jnp.concatenate([cm[sl], f[sl], pad], axis=0).T
        for d in range(2):
            for h in range(M_HEADS):
                gi = d * SUBLANES + h
                tiles_ref[c, d, h] = jnp.broadcast_to(cols[:, gi:gi + 1], (L, LANES))
                tiles_ref[c, d, M_HEADS + h] = jnp.broadcast_to(cols[:, _GROWS + gi:_GROWS + gi + 1], (L, LANES))
        for h in range(M_HEADS):
            kt_ref[c, h] = (k_ref[c * L:(c + 1) * L, h * M_DK:(h + 1) * M_DK] * scale).T


def mlstm_gates(z, zg, gate_b, tile=512):
    n = zg.shape[0]
    L = M_CHUNK
    nck = tile // L
    return pl.pallas_call(
        _mlstm_gate_kernel,
        out_shape=(jax.ShapeDtypeStruct((n // L * _GROWS, L), F32),
                   jax.ShapeDtypeStruct((n // L, 2, 2 * M_HEADS, L, LANES), F32),
                   jax.ShapeDtypeStruct((n // L, M_HEADS, M_DK, L), F32)),
        grid=(n // tile,),
        in_specs=[pl.BlockSpec((tile, LANES), lambda i: (i, 0)), pl.BlockSpec((1, LANES), lambda i: (0, 0)),
                  pl.BlockSpec((tile, M_WIDTH), lambda i: (i, _C_KM // M_WIDTH))],
        out_specs=(pl.BlockSpec((nck * _GROWS, L), lambda i: (i, 0)),
                   pl.BlockSpec((nck, 2, 2 * M_HEADS, L, LANES), lambda i: (i, 0, 0, 0, 0)),
                   pl.BlockSpec((nck, M_HEADS, M_DK, L), lambda i: (i, 0, 0, 0))),
        compiler_params=_cparams(("arbitrary",)),
        name="mlstm_gates",
    )(zg, gate_b, z)


def _mlstm_kernel(m0_ref, qf_ref, vf_ref, ktf_ref, rf_ref, tf_ref, qb_ref, vb_ref, ktb_ref, rb_ref, tb_ref,
                  cn0f_ref, cn0b_ref, hf_ref, hb_ref, ocnf_ref, omf_ref, ocnb_ref, omb_ref,
                  cn_s, m_s, *, seqs, layer, depth):
    L = M_CHUNK
    step = pl.program_id(0)
    ti = lax.broadcasted_iota(I32, (L, L), 0)
    si = lax.broadcasted_iota(I32, (L, L), 1)
    ones = jnp.ones((L, M_DK), BF16)
    streams = (
        (0, seqs[0], qf_ref, vf_ref, ktf_ref, rf_ref, tf_ref, cn0f_ref, hf_ref, ocnf_ref, omf_ref),
        (1, seqs[1], qb_ref, vb_ref, ktb_ref, rb_ref, tb_ref, cn0b_ref, hb_ref, ocnb_ref, omb_ref),
    )
    for d, sq, q_ref, v_ref, kt_ref, rows_ref, tiles_ref, cn0_ref, h_ref, ocn_ref, om_ref in streams:
        rev = d == 1
        is_ctx, _, lat, first, last = sq.info(sq.chunk_of_step(step))

        @pl.when(first)
        def _(d=d, is_ctx=is_ctx, lat=lat, cn0_ref=cn0_ref):
            keep = jnp.where(is_ctx, 0.0, 1.0).astype(F32)
            cn_s[d] = cn0_ref[...] * keep
            m_s[d] = jnp.zeros((SUBLANES, LANES), F32)
            for h in range(M_HEADS):
                m0 = m0_ref[((lat * depth + layer) * 2 + d) * M_HEADS + h]
                m_s[d, h:h + 1, :] = jnp.full((1, LANES), m0, F32) * keep

        end = 0 if rev else L - 1
        causal = (si >= ti) if rev else (si <= ti)
        for h in range(M_HEADS):
            hs = slice(h * M_DK, (h + 1) * M_DK)
            gi = d * SUBLANES + h
            q = q_ref[:, hs].astype(BF16)
            vaug = jnp.concatenate([v_ref[:, hs].astype(BF16), ones], axis=1)
            kt = kt_ref[h]
            r_row = rows_ref[gi:gi + 1, :]
            f_b = tiles_ref[M_HEADS + h]
            m_old = m_s[d, h:h + 1, :]
            m_b = jnp.maximum(m_old, tiles_ref[h])
            dmat = jnp.where(causal, jnp.exp(jnp.where(causal, r_row - m_b, 0.0)), 0.0)
            s = jnp.dot(q, kt.astype(BF16), preferred_element_type=F32) * dmat
            wpq = (jnp.exp(m_old - m_b) * q.astype(F32)).astype(BF16)
            cn_old = cn_s[d, h]
            out = jnp.dot(jnp.concatenate([s.astype(BF16), wpq], axis=1),
                          jnp.concatenate([vaug, cn_old.astype(BF16)], axis=0), preferred_element_type=F32)
            h_ref[:, hs] = out[:, :M_DK] / jnp.maximum(jnp.abs(out[:, M_DK:]), jnp.exp(-(f_b + m_b)))
            m_end = m_b[end:end + 1, :]
            dec = jnp.exp(m_old - m_end)
            kwt = (kt * jnp.exp(r_row - m_end)).astype(BF16)
            cn_s[d, h] = jnp.concatenate([dec, dec], axis=1) * cn_old + jnp.dot(kwt, vaug, preferred_element_type=F32)
            m_s[d, h:h + 1, :] = f_b[end:end + 1, :] + m_end

        @pl.when(last)
        def _(d=d, ocn_ref=ocn_ref, om_ref=om_ref):
            ocn_ref[...] = cn_s[d]
            om_ref[...] = m_s[d]


def mlstm_scan(z, rows, tiles, kt, cn0, st_m, layer, n_ctx, t_ctx, n_lat, t_lat):
    L = M_CHUNK
    seqs = (_Seqs(n_ctx, t_ctx, n_lat, t_lat, L, False), _Seqs(n_ctx, t_ctx, n_lat, t_lat, L, True))
    n_seq = n_ctx + n_lat
    n_chunks = seqs[0].n_chunks
    kern = functools.partial(_mlstm_kernel, seqs=seqs, layer=layer, depth=st_m.shape[1])
    in_specs = [pl.BlockSpec(memory_space=pltpu.SMEM)]
    for d, sq in enumerate(seqs):
        cm = lambda s, sq=sq: sq.chunk_of_step(s)
        in_specs += [
            pl.BlockSpec((L, M_WIDTH), lambda s, cm=cm: (cm(s), _C_QM // M_WIDTH)),
            pl.BlockSpec((L, M_WIDTH), lambda s, cm=cm: (cm(s), _C_VM // M_WIDTH)),
            pl.BlockSpec((None, M_HEADS, M_DK, L), lambda s, cm=cm: (cm(s), 0, 0, 0)),
            pl.BlockSpec((_GROWS, L), lambda s, cm=cm: (cm(s), 0)),
            pl.BlockSpec((None, None, 2 * M_HEADS, L, LANES), lambda s, cm=cm, d=d: (cm(s), d, 0, 0, 0)),
        ]
    for d, sq in enumerate(seqs):
        lat_of = lambda s, sq=sq: sq.info(sq.chunk_of_step(s))[2]
        in_specs.append(pl.BlockSpec((None, None, None, M_HEADS, M_DK, 2 * M_DK),
                                     lambda s, f=lat_of, d=d: (f(s), layer, d, 0, 0, 0)))
    out_shape = [jax.ShapeDtypeStruct((n_chunks * L, M_WIDTH), F32)] * 2
    out_specs = [pl.BlockSpec((L, M_WIDTH), lambda s, sq=sq: (sq.chunk_of_step(s), 0)) for sq in seqs]
    for sq in seqs:
        seq_of = lambda s, sq=sq: sq.info(sq.chunk_of_step(s))[1]
        out_shape += [jax.ShapeDtypeStruct((n_seq, M_HEADS, M_DK, 2 * M_DK), F32),
                      jax.ShapeDtypeStruct((n_seq, SUBLANES, LANES), F32)]
        out_specs += [pl.BlockSpec((None, M_HEADS, M_DK, 2 * M_DK), lambda s, f=seq_of: (f(s), 0, 0, 0)),
                      pl.BlockSpec((None, SUBLANES, LANES), lambda s, f=seq_of: (f(s), 0, 0))]
    args = [st_m.reshape(-1), z, z, kt, rows, tiles, z, z, kt, rows, tiles, cn0, cn0]
    return pl.pallas_call(
        kern,
        out_shape=tuple(out_shape),
        grid=(n_chunks,),
        in_specs=in_specs,
        out_specs=tuple(out_specs),
        scratch_shapes=[pltpu.VMEM((2, M_HEADS, M_DK, 2 * M_DK), F32), pltpu.VMEM((2, SUBLANES, LANES), F32)],
        compiler_params=_cparams(("arbitrary",)),
        name="mlstm_scan",
    )(*args)


def _rglru_kernel(x_ref, xp_ref, xn_ref, cw_ref, cb_ref, wr_ref, br_ref, wi_ref, bi_ref, lam_ref, h0_ref,
                  h_ref, hf_ref, xpad, carry, *, seqs):
    L = seqs.chunk
    rev = seqs.reverse
    g = seqs.chunk_of_step(pl.program_id(0))
    is_ctx, _, _, first, last = seqs.info(g)
    head = last if rev else first
    tail = first if rev else last
    halo = SUBLANES
    xpad[0:halo, :] = xp_ref[...] * jnp.where(head, 0.0, 1.0).astype(F32)
    xpad[halo:halo + L, :] = x_ref[...]
    xpad[halo + L:2 * halo + L, :] = xn_ref[...] * jnp.where(tail, 0.0, 1.0).astype(F32)
    x = cb_ref[...]
    for j in range(CONV_W):
        x = x + cw_ref[j:j + 1, :] * xpad[halo - 2 + j:halo - 2 + j + L, :]

    rs, is_ = [], []
    for n in range(R_BLOCKS):
        xb = x[:, n * R_BW:(n + 1) * R_BW].astype(BF16)
        rs.append(jnp.dot(xb, wr_ref[n].astype(BF16), preferred_element_type=F32))
        is_.append(jnp.dot(xb, wi_ref[n].astype(BF16), preferred_element_type=F32))
    rg = jax.nn.sigmoid(jnp.concatenate(rs, axis=-1) + br_ref[...])
    ig = jax.nn.sigmoid(jnp.concatenate(is_, axis=-1) + bi_ref[...])
    log_a = -RG_C * rg * jax.nn.softplus(-lam_ref[...])
    a = jnp.exp(log_a)
    u = jnp.sqrt(1.0 - jnp.exp(2.0 * log_a)) * (ig * x)

    row = lax.broadcasted_iota(I32, (L, R_WIDTH), 0)
    s = 1
    while s < L:
        if rev:
            ok = row < L - s
            a_sh, u_sh = pltpu.roll(a, L - s, 0), pltpu.roll(u, L - s, 0)
        else:
            ok = row >= s
            a_sh, u_sh = pltpu.roll(a, s, 0), pltpu.roll(u, s, 0)
        u = jnp.where(ok, a * u_sh + u, u)
        a = jnp.where(ok, a * a_sh, a)
        s *= 2

    @pl.when(first)
    def _():
        carry[...] = h0_ref[...] * jnp.where(is_ctx, 0.0, 1.0).astype(F32)

    h = a * carry[...] + u
    h_ref[...] = h
    end = 0 if rev else L - 1
    carry[...] = h[end:end + 1, :]

    @pl.when(last)
    def _():
        hf_ref[...] = h[end:end + 1, :]


def rglru_direction(z, conv_w, conv_b, wr, br, wi, bi, lam, h0, layer, direction, n_ctx, t_ctx, n_lat, t_lat):
    seqs = _Seqs(n_ctx, t_ctx, n_lat, t_lat, R_CHUNK, direction == 1)
    n_seq = n_ctx + n_lat
    L = R_CHUNK
    n_rows = seqs.n_chunks * L
    hb = L // SUBLANES
    nb8 = n_rows // SUBLANES
    cm = lambda s: seqs.chunk_of_step(s)
    lat_of = lambda s: seqs.info(cm(s))[2]
    seq_of = lambda s: seqs.info(cm(s))[1]
    xcol = _C_XR // R_WIDTH
    vec_ld = lambda: pl.BlockSpec((None, None, 1, R_WIDTH), lambda s: (layer, direction, 0, 0))
    mat_ld = lambda: pl.BlockSpec((None, None, R_BLOCKS, R_BW, R_BW), lambda s: (layer, direction, 0, 0, 0))
    kern = functools.partial(_rglru_kernel, seqs=seqs)
    depth = conv_w.shape[0]
    r4 = lambda a: a.reshape(depth, 2, 1, R_WIDTH)
    return pl.pallas_call(
        kern,
        out_shape=(jax.ShapeDtypeStruct((n_rows, R_WIDTH), F32), jax.ShapeDtypeStruct((n_seq, 1, R_WIDTH), F32)),
        grid=(seqs.n_chunks,),
        in_specs=[
            pl.BlockSpec((L, R_WIDTH), lambda s: (cm(s), xcol)),
            pl.BlockSpec((SUBLANES, R_WIDTH), lambda s: (jnp.maximum(cm(s) * hb - 1, 0), xcol)),
            pl.BlockSpec((SUBLANES, R_WIDTH), lambda s: (jnp.minimum((cm(s) + 1) * hb, nb8 - 1), xcol)),
            pl.BlockSpec((None, CONV_W, R_WIDTH), lambda s: (layer, 0, 0)),
            pl.BlockSpec((None, 1, R_WIDTH), lambda s: (layer, 0, 0)),
            mat_ld(), vec_ld(), mat_ld(), vec_ld(), vec_ld(),
            pl.BlockSpec((None, None, None, 1, R_WIDTH), lambda s: (lat_of(s), layer, direction, 0, 0)),
        ],
        out_specs=(pl.BlockSpec((L, R_WIDTH), lambda s: (cm(s), 0)),
                   pl.BlockSpec((None, 1, R_WIDTH), lambda s: (seq_of(s), 0, 0))),
        scratch_shapes=[pltpu.VMEM((L + 2 * SUBLANES, R_WIDTH), F32), pltpu.VMEM((1, R_WIDTH), F32)],
        compiler_params=_cparams(("arbitrary",)),
        name=f"rglru_d{direction}",
    )(z, z, z, conv_w, conv_b.reshape(depth, 1, R_WIDTH), wr, r4(br), wi, r4(bi), r4(lam),
      h0.reshape(h0.shape[0], depth, 2, 1, R_WIDTH))


def _gelu_tanh(x):
    return 0.5 * x * (1.0 + jnp.tanh(math.sqrt(2.0 / math.pi) * (x + 0.044715 * (x * x * x))))


def _out_kernel(x_ref, a_ref, mf_ref, mb_ref, om_ref, rf_ref, rb_ref, yr_ref, mg_ref, gate_ref, w_ref,
                o_ref, mix_ref):
    @pl.when(pl.program_id(1) == 0)
    def _():
        mix_ref[:, 0:A_WIDTH] = a_ref[...]
        hs = mf_ref[...] + mb_ref[...]
        for h in range(M_HEADS):
            cs = slice(h * M_DK, (h + 1) * M_DK)
            hh = hs[:, cs]
            hn = hh * lax.rsqrt(jnp.mean(hh * hh, axis=-1, keepdims=True) + NORM_EPS) * mg_ref[:, cs]
            mix_ref[:, A_WIDTH + h * M_DK:A_WIDTH + (h + 1) * M_DK] = (
                hn * jax.nn.sigmoid(om_ref[:, cs])).astype(BF16)
        mix_ref[:, A_WIDTH + M_WIDTH:] = ((rf_ref[...] + rb_ref[...]) * _gelu_tanh(yr_ref[...])).astype(BF16)

    o_ref[...] = x_ref[...] + gate_ref[...] * jnp.dot(mix_ref[...], w_ref[...], preferred_element_type=F32)


def out_proj(x, a, mf, mb, rf, rb, z, mnorm_g, mod, w_out, nc, tl, tm=1024, tn=512):
    n, d = x.shape
    grp = lambda i: _group_of_tile(i * tm, nc, tl)
    rowblk = lambda w, col: pl.BlockSpec((tm, w), lambda i, j: (i, col))
    return pl.pallas_call(
        _out_kernel,
        out_shape=jax.ShapeDtypeStruct((n, d), F32),
        grid=(n // tm, d // tn),
        in_specs=[
            pl.BlockSpec((tm, tn), lambda i, j: (i, j)),
            rowblk(A_WIDTH, 0),
            rowblk(M_WIDTH, 0), rowblk(M_WIDTH, 0), rowblk(M_WIDTH, _C_OM // M_WIDTH),
            rowblk(R_WIDTH, 0), rowblk(R_WIDTH, 0), rowblk(R_WIDTH, _C_YR // R_WIDTH),
            pl.BlockSpec((1, M_WIDTH), lambda i, j: (0, 0)),
            pl.BlockSpec((None, None, 1, tn), lambda i, j: (grp(i), 2, 0, j)),
            pl.BlockSpec((d, tn), lambda i, j: (0, j)),
        ],
        out_specs=pl.BlockSpec((tm, tn), lambda i, j: (i, j)),
        scratch_shapes=[pltpu.VMEM((tm, d), BF16)],
        compiler_params=_cparams(("arbitrary", "arbitrary")),
        name="out_proj",
    )(x, a, mf, mb, z, rf, rb, z, mnorm_g.reshape(1, -1), mod, w_out)


def _bits(x):
    return lax.bitcast_convert_type(x, U32)


def _moe_pre_kernel(x_ref, g_ref, sh_ref, sc_ref, wr_ref, br_ref, xp_ref, rg_ref, re_ref):
    x = x_ref[...]
    y = x * lax.rsqrt(jnp.mean(x * x, axis=-1, keepdims=True) + NORM_EPS) * g_ref[...]
    xn = y * (1.0 + sc_ref[...]) + sh_ref[...]
    lo = _bits(xn[:, :HALF].astype(BF16).astype(F32))
    hi = _bits(xn[:, HALF:].astype(BF16).astype(F32))
    xp_ref[...] = (hi & jnp.uint32(0xFFFF0000)) | (lo >> 16)

    lg = jnp.dot(xn, wr_ref[...], preferred_element_type=F32, precision=lax.Precision.HIGHEST) + br_ref[...]
    lane = lax.broadcasted_iota(I32, lg.shape, 1).astype(F32)
    ninf = jnp.float32(-jnp.inf)

    def top(mask):
        val = jnp.max(jnp.where(mask, lg, ninf), axis=-1, keepdims=True)
        idx = jnp.min(jnp.where(mask & (lg == val), lane, float(LANES)), axis=-1, keepdims=True)
        return val, idx

    is_g = lane < N_GROUPS
    g_val, g_idx = top(is_g)
    g_w = 1.0 / jnp.sum(jnp.where(is_g, jnp.exp(lg - g_val), 0.0), axis=-1, keepdims=True)
    e_lo = N_GROUPS + g_idx * EXPERTS_PER_GROUP
    in_grp = (lane >= e_lo) & (lane < e_lo + EXPERTS_PER_GROUP)
    v1, i1 = top(in_grp)
    v2, i2 = top(in_grp & (lane != i1))
    t = jnp.exp(v2 - v1)
    w1 = g_w / (1.0 + t)
    rg_ref[...] = jnp.where(lane == 0, w1, jnp.where(lane == 1, w1 * t, 0.0))
    re_ref[...] = jnp.where(lane == 0, i1 - N_GROUPS, jnp.where(lane == 1, i2 - N_GROUPS, 0.0)).astype(I32)


def moe_pre(x, norm_g, mod, w_router, b_router, nc, tl, tm=512):
    n, d = x.shape
    grp = lambda i: _group_of_tile(i * tm, nc, tl)
    lane_out = lambda: pl.BlockSpec((tm, LANES), lambda i: (i, 0))
    return pl.pallas_call(
        _moe_pre_kernel,
        out_shape=(jax.ShapeDtypeStruct((n, HALF), U32), jax.ShapeDtypeStruct((n, LANES), F32),
                   jax.ShapeDtypeStruct((n, LANES), I32)),
        grid=(n // tm,),
        in_specs=[
            pl.BlockSpec((tm, d), lambda i: (i, 0)),
            pl.BlockSpec((1, d), lambda i: (0, 0)),
            pl.BlockSpec((None, None, 1, d), lambda i: (grp(i), 3, 0, 0)),
            pl.BlockSpec((None, None, 1, d), lambda i: (grp(i), 4, 0, 0)),
            pl.BlockSpec((d, LANES), lambda i: (0, 0)),
            pl.BlockSpec((1, LANES), lambda i: (0, 0)),
        ],
        out_specs=(pl.BlockSpec((tm, HALF), lambda i: (i, 0)), lane_out(), lane_out()),
        compiler_params=_cparams(("arbitrary",)),
        name="moe_pre",
    )(x, norm_g.reshape(1, d), mod, mod, w_router, b_router)


def _expert_changed(be_ref, j):
    return (j == 0) | (be_ref[j] != be_ref[jnp.maximum(j - 1, 0)])


def _moe_up_kernel(be_ref, bs_ref, nu_ref, x_ref, wg_ref, wu_ref, h_ref, wg_s, wu_s):
    j = pl.program_id(1)

    @pl.when((j < nu_ref[0]) & _expert_changed(be_ref, j))
    def _():
        wg_s[...] = wg_ref[...].astype(BF16)
        wu_s[...] = wu_ref[...].astype(BF16)

    @pl.when(j < nu_ref[0])
    def _():
        w = x_ref[...]
        lo = lax.bitcast_convert_type(w << 16, F32).astype(BF16)
        hi = lax.bitcast_convert_type(w & jnp.uint32(0xFFFF0000), F32).astype(BF16)

        def mm(w_s):
            return (jnp.dot(lo, w_s[0:HALF, :], preferred_element_type=F32)
                    + jnp.dot(hi, w_s[HALF:, :], preferred_element_type=F32))

        gt, up = mm(wg_s), mm(wu_s)
        h_ref[...] = (gt * jax.nn.sigmoid(gt) * up).astype(BF16)


def _moe_down_kernel(be_ref, bs_ref, nu_ref, h_ref, wd_ref, y_ref, wd_s):
    j = pl.program_id(1)

    @pl.when((j < nu_ref[0]) & _expert_changed(be_ref, j))
    def _():
        wd_s[...] = wd_ref[...].astype(BF16)

    @pl.when(j < nu_ref[0])
    def _():
        y_ref[...] = jnp.dot(h_ref[...], wd_s[...], preferred_element_type=F32)


def moe_experts(xb, blk_e, blk_src, n_used, w_gate, w_up, w_down, layer, tb=MOE_TB, tc=512, tc2=D_MODEL):
    rows = xb.shape[0]
    n_blocks = rows // tb
    d, de = w_gate.shape[-2:]
    h = pl.pallas_call(
        _moe_up_kernel,
        out_shape=jax.ShapeDtypeStruct((rows, de), BF16),
        grid_spec=pltpu.PrefetchScalarGridSpec(
            num_scalar_prefetch=3,
            grid=(de // tc, n_blocks),
            in_specs=[
                pl.BlockSpec((tb, HALF), lambda c, j, be, bs, nu: (bs[j], 0)),
                pl.BlockSpec((None, None, d, tc), lambda c, j, be, bs, nu: (layer, be[j], 0, c)),
                pl.BlockSpec((None, None, d, tc), lambda c, j, be, bs, nu: (layer, be[j], 0, c)),
            ],
            out_specs=pl.BlockSpec((tb, tc), lambda c, j, be, bs, nu: (bs[j], c)),
            scratch_shapes=[pltpu.VMEM((d, tc), BF16), pltpu.VMEM((d, tc), BF16)],
        ),
        compiler_params=_cparams(("arbitrary", "arbitrary")),
        name="moe_up",
    )(blk_e, blk_src, n_used, xb, w_gate, w_up)
    return pl.pallas_call(
        _moe_down_kernel,
        out_shape=jax.ShapeDtypeStruct((rows, d), F32),
        grid_spec=pltpu.PrefetchScalarGridSpec(
            num_scalar_prefetch=3,
            grid=(d // tc2, n_blocks),
            in_specs=[
                pl.BlockSpec((tb, de), lambda c, j, be, bs, nu: (bs[j], 0)),
                pl.BlockSpec((None, None, de, tc2), lambda c, j, be, bs, nu: (layer, be[j], 0, c)),
            ],
            out_specs=pl.BlockSpec((tb, tc2), lambda c, j, be, bs, nu: (bs[j], c)),
            scratch_shapes=[pltpu.VMEM((de, tc2), BF16)],
        ),
        compiler_params=_cparams(("arbitrary", "arbitrary")),
        name="moe_down",
    )(blk_e, blk_src, n_used, h, w_down)


def _moe_combine_kernel(x_ref, y0_ref, y1_ref, gt_ref, gate_ref, o_ref):
    y = gt_ref[:, 0:1] * y0_ref[...] + gt_ref[:, 1:2] * y1_ref[...]
    o_ref[...] = x_ref[...] + gate_ref[...] * y


def moe_combine(x, y0, y1, gates, mod, nc, tl, tm=512):
    n, d = x.shape
    grp = lambda i: _group_of_tile(i * tm, nc, tl)
    blk = lambda: pl.BlockSpec((tm, d), lambda i: (i, 0))
    return pl.pallas_call(
        _moe_combine_kernel,
        out_shape=jax.ShapeDtypeStruct((n, d), F32),
        grid=(n // tm,),
        in_specs=[blk(), blk(), blk(), pl.BlockSpec((tm, LANES), lambda i: (i, 0)),
                  pl.BlockSpec((None, None, 1, d), lambda i: (grp(i), 5, 0, 0))],
        out_specs=blk(),
        compiler_params=_cparams(("arbitrary",)),
        name="moe_combine",
    )(x, y0, y1, gates, mod)


def moe_layout(eid, tb=MOE_TB):
    n = eid.shape[0]
    s_len = n * TOP_K
    flat_e = eid.reshape(s_len)
    onehot = (flat_e[:, None] == jnp.arange(N_EXPERTS, dtype=I32)[None, :]).astype(I32)
    rank = jnp.sum((jnp.cumsum(onehot, axis=0) - onehot) * onehot, axis=1)
    counts = jnp.sum(onehot, axis=0)
    start = jnp.cumsum(counts) - counts
    nblk_e = (counts + tb - 1) // tb
    blk_end = jnp.cumsum(nblk_e)
    blk_start = blk_end - nblk_e
    dest = jnp.sum(onehot * blk_start[None, :], axis=1) * tb + rank
    n_blocks = s_len // tb + N_EXPERTS
    n_used = blk_end[-1]
    jc = jnp.minimum(jnp.arange(n_blocks, dtype=I32), n_used - 1)
    blk_e = jnp.minimum(jnp.sum((blk_end[None, :] <= jc[:, None]).astype(I32), axis=1), N_EXPERTS - 1).astype(I32)
    order = jnp.argsort(flat_e).astype(I32)
    p = jnp.arange(n_blocks * tb, dtype=I32)
    e_p = blk_e[p // tb]
    off = p - blk_start[e_p] * tb
    valid = (p // tb < n_used) & (off < counts[e_p])
    slot_tok = jnp.where(valid, order[jnp.clip(start[e_p] + off, 0, s_len - 1)] // TOP_K, p % n)
    return dest.reshape(n, TOP_K), slot_tok, blk_e, jc, n_used.reshape(1).astype(I32)


def _final_norm_kernel(x_ref, g_ref, o_ref):
    x = x_ref[...]
    o_ref[...] = x * lax.rsqrt(jnp.mean(x * x, axis=-1, keepdims=True) + NORM_EPS) * g_ref[...]


def final_norm(x, g, tm=512):
    n, d = x.shape
    return pl.pallas_call(
        _final_norm_kernel,
        out_shape=jax.ShapeDtypeStruct((n, d), F32),
        grid=(n // tm,),
        in_specs=[pl.BlockSpec((tm, d), lambda i: (i, 0)), pl.BlockSpec((1, d), lambda i: (0, 0))],
        out_specs=pl.BlockSpec((tm, d), lambda i: (i, 0)),
        compiler_params=_cparams(("arbitrary",)),
        name="final_norm",
    )(x, g.reshape(1, d))


def _pad_lanes(a):
    return jnp.pad(a, ((0, 0), (0, LANES - a.shape[1])))


def kernel(x_prompt, x_sample, cache_k, cache_v, state_mlstm_C, state_mlstm_n, state_mlstm_m, state_rglru_h, c, c_ctx, ada_w, ada_b, norm1_g, w_in, attn_sink, mlstm_gate_b, mlstm_norm_g, rg_conv_w, rg_conv_b, rg_wr, rg_br, rg_wi, rg_bi, rg_lam, w_out, norm2_g, router_wg, router_bg, router_we, router_be, exp_w_gate, exp_w_up, exp_w_down, final_norm_g):
    bc, tc, d = x_prompt.shape
    bl, tl, _ = x_sample.shape
    depth = w_in.shape[0]
    past = cache_k.shape[2]
    nc, nl = bc * tc, bl * tl
    tm = math.gcd(1024, math.gcd(tl, nc))

    x = jnp.concatenate([x_prompt.reshape(nc, d), x_sample.reshape(nl, d)], axis=0)
    cvec = jnp.concatenate([c_ctx[None, :], c, jnp.zeros((SUBLANES - 1 - bl, d), F32)], axis=0)
    mods = adaln_all(cvec, ada_w, ada_b).reshape(depth, SUBLANES, 6, 1, d)
    rope_c, rope_s = rope_tables(tl)
    gsplit = Z_MAIN - 2 * R_WIDTH
    w_main = jnp.concatenate([w_in[:, :, :gsplit], w_in[:, :, gsplit + M_GATES:]], axis=2).astype(BF16)
    w_gcol = jnp.pad(w_in[:, :, gsplit:gsplit + M_GATES], ((0, 0), (0, 0), (0, LANES - M_GATES))).astype(BF16)
    w_out_b = w_out.astype(BF16)
    ck = cache_k.reshape(bl, depth, past, KV_WIDTH)
    cv = cache_v.reshape(bl, depth, past, KV_WIDTH)
    cn0 = jnp.concatenate([state_mlstm_C, jnp.broadcast_to(state_mlstm_n[..., None], state_mlstm_C.shape)], axis=-1)

    new_k, new_v, new_c, new_n, new_m, new_h = [], [], [], [], [], []
    for l in range(depth):
        mod = mods[l]
        z, zg = in_proj(x, norm1_g[l], mod, w_main[l], w_gcol[l], nc, tl, tm=tm)

        a_ctx = attention_context(z, attn_sink[l], bc, tc)
        a_lat = attention_latent(z, ck, cv, l, attn_sink[l], rope_c, rope_s, nc, bl, tl)
        a = jnp.concatenate([a_ctx, a_lat], axis=0)

        rows, tiles, kt = mlstm_gates(z, zg, _pad_lanes(mlstm_gate_b[l].reshape(1, M_GATES)), tile=tm)
        res = mlstm_scan(z, rows, tiles, kt, cn0, state_mlstm_m, l, bc, tc, bl, tl)
        mh = res[0:2]
        for dr in range(2):
            cnf, mf = res[2 + 2 * dr:4 + 2 * dr]
            new_c.append(cnf[:bc, :, :, :M_DK])
            new_n.append(cnf[:bc, :, :, M_DK])
            new_m.append(mf[:bc, :M_HEADS, 0])
        rh = []
        for dr in range(2):
            rd, hf = rglru_direction(z, rg_conv_w, rg_conv_b, rg_wr, rg_br, rg_wi, rg_bi, rg_lam, state_rglru_h,
                                     l, dr, bc, tc, bl, tl)
            rh.append(rd)
            new_h.append(hf[:bc, 0])
        new_k.append(z[:nc, _C_K:_C_K + KV_WIDTH].reshape(bc, tc, KV_HEADS, HEAD_DIM))
        new_v.append(z[:nc, _C_V:_C_V + KV_WIDTH].reshape(bc, tc, KV_HEADS, HEAD_DIM))

        x = out_proj(x, a, mh[0], mh[1], rh[0], rh[1], z, mlstm_norm_g[l], mod, w_out_b[l], nc, tl, tm=tm)

        w_router = _pad_lanes(jnp.concatenate([router_wg[l], router_we[l]], axis=1))
        b_router = _pad_lanes(jnp.concatenate([router_bg[l], router_be[l]])[None, :])
        xp, route_g, route_e = moe_pre(x, norm2_g[l], mod, w_router, b_router, nc, tl)
        dest, slot_tok, blk_e, blk_src, n_used = moe_layout(route_e[:, :TOP_K])
        yb = moe_experts(xp[slot_tok], blk_e, blk_src, n_used, exp_w_gate, exp_w_up, exp_w_down, l)
        x = moe_combine(x, yb[dest[:, 0]], yb[dest[:, 1]], route_g, mod, nc, tl)

    y = final_norm(x, final_norm_g)
    stack2 = lambda parts: jnp.stack([jnp.stack(parts[2 * l:2 * l + 2], axis=1) for l in range(depth)], axis=1)
    return (y[:nc].reshape(bc, tc, d), y[nc:].reshape(bl, tl, d),
            jnp.stack(new_k, axis=1), jnp.stack(new_v, axis=1),
            stack2(new_c), stack2(new_n), stack2(new_m), stack2(new_h))
```

```python
import functools
import math

import jax
import jax.numpy as jnp
from jax import lax
from jax.experimental import pallas as pl
from jax.experimental.pallas import tpu as pltpu

F32 = jnp.float32
BF16 = jnp.bfloat16
U32 = jnp.uint32
I32 = jnp.int32

D_MODEL = 2048
HEAD_DIM = 128
A_WIDTH = D_MODEL // 2
N_HEADS = A_WIDTH // HEAD_DIM
KV_HEADS = 2
GQA_GROUP = N_HEADS // KV_HEADS
KV_WIDTH = KV_HEADS * HEAD_DIM
WINDOW = 128
Q_BLOCK = 128
GRID_W = 64
ROPE_BASE = 10000.0
ROPE_AXIS = HEAD_DIM // 2
ATTN_SCALE = HEAD_DIM ** -0.5
NEG_INF = -1e30
M_WIDTH = D_MODEL // 4
M_HEADS = 4
M_DK = M_WIDTH // M_HEADS
M_GATES = 2 * 2 * M_HEADS
R_WIDTH = D_MODEL // 4
R_BLOCKS = 4
R_BW = R_WIDTH // R_BLOCKS
CONV_W = 4
RG_C = 8.0
N_GROUPS = 4
EXPERTS_PER_GROUP = 8
N_EXPERTS = N_GROUPS * EXPERTS_PER_GROUP
TOP_K = 2
D_EXPERT = D_MODEL // 2
NORM_EPS = 1e-6

LANES = 128
SUBLANES = 8
Z_MAIN = A_WIDTH + 2 * KV_WIDTH + 4 * M_WIDTH + 2 * R_WIDTH
_C_Q, _C_K, _C_V = 0, A_WIDTH, A_WIDTH + KV_WIDTH
_C_QM = A_WIDTH + 2 * KV_WIDTH
_C_KM, _C_VM, _C_OM = _C_QM + M_WIDTH, _C_QM + 2 * M_WIDTH, _C_QM + 3 * M_WIDTH
_C_XR, _C_YR = _C_QM + 4 * M_WIDTH, _C_QM + 4 * M_WIDTH + R_WIDTH

M_CHUNK = LANES
R_CHUNK = 256
MOE_TB = 512
VMEM_LIMIT = 56 * 1024 * 1024
HALF = D_MODEL // 2


def _cparams(sem):
    return pltpu.CompilerParams(dimension_semantics=sem, vmem_limit_bytes=VMEM_LIMIT)


def _group_of_tile(row0, nc, tl):
    return jnp.where(row0 < nc, 0, 1 + (jnp.maximum(row0 - nc, 0)) // tl)


def _adaln_kernel(c_ref, w_ref, b_ref, o_ref):
    c = c_ref[...]
    s = (c * jax.nn.sigmoid(c)).astype(BF16)
    o_ref[...] = jnp.dot(s, w_ref[...].astype(BF16), preferred_element_type=F32) + b_ref[...]


def adaln_all(cvec, ada_w, ada_b, tn=1024):
    depth, d, d6 = ada_w.shape
    return pl.pallas_call(
        _adaln_kernel,
        out_shape=jax.ShapeDtypeStruct((depth, SUBLANES, d6), F32),
        grid=(depth, d6 // tn),
        in_specs=[
            pl.BlockSpec((SUBLANES, d), lambda l, j: (0, 0)),
            pl.BlockSpec((None, d, tn), lambda l, j: (l, 0, j)),
            pl.BlockSpec((None, 1, tn), lambda l, j: (l, 0, j)),
        ],
        out_specs=pl.BlockSpec((None, SUBLANES, tn), lambda l, j: (l, 0, j)),
        compiler_params=_cparams(("arbitrary", "arbitrary")),
        name="adaln",
    )(cvec, ada_w, ada_b.reshape(depth, 1, d6))


def _in_kernel(x_ref, g_ref, sh_ref, sc_ref, w_ref, wg_ref, z_ref, zg_ref, xn_ref):
    @pl.when(pl.program_id(1) == 0)
    def _():
        x = x_ref[...]
        y = x * lax.rsqrt(jnp.mean(x * x, axis=-1, keepdims=True) + NORM_EPS) * g_ref[...]
        xn = (y * (1.0 + sc_ref[...]) + sh_ref[...]).astype(BF16)
        xn_ref[...] = xn
        zg_ref[...] = jnp.dot(xn, wg_ref[...], preferred_element_type=F32)

    z_ref[...] = jnp.dot(xn_ref[...], w_ref[...], preferred_element_type=F32)


def in_proj(x, norm_g, mod, w_main, w_gate, nc, tl, tm=1024, tn=768):
    n, d = x.shape
    zw = w_main.shape[1]
    grp = lambda i: _group_of_tile(i * tm, nc, tl)
    return pl.pallas_call(
        _in_kernel,
        out_shape=(jax.ShapeDtypeStruct((n, zw), F32), jax.ShapeDtypeStruct((n, LANES), F32)),
        grid=(n // tm, zw // tn),
        in_specs=[
            pl.BlockSpec((tm, d), lambda i, j: (i, 0)),
            pl.BlockSpec((1, d), lambda i, j: (0, 0)),
            pl.BlockSpec((None, None, 1, d), lambda i, j: (grp(i), 0, 0, 0)),
            pl.BlockSpec((None, None, 1, d), lambda i, j: (grp(i), 1, 0, 0)),
            pl.BlockSpec((d, tn), lambda i, j: (0, j)),
            pl.BlockSpec((d, LANES), lambda i, j: (0, 0)),
        ],
        out_specs=(pl.BlockSpec((tm, tn), lambda i, j: (i, j)), pl.BlockSpec((tm, LANES), lambda i, j: (i, 0))),
        scratch_shapes=[pltpu.VMEM((tm, d), BF16)],
        compiler_params=_cparams(("arbitrary", "arbitrary")),
        name="in_proj",
    )(x, norm_g.reshape(1, d), mod, mod, w_main, w_gate)


def _rope(x, c, s):
    lane = lax.broadcasted_iota(I32, x.shape, 1)
    half = ROPE_AXIS // 2
    partner = jnp.where((lane & (ROPE_AXIS - 1)) < half, pltpu.roll(x, HEAD_DIM - half, 1), pltpu.roll(x, half, 1))
    return x * c + partner * s


def _softmax_pv(parts, sink_col):
    m = sink_col
    for s, _ in parts:
        m = jnp.maximum(m, jnp.max(s, axis=-1, keepdims=True))
    den = jnp.exp(sink_col - m)
    acc = None
    for s, v in parts:
        p = jnp.exp(s - m)
        den = den + jnp.sum(p, axis=-1, keepdims=True)
        pv = jnp.dot(p.astype(BF16), v, preferred_element_type=F32)
        acc = pv if acc is None else acc + pv
    return acc / den


def _qk(q, k):
    return lax.dot_general(q, k, (((1,), (1,)), ((), ())), preferred_element_type=F32) * ATTN_SCALE


def _sink_column(sink_ref, n, rows):
    ridx = lax.broadcasted_iota(I32, (GQA_GROUP * rows, 1), 0)
    col = jnp.full((GQA_GROUP * rows, 1), sink_ref[n * GQA_GROUP], F32)
    for g in range(1, GQA_GROUP):
        col = jnp.where(ridx >= g * rows, sink_ref[n * GQA_GROUP + g], col)
    return col


def _attn_ctx_kernel(sink_ref, q_ref, k_ref, v_ref, o_ref):
    t = q_ref.shape[0]
    for n in range(KV_HEADS):
        k = k_ref[:, n * HEAD_DIM:(n + 1) * HEAD_DIM].astype(BF16)
        v = v_ref[:, n * HEAD_DIM:(n + 1) * HEAD_DIM].astype(BF16)
        q = jnp.concatenate(
            [q_ref[:, (n * GQA_GROUP + g) * HEAD_DIM:(n * GQA_GROUP + g + 1) * HEAD_DIM] for g in range(GQA_GROUP)],
            axis=0).astype(BF16)
        out = _softmax_pv([(_qk(q, k), v)], _sink_column(sink_ref, n, t))
        for g in range(GQA_GROUP):
            h = n * GQA_GROUP + g
            o_ref[:, h * HEAD_DIM:(h + 1) * HEAD_DIM] = out[g * t:(g + 1) * t].astype(o_ref.dtype)


def attention_context(z, sink, b, t):
    return pl.pallas_call(
        _attn_ctx_kernel,
        out_shape=jax.ShapeDtypeStruct((b * t, A_WIDTH), BF16),
        grid=(b,),
        in_specs=[
            pl.BlockSpec(memory_space=pltpu.SMEM),
            pl.BlockSpec((t, A_WIDTH), lambda i: (i, _C_Q // A_WIDTH)),
            pl.BlockSpec((t, KV_WIDTH), lambda i: (i, _C_K // KV_WIDTH)),
            pl.BlockSpec((t, KV_WIDTH), lambda i: (i, _C_V // KV_WIDTH)),
        ],
        out_specs=pl.BlockSpec((t, A_WIDTH), lambda i: (i, 0)),
        compiler_params=_cparams(("arbitrary",)),
        name="attn_ctx",
    )(sink, z, z, z)


def _attn_lat_kernel(sink_ref, q_ref, k_ref, v_ref, kc_ref, vc_ref, cq_ref, sq_ref, ck_ref, sk_ref, o_ref,
                     kr_ref, vp_ref, *, t):
    i = pl.program_id(1)
    rope_rows = 512

    @pl.when(i == 0)
    def _():
        zpad = jnp.zeros((WINDOW, KV_WIDTH), BF16)
        kr_ref[0:WINDOW, :] = zpad
        kr_ref[WINDOW + t:2 * WINDOW + t, :] = zpad
        vp_ref[0:WINDOW, :] = zpad
        vp_ref[WINDOW + t:2 * WINDOW + t, :] = zpad

        def body(c, carry):
            r0 = pl.multiple_of(c * rope_rows, rope_rows)
            cs, sn = ck_ref[pl.ds(r0, rope_rows), :], sk_ref[pl.ds(r0, rope_rows), :]
            for n in range(KV_HEADS):
                kk = k_ref[pl.ds(r0, rope_rows), n * HEAD_DIM:(n + 1) * HEAD_DIM]
                kr_ref[pl.ds(WINDOW + r0, rope_rows), n * HEAD_DIM:(n + 1) * HEAD_DIM] = _rope(kk, cs, sn).astype(BF16)
            vp_ref[pl.ds(WINDOW + r0, rope_rows), :] = v_ref[pl.ds(r0, rope_rows), :].astype(BF16)
            return carry

        lax.fori_loop(0, t // rope_rows, body, 0)

    span = Q_BLOCK + 2 * WINDOW
    rows = GQA_GROUP * Q_BLOCK
    r = lax.broadcasted_iota(I32, (rows, span), 0) & (Q_BLOCK - 1)
    c = lax.broadcasted_iota(I32, (rows, span), 1)
    kpos = (i - 1) * Q_BLOCK + c
    mask = (c >= r) & (c <= r + 2 * WINDOW) & (kpos >= 0) & (kpos < t)
    w0 = pl.multiple_of(i * Q_BLOCK, Q_BLOCK)
    cq, sq = cq_ref[...], sq_ref[...]
    for n in range(KV_HEADS):
        hs = slice(n * HEAD_DIM, (n + 1) * HEAD_DIM)
        q = jnp.concatenate(
            [_rope(q_ref[:, (n * GQA_GROUP + g) * HEAD_DIM:(n * GQA_GROUP + g + 1) * HEAD_DIM], cq, sq)
             for g in range(GQA_GROUP)], axis=0).astype(BF16)
        s_win = jnp.where(mask, _qk(q, kr_ref[pl.ds(w0, span), hs]), NEG_INF)
        s_ctx = _qk(q, kc_ref[:, hs].astype(BF16))
        out = _softmax_pv([(s_win, vp_ref[pl.ds(w0, span), hs]), (s_ctx, vc_ref[:, hs].astype(BF16))],
                          _sink_column(sink_ref, n, Q_BLOCK))
        for g in range(GQA_GROUP):
            h = n * GQA_GROUP + g
            o_ref[:, h * HEAD_DIM:(h + 1) * HEAD_DIM] = out[g * Q_BLOCK:(g + 1) * Q_BLOCK].astype(o_ref.dtype)


def attention_latent(z, cache_k, cache_v, layer, sink, rope_c, rope_s, row0, b, t):
    nqb = t // Q_BLOCK
    past = cache_k.shape[2]
    qb0 = row0 // Q_BLOCK
    tb0 = row0 // t
    kern = functools.partial(_attn_lat_kernel, t=t)
    return pl.pallas_call(
        kern,
        out_shape=jax.ShapeDtypeStruct((b * t, A_WIDTH), BF16),
        grid=(b, nqb),
        in_specs=[
            pl.BlockSpec(memory_space=pltpu.SMEM),
            pl.BlockSpec((Q_BLOCK, A_WIDTH), lambda bi, i: (qb0 + bi * nqb + i, _C_Q // A_WIDTH)),
            pl.BlockSpec((t, KV_WIDTH), lambda bi, i: (tb0 + bi, _C_K // KV_WIDTH)),
            pl.BlockSpec((t, KV_WIDTH), lambda bi, i: (tb0 + bi, _C_V // KV_WIDTH)),
            pl.BlockSpec((None, None, past, KV_WIDTH), lambda bi, i: (bi, layer, 0, 0)),
            pl.BlockSpec((None, None, past, KV_WIDTH), lambda bi, i: (bi, layer, 0, 0)),
            pl.BlockSpec((Q_BLOCK, HEAD_DIM), lambda bi, i: (i, 0)),
            pl.BlockSpec((Q_BLOCK, HEAD_DIM), lambda bi, i: (i, 0)),
            pl.BlockSpec((t, HEAD_DIM), lambda bi, i: (0, 0)),
            pl.BlockSpec((t, HEAD_DIM), lambda bi, i: (0, 0)),
        ],
        out_specs=pl.BlockSpec((Q_BLOCK, A_WIDTH), lambda bi, i: (bi * nqb + i, 0)),
        scratch_shapes=[pltpu.VMEM((t + 2 * WINDOW, KV_WIDTH), BF16), pltpu.VMEM((t + 2 * WINDOW, KV_WIDTH), BF16)],
        compiler_params=_cparams(("arbitrary", "arbitrary")),
        name="attn_lat",
    )(sink, z, z, z, cache_k, cache_v, rope_c, rope_s, rope_c, rope_s)


def rope_tables(t):
    pos = jnp.arange(t)
    inv = ROPE_BASE ** (-jnp.arange(0, ROPE_AXIS, 2, dtype=F32) / ROPE_AXIS)

    def cs(p):
        ang = p.astype(F32)[:, None] * inv[None, :]
        return jnp.cos(ang), jnp.sin(ang)

    cr, sr = cs(pos // GRID_W)
    cc, sc = cs(pos % GRID_W)
    return (jnp.concatenate([cr, cr, cc, cc], axis=-1), jnp.concatenate([-sr, sr, -sc, sc], axis=-1))


class _Seqs:
    def __init__(self, n_ctx, t_ctx, n_lat, t_lat, chunk, reverse):
        self.n_ctx, self.t_ctx, self.n_lat, self.t_lat = n_ctx, t_ctx, n_lat, t_lat
        self.chunk, self.reverse = chunk, reverse
        self.cpc, self.cpl = t_ctx // chunk, t_lat // chunk
        self.ctx_chunks = n_ctx * self.cpc
        self.n_chunks = self.ctx_chunks + n_lat * self.cpl

    def chunk_of_step(self, s):
        return (self.n_chunks - 1 - s) if self.reverse else s

    def info(self, g):
        is_ctx = g < self.ctx_chunks
        gl = jnp.maximum(g - self.ctx_chunks, 0)
        gc = jnp.minimum(g, self.ctx_chunks - 1)
        pos = jnp.where(is_ctx, gc % self.cpc, gl % self.cpl)
        per = jnp.where(is_ctx, self.cpc, self.cpl)
        lat = gl // self.cpl
        seq = jnp.where(is_ctx, gc // self.cpc, self.n_ctx + lat)
        head, tail = pos == 0, pos == per - 1
        return is_ctx, seq, lat, (tail if self.reverse else head), (head if self.reverse else tail)


def _lane_scan(x, op, reverse):
    n = x.shape[1]
    lane = lax.broadcasted_iota(I32, x.shape, 1)
    s = 1
    while s < n:
        if reverse:
            x = jnp.where(lane < n - s, op(x, pltpu.roll(x, n - s, 1)), x)
        else:
            x = jnp.where(lane >= s, op(x, pltpu.roll(x, s, 1)), x)
        s *= 2
    return x


_GROWS = 2 * SUBLANES


def _mlstm_gate_kernel(g_ref, gb_ref, k_ref, rows_ref, tiles_ref, kt_ref):
    L = M_CHUNK
    nck = g_ref.shape[0] // L
    g = g_ref[...] + gb_ref[...]
    s = jnp.concatenate([g[c * L:(c + 1) * L].T[0:_GROWS] for c in range(nck)], axis=0)
    row = lax.broadcasted_iota(I32, s.shape, 0)
    is_rev = (row & SUBLANES) != 0
    lf = jax.nn.log_sigmoid(s)
    f = jnp.where(is_rev, _lane_scan(lf, jnp.add, True), _lane_scan(lf, jnp.add, False))
    f = pltpu.roll(f, s.shape[0] - M_HEADS, 0)
    r = s - f
    cm = jnp.where(is_rev, _lane_scan(r, jnp.maximum, True), _lane_scan(r, jnp.maximum, False))
    rows_ref[...] = r
    pad = jnp.zeros((LANES - 2 * _GROWS, L), F32)
    scale = M_DK ** -0.5
    for c in range(nck):
        sl = slice(c * _GROWS, (c + 1) * _GROWS)
        cols = jnp.concatenate([cm[sl], f[sl], pad], axis=0).T
        for d in range(2):
            for h in range(M_HEADS):
                gi = d * SUBLANES + h
                tiles_ref[c, d, h] = jnp.broadcast_to(cols[:, gi:gi + 1], (L, LANES))
                tiles_ref[c, d, M_HEADS + h] = jnp.broadcast_to(cols[:, _GROWS + gi:_GROWS + gi + 1], (L, LANES))
        for h in range(M_HEADS):
            kt_ref[c, h] = (k_ref[c * L:(c + 1) * L, h * M_DK:(h + 1) * M_DK] * scale).T


def mlstm_gates(z, zg, gate_b, tile=512):
    n = zg.shape[0]
    L = M_CHUNK
    nck = tile // L
    return pl.pallas_call(
        _mlstm_gate_kernel,
        out_shape=(jax.ShapeDtypeStruct((n // L * _GROWS, L), F32),
                   jax.ShapeDtypeStruct((n // L, 2, 2 * M_HEADS, L, LANES), F32),
                   jax.ShapeDtypeStruct((n // L, M_HEADS, M_DK, L), F32)),
        grid=(n // tile,),
        in_specs=[pl.BlockSpec((tile, LANES), lambda i: (i, 0)), pl.BlockSpec((1, LANES), lambda i: (0, 0)),
                  pl.BlockSpec((tile, M_WIDTH), lambda i: (i, _C_KM // M_WIDTH))],
        out_specs=(pl.BlockSpec((nck * _GROWS, L), lambda i: (i, 0)),
                   pl.BlockSpec((nck, 2, 2 * M_HEADS, L, LANES), lambda i: (i, 0, 0, 0, 0)),
                   pl.BlockSpec((nck, M_HEADS, M_DK, L), lambda i: (i, 0, 0, 0))),
        compiler_params=_cparams(("arbitrary",)),
        name="mlstm_gates",
    )(zg, gate_b, z)


def _mlstm_kernel(m0_ref, qf_ref, vf_ref, ktf_ref, rf_ref, tf_ref, qb_ref, vb_ref, ktb_ref, rb_ref, tb_ref,
                  cn0f_ref, cn0b_ref, hf_ref, hb_ref, ocnf_ref, omf_ref, ocnb_ref, omb_ref,
                  cn_s, m_s, *, seqs, layer, depth):
    L = M_CHUNK
    step = pl.program_id(0)
    ti = lax.broadcasted_iota(I32, (L, L), 0)
    si = lax.broadcasted_iota(I32, (L, L), 1)
    ones = jnp.ones((L, M_DK), BF16)
    streams = (
        (0, seqs[0], qf_ref, vf_ref, ktf_ref, rf_ref, tf_ref, cn0f_ref, hf_ref, ocnf_ref, omf_ref),
        (1, seqs[1], qb_ref, vb_ref, ktb_ref, rb_ref, tb_ref, cn0b_ref, hb_ref, ocnb_ref, omb_ref),
    )
    for d, sq, q_ref, v_ref, kt_ref, rows_ref, tiles_ref, cn0_ref, h_ref, ocn_ref, om_ref in streams:
        rev = d == 1
        is_ctx, _, lat, first, last = sq.info(sq.chunk_of_step(step))

        @pl.when(first)
        def _(d=d, is_ctx=is_ctx, lat=lat, cn0_ref=cn0_ref):
            keep = jnp.where(is_ctx, 0.0, 1.0).astype(F32)
            cn_s[d] = cn0_ref[...] * keep
            m_s[d] = jnp.zeros((SUBLANES, LANES), F32)
            for h in range(M_HEADS):
                m0 = m0_ref[((lat * depth + layer) * 2 + d) * M_HEADS + h]
                m_s[d, h:h + 1, :] = jnp.full((1, LANES), m0, F32) * keep

        end = 0 if rev else L - 1
        causal = (si >= ti) if rev else (si <= ti)
        for h in range(M_HEADS):
            hs = slice(h * M_DK, (h + 1) * M_DK)
            gi = d * SUBLANES + h
            q = q_ref[:, hs].astype(BF16)
            vaug = jnp.concatenate([v_ref[:, hs].astype(BF16), ones], axis=1)
            kt = kt_ref[h]
            r_row = rows_ref[gi:gi + 1, :]
            f_b = tiles_ref[M_HEADS + h]
            m_old = m_s[d, h:h + 1, :]
            m_b = jnp.maximum(m_old, tiles_ref[h])
            dmat = jnp.where(causal, jnp.exp(jnp.where(causal, r_row - m_b, 0.0)), 0.0)
            s = jnp.dot(q, kt.astype(BF16), preferred_element_type=F32) * dmat
            wpq = (jnp.exp(m_old - m_b) * q.astype(F32)).astype(BF16)
            cn_old = cn_s[d, h]
            out = jnp.dot(jnp.concatenate([s.astype(BF16), wpq], axis=1),
                          jnp.concatenate([vaug, cn_old.astype(BF16)], axis=0), preferred_element_type=F32)
            h_ref[:, hs] = out[:, :M_DK] / jnp.maximum(jnp.abs(out[:, M_DK:]), jnp.exp(-(f_b + m_b)))
            m_end = m_b[end:end + 1, :]
            dec = jnp.exp(m_old - m_end)
            kwt = (kt * jnp.exp(r_row - m_end)).astype(BF16)
            cn_s[d, h] = jnp.concatenate([dec, dec], axis=1) * cn_old + jnp.dot(kwt, vaug, preferred_element_type=F32)
            m_s[d, h:h + 1, :] = f_b[end:end + 1, :] + m_end

        @pl.when(last)
        def _(d=d, ocn_ref=ocn_ref, om_ref=om_ref):
            ocn_ref[...] = cn_s[d]
            om_ref[...] = m_s[d]


def mlstm_scan(z, rows, tiles, kt, cn0, st_m, layer, n_ctx, t_ctx, n_lat, t_lat):
    L = M_CHUNK
    seqs = (_Seqs(n_ctx, t_ctx, n_lat, t_lat, L, False), _Seqs(n_ctx, t_ctx, n_lat, t_lat, L, True))
    n_seq = n_ctx + n_lat
    n_chunks = seqs[0].n_chunks
    kern = functools.partial(_mlstm_kernel, seqs=seqs, layer=layer, depth=st_m.shape[1])
    in_specs = [pl.BlockSpec(memory_space=pltpu.SMEM)]
    for d, sq in enumerate(seqs):
        cm = lambda s, sq=sq: sq.chunk_of_step(s)
        in_specs += [
            pl.BlockSpec((L, M_WIDTH), lambda s, cm=cm: (cm(s), _C_QM // M_WIDTH)),
            pl.BlockSpec((L, M_WIDTH), lambda s, cm=cm: (cm(s), _C_VM // M_WIDTH)),
            pl.BlockSpec((None, M_HEADS, M_DK, L), lambda s, cm=cm: (cm(s), 0, 0, 0)),
            pl.BlockSpec((_GROWS, L), lambda s, cm=cm: (cm(s), 0)),
            pl.BlockSpec((None, None, 2 * M_HEADS, L, LANES), lambda s, cm=cm, d=d: (cm(s), d, 0, 0, 0)),
        ]
    for d, sq in enumerate(seqs):
        lat_of = lambda s, sq=sq: sq.info(sq.chunk_of_step(s))[2]
        in_specs.append(pl.BlockSpec((None, None, None, M_HEADS, M_DK, 2 * M_DK),
                                     lambda s, f=lat_of, d=d: (f(s), layer, d, 0, 0, 0)))
    out_shape = [jax.ShapeDtypeStruct((n_chunks * L, M_WIDTH), F32)] * 2
    out_specs = [pl.BlockSpec((L, M_WIDTH), lambda s, sq=sq: (sq.chunk_of_step(s), 0)) for sq in seqs]
    for sq in seqs:
        seq_of = lambda s, sq=sq: sq.info(sq.chunk_of_step(s))[1]
        out_shape += [jax.ShapeDtypeStruct((n_seq, M_HEADS, M_DK, 2 * M_DK), F32),
                      jax.ShapeDtypeStruct((n_seq, SUBLANES, LANES), F32)]
        out_specs += [pl.BlockSpec((None, M_HEADS, M_DK, 2 * M_DK), lambda s, f=seq_of: (f(s), 0, 0, 0)),
                      pl.BlockSpec((None, SUBLANES, LANES), lambda s, f=seq_of: (f(s), 0, 0))]
    args = [st_m.reshape(-1), z, z, kt, rows, tiles, z, z, kt, rows, tiles, cn0, cn0]
    return pl.pallas_call(
        kern,
        out_shape=tuple(out_shape),
        grid=(n_chunks,),
        in_specs=in_specs,
        out_specs=tuple(out_specs),
        scratch_shapes=[pltpu.VMEM((2, M_HEADS, M_DK, 2 * M_DK), F32), pltpu.VMEM((2, SUBLANES, LANES), F32)],
        compiler_params=_cparams(("arbitrary",)),
        name="mlstm_scan",
    )(*args)


def _rglru_kernel(x_ref, xp_ref, xn_ref, cw_ref, cb_ref, wr_ref, br_ref, wi_ref, bi_ref, lam_ref, h0_ref,
                  h_ref, hf_ref, xpad, carry, *, seqs):
    L = seqs.chunk
    rev = seqs.reverse
    g = seqs.chunk_of_step(pl.program_id(0))
    is_ctx, _, _, first, last = seqs.info(g)
    head = last if rev else first
    tail = first if rev else last
    halo = SUBLANES
    xpad[0:halo, :] = xp_ref[...] * jnp.where(head, 0.0, 1.0).astype(F32)
    xpad[halo:halo + L, :] = x_ref[...]
    xpad[halo + L:2 * halo + L, :] = xn_ref[...] * jnp.where(tail, 0.0, 1.0).astype(F32)
    x = cb_ref[...]
    for j in range(CONV_W):
        x = x + cw_ref[j:j + 1, :] * xpad[halo - 2 + j:halo - 2 + j + L, :]

    rs, is_ = [], []
    for n in range(R_BLOCKS):
        xb = x[:, n * R_BW:(n + 1) * R_BW].astype(BF16)
        rs.append(jnp.dot(xb, wr_ref[n].astype(BF16), preferred_element_type=F32))
        is_.append(jnp.dot(xb, wi_ref[n].astype(BF16), preferred_element_type=F32))
    rg = jax.nn.sigmoid(jnp.concatenate(rs, axis=-1) + br_ref[...])
    ig = jax.nn.sigmoid(jnp.concatenate(is_, axis=-1) + bi_ref[...])
    log_a = -RG_C * rg * jax.nn.softplus(-lam_ref[...])
    a = jnp.exp(log_a)
    u = jnp.sqrt(1.0 - jnp.exp(2.0 * log_a)) * (ig * x)

    sub = lax.broadcasted_iota(I32, (L, R_WIDTH), 0) & (SUBLANES - 1)
    s = 1
    while s < SUBLANES:
        if rev:
            ok = sub < SUBLANES - s
            a_sh, u_sh = pltpu.roll(a, L - s, 0), pltpu.roll(u, L - s, 0)
        else:
            ok = sub >= s
            a_sh, u_sh = pltpu.roll(a, s, 0), pltpu.roll(u, s, 0)
        u = jnp.where(ok, a * u_sh + u, u)
        a = jnp.where(ok, a * a_sh, a)
        s *= 2

    @pl.when(first)
    def _():
        carry[...] = h0_ref[...] * jnp.where(is_ctx, 0.0, 1.0).astype(F32)

    n_grp = L // SUBLANES
    edge = 0 if rev else SUBLANES - 1
    hprev = carry[...]
    for k in range(n_grp):
        gidx = (n_grp - 1 - k) if rev else k
        rows = slice(gidx * SUBLANES, (gidx + 1) * SUBLANES)
        hg = a[rows] * hprev + u[rows]
        h_ref[rows, :] = hg
        hprev = hg[edge:edge + 1, :]
    carry[...] = hprev

    @pl.when(last)
    def _():
        hf_ref[...] = hprev


def rglru_direction(z, conv_w, conv_b, wr, br, wi, bi, lam, h0, layer, direction, n_ctx, t_ctx, n_lat, t_lat):
    seqs = _Seqs(n_ctx, t_ctx, n_lat, t_lat, R_CHUNK, direction == 1)
    n_seq = n_ctx + n_lat
    L = R_CHUNK
    n_rows = seqs.n_chunks * L
    hb = L // SUBLANES
    nb8 = n_rows // SUBLANES
    cm = lambda s: seqs.chunk_of_step(s)
    lat_of = lambda s: seqs.info(cm(s))[2]
    seq_of = lambda s: seqs.info(cm(s))[1]
    xcol = _C_XR // R_WIDTH
    vec_ld = lambda: pl.BlockSpec((None, None, 1, R_WIDTH), lambda s: (layer, direction, 0, 0))
    mat_ld = lambda: pl.BlockSpec((None, None, R_BLOCKS, R_BW, R_BW), lambda s: (layer, direction, 0, 0, 0))
    kern = functools.partial(_rglru_kernel, seqs=seqs)
    depth = conv_w.shape[0]
    r4 = lambda a: a.reshape(depth, 2, 1, R_WIDTH)
    return pl.pallas_call(
        kern,
        out_shape=(jax.ShapeDtypeStruct((n_rows, R_WIDTH), F32), jax.ShapeDtypeStruct((n_seq, 1, R_WIDTH), F32)),
        grid=(seqs.n_chunks,),
        in_specs=[
            pl.BlockSpec((L, R_WIDTH), lambda s: (cm(s), xcol)),
            pl.BlockSpec((SUBLANES, R_WIDTH), lambda s: (jnp.maximum(cm(s) * hb - 1, 0), xcol)),
            pl.BlockSpec((SUBLANES, R_WIDTH), lambda s: (jnp.minimum((cm(s) + 1) * hb, nb8 - 1), xcol)),
            pl.BlockSpec((None, CONV_W, R_WIDTH), lambda s: (layer, 0, 0)),
            pl.BlockSpec((None, 1, R_WIDTH), lambda s: (layer, 0, 0)),
            mat_ld(), vec_ld(), mat_ld(), vec_ld(), vec_ld(),
            pl.BlockSpec((None, None, None, 1, R_WIDTH), lambda s: (lat_of(s), layer, direction, 0, 0)),
        ],
        out_specs=(pl.BlockSpec((L, R_WIDTH), lambda s: (cm(s), 0)),
                   pl.BlockSpec((None, 1, R_WIDTH), lambda s: (seq_of(s), 0, 0))),
        scratch_shapes=[pltpu.VMEM((L + 2 * SUBLANES, R_WIDTH), F32), pltpu.VMEM((1, R_WIDTH), F32)],
        compiler_params=_cparams(("arbitrary",)),
        name=f"rglru_d{direction}",
    )(z, z, z, conv_w, conv_b.reshape(depth, 1, R_WIDTH), wr, r4(br), wi, r4(bi), r4(lam),
      h0.reshape(h0.shape[0], depth, 2, 1, R_WIDTH))


def _gelu_tanh(x):
    return 0.5 * x * (1.0 + jnp.tanh(math.sqrt(2.0 / math.pi) * (x + 0.044715 * (x * x * x))))


def _out_kernel(x_ref, ac_ref, al_ref, mf_ref, mb_ref, om_ref, rf_ref, rb_ref, yr_ref, mg_ref, gate_ref, w_ref,
                o_ref, *, ctx_tiles, sub):
    is_ctx = pl.program_id(0) < ctx_tiles
    for r0 in range(0, x_ref.shape[0], sub):
        rows = slice(r0, r0 + sub)
        att = jnp.where(is_ctx, ac_ref[rows, :], al_ref[rows, :])
        hs = mf_ref[rows, :] + mb_ref[rows, :]
        parts = [att]
        for h in range(M_HEADS):
            cs = slice(h * M_DK, (h + 1) * M_DK)
            hh = hs[:, cs]
            hn = hh * lax.rsqrt(jnp.mean(hh * hh, axis=-1, keepdims=True) + NORM_EPS) * mg_ref[:, cs]
            parts.append((hn * jax.nn.sigmoid(om_ref[rows, cs])).astype(BF16))
        parts.append(((rf_ref[rows, :] + rb_ref[rows, :]) * _gelu_tanh(yr_ref[rows, :])).astype(BF16))
        mix = jnp.concatenate(parts, axis=1)
        o_ref[rows, :] = x_ref[rows, :] + gate_ref[...] * jnp.dot(mix, w_ref[...], preferred_element_type=F32)


def out_proj(x, a_ctx, a_lat, mf, mb, rf, rb, z, mnorm_g, mod, w_out, nc, tl, tm=512, sub=256):
    n, d = x.shape
    grp = lambda i: _group_of_tile(i * tm, nc, tl)
    ctx_tiles = nc // tm
    lat_tiles = (n - nc) // tm
    rowblk = lambda w, col: pl.BlockSpec((tm, w), lambda i: (i, col))
    kern = functools.partial(_out_kernel, ctx_tiles=ctx_tiles, sub=sub)
    return pl.pallas_call(
        kern,
        out_shape=jax.ShapeDtypeStruct((n, d), F32),
        grid=(n // tm,),
        in_specs=[
            rowblk(d, 0),
            pl.BlockSpec((tm, A_WIDTH), lambda i: (jnp.minimum(i, ctx_tiles - 1), 0)),
            pl.BlockSpec((tm, A_WIDTH), lambda i: (jnp.clip(i - ctx_tiles, 0, lat_tiles - 1), 0)),
            rowblk(M_WIDTH, 0), rowblk(M_WIDTH, 0), rowblk(M_WIDTH, _C_OM // M_WIDTH),
            rowblk(R_WIDTH, 0), rowblk(R_WIDTH, 0), rowblk(R_WIDTH, _C_YR // R_WIDTH),
            pl.BlockSpec((1, M_WIDTH), lambda i: (0, 0)),
            pl.BlockSpec((None, None, 1, d), lambda i: (grp(i), 2, 0, 0)),
            pl.BlockSpec((d, d), lambda i: (0, 0), pipeline_mode=pl.Buffered(1)),
        ],
        out_specs=rowblk(d, 0),
        compiler_params=_cparams(("arbitrary",)),
        name="out_proj",
    )(x, a_ctx, a_lat, mf, mb, z, rf, rb, z, mnorm_g.reshape(1, -1), mod, w_out)


def _bits(x):
    return lax.bitcast_convert_type(x, U32)


def _moe_pre_kernel(x_ref, g_ref, sh_ref, sc_ref, wh_ref, wl_ref, br_ref, xp_ref, rg_ref, re_ref):
    x = x_ref[...]
    y = x * lax.rsqrt(jnp.mean(x * x, axis=-1, keepdims=True) + NORM_EPS) * g_ref[...]
    xn = y * (1.0 + sc_ref[...]) + sh_ref[...]
    xb = xn.astype(BF16)
    xb32 = xb.astype(F32)
    xp_ref[...] = (_bits(xb32[:, HALF:]) & jnp.uint32(0xFFFF0000)) | (_bits(xb32[:, :HALF]) >> 16)

    xl = (xn - xb32).astype(BF16)
    lg = (jnp.dot(xb, wh_ref[...], preferred_element_type=F32) + jnp.dot(xb, wl_ref[...], preferred_element_type=F32)
          + jnp.dot(xl, wh_ref[...], preferred_element_type=F32) + br_ref[...])
    lane = lax.broadcasted_iota(I32, lg.shape, 1).astype(F32)
    ninf = jnp.float32(-jnp.inf)

    def top(mask):
        val = jnp.max(jnp.where(mask, lg, ninf), axis=-1, keepdims=True)
        idx = jnp.min(jnp.where(mask & (lg == val), lane, float(LANES)), axis=-1, keepdims=True)
        return val, idx

    is_g = lane < N_GROUPS
    g_val, g_idx = top(is_g)
    g_w = 1.0 / jnp.sum(jnp.where(is_g, jnp.exp(lg - g_val), 0.0), axis=-1, keepdims=True)
    e_lo = N_GROUPS + g_idx * EXPERTS_PER_GROUP
    in_grp = (lane >= e_lo) & (lane < e_lo + EXPERTS_PER_GROUP)
    v1, i1 = top(in_grp)
    v2, i2 = top(in_grp & (lane != i1))
    t = jnp.exp(v2 - v1)
    w1 = g_w / (1.0 + t)
    rg_ref[...] = jnp.where(lane == 0, w1, jnp.where(lane == 1, w1 * t, 0.0))
    re_ref[...] = jnp.where(lane == 0, i1 - N_GROUPS, jnp.where(lane == 1, i2 - N_GROUPS, 0.0)).astype(I32)


def moe_pre(x, norm_g, mod, w_router, b_router, nc, tl, tm=512):
    n, d = x.shape
    grp = lambda i: _group_of_tile(i * tm, nc, tl)
    lane_out = lambda: pl.BlockSpec((tm, LANES), lambda i: (i, 0))
    w_hi = w_router.astype(BF16)
    return pl.pallas_call(
        _moe_pre_kernel,
        out_shape=(jax.ShapeDtypeStruct((n, HALF), U32), jax.ShapeDtypeStruct((n, LANES), F32),
                   jax.ShapeDtypeStruct((n, LANES), I32)),
        grid=(n // tm,),
        in_specs=[
            pl.BlockSpec((tm, d), lambda i: (i, 0)),
            pl.BlockSpec((1, d), lambda i: (0, 0)),
            pl.BlockSpec((None, None, 1, d), lambda i: (grp(i), 3, 0, 0)),
            pl.BlockSpec((None, None, 1, d), lambda i: (grp(i), 4, 0, 0)),
            pl.BlockSpec((d, LANES), lambda i: (0, 0)),
            pl.BlockSpec((d, LANES), lambda i: (0, 0)),
            pl.BlockSpec((1, LANES), lambda i: (0, 0)),
        ],
        out_specs=(pl.BlockSpec((tm, HALF), lambda i: (i, 0)), lane_out(), lane_out()),
        compiler_params=_cparams(("arbitrary",)),
        name="moe_pre",
    )(x, norm_g.reshape(1, d), mod, mod, w_hi, (w_router - w_hi.astype(F32)).astype(BF16), b_router)


def _expert_changed(be_ref, j):
    return (j == 0) | (be_ref[j] != be_ref[jnp.maximum(j - 1, 0)])


_ROW_STEP = LANES


def _for_row_count(nv, tb, body):
    for r in range(_ROW_STEP, tb + 1, _ROW_STEP):
        @pl.when((nv > r - _ROW_STEP) & (nv <= r))
        def _(r=r):
            body(r)


def _moe_up_kernel(be_ref, bs_ref, nv_ref, x_ref, wg_ref, wu_ref, h_ref, wg_s, wu_s):
    j = pl.program_id(1)
    nv = nv_ref[j]
    tb = x_ref.shape[0]

    @pl.when((nv > 0) & _expert_changed(be_ref, j))
    def _():
        wg_s[...] = wg_ref[...].astype(BF16)
        wu_s[...] = wu_ref[...].astype(BF16)

    def body(r):
        w = x_ref[0:r, :]
        lo = lax.bitcast_convert_type(w << 16, F32).astype(BF16)
        hi = lax.bitcast_convert_type(w & jnp.uint32(0xFFFF0000), F32).astype(BF16)

        def mm(w_s):
            return (jnp.dot(lo, w_s[0:HALF, :], preferred_element_type=F32)
                    + jnp.dot(hi, w_s[HALF:, :], preferred_element_type=F32))

        gt, up = mm(wg_s), mm(wu_s)
        h_ref[0:r, :] = (gt * jax.nn.sigmoid(gt) * up).astype(BF16)
        if r < tb:
            h_ref[r:tb, :] = jnp.zeros((tb - r, h_ref.shape[1]), BF16)

    _for_row_count(nv, tb, body)


def _moe_down_kernel(be_ref, bs_ref, nv_ref, h_ref, wd_ref, y_ref, wd_s):
    j = pl.program_id(1)
    nv = nv_ref[j]
    tb = h_ref.shape[0]

    @pl.when((nv > 0) & _expert_changed(be_ref, j))
    def _():
        wd_s[...] = wd_ref[...].astype(BF16)

    def body(r):
        y_ref[0:r, :] = jnp.dot(h_ref[0:r, :], wd_s[...], preferred_element_type=F32)
        if r < tb:
            y_ref[r:tb, :] = jnp.zeros((tb - r, y_ref.shape[1]), F32)

    _for_row_count(nv, tb, body)


def moe_experts(xb, blk_e, blk_src, n_valid, w_gate, w_up, w_down, layer, tb=MOE_TB, tc=512, tc2=D_MODEL):
    rows = xb.shape[0]
    n_blocks = rows // tb
    d, de = w_gate.shape[-2:]
    h = pl.pallas_call(
        _moe_up_kernel,
        out_shape=jax.ShapeDtypeStruct((rows, de), BF16),
        grid_spec=pltpu.PrefetchScalarGridSpec(
            num_scalar_prefetch=3,
            grid=(de // tc, n_blocks),
            in_specs=[
                pl.BlockSpec((tb, HALF), lambda c, j, be, bs, nu: (bs[j], 0)),
                pl.BlockSpec((None, None, d, tc), lambda c, j, be, bs, nu: (layer, be[j], 0, c)),
                pl.BlockSpec((None, None, d, tc), lambda c, j, be, bs, nu: (layer, be[j], 0, c)),
            ],
            out_specs=pl.BlockSpec((tb, tc), lambda c, j, be, bs, nu: (bs[j], c)),
            scratch_shapes=[pltpu.VMEM((d, tc), BF16), pltpu.VMEM((d, tc), BF16)],
        ),
        compiler_params=_cparams(("arbitrary", "arbitrary")),
        name="moe_up",
    )(blk_e, blk_src, n_valid, xb, w_gate, w_up)
    return pl.pallas_call(
        _moe_down_kernel,
        out_shape=jax.ShapeDtypeStruct((rows, d), F32),
        grid_spec=pltpu.PrefetchScalarGridSpec(
            num_scalar_prefetch=3,
            grid=(d // tc2, n_blocks),
            in_specs=[
                pl.BlockSpec((tb, de), lambda c, j, be, bs, nu: (bs[j], 0)),
                pl.BlockSpec((None, None, de, tc2), lambda c, j, be, bs, nu: (layer, be[j], 0, c)),
            ],
            out_specs=pl.BlockSpec((tb, tc2), lambda c, j, be, bs, nu: (bs[j], c)),
            scratch_shapes=[pltpu.VMEM((de, tc2), BF16)],
        ),
        compiler_params=_cparams(("arbitrary", "arbitrary")),
        name="moe_down",
    )(blk_e, blk_src, n_valid, h, w_down)


def _moe_combine_kernel(x_ref, y0_ref, y1_ref, gt_ref, gate_ref, o_ref):
    y = gt_ref[:, 0:1] * y0_ref[...] + gt_ref[:, 1:2] * y1_ref[...]
    o_ref[...] = x_ref[...] + gate_ref[...] * y


def moe_combine(x, y0, y1, gates, mod, nc, tl, tm=512):
    n, d = x.shape
    grp = lambda i: _group_of_tile(i * tm, nc, tl)
    blk = lambda: pl.BlockSpec((tm, d), lambda i: (i, 0))
    return pl.pallas_call(
        _moe_combine_kernel,
        out_shape=jax.ShapeDtypeStruct((n, d), F32),
        grid=(n // tm,),
        in_specs=[blk(), blk(), blk(), pl.BlockSpec((tm, LANES), lambda i: (i, 0)),
                  pl.BlockSpec((None, None, 1, d), lambda i: (grp(i), 5, 0, 0))],
        out_specs=blk(),
        compiler_params=_cparams(("arbitrary",)),
        name="moe_combine",
    )(x, y0, y1, gates, mod)


def moe_layout(eid, tb=MOE_TB):
    n = eid.shape[0]
    s_len = n * TOP_K
    flat_e = eid.reshape(s_len)
    onehot = (flat_e[:, None] == jnp.arange(N_EXPERTS, dtype=I32)[None, :]).astype(I32)
    rank = jnp.sum((jnp.cumsum(onehot, axis=0) - onehot) * onehot, axis=1)
    counts = jnp.sum(onehot, axis=0)
    start = jnp.cumsum(counts) - counts
    nblk_e = (counts + tb - 1) // tb
    blk_end = jnp.cumsum(nblk_e)
    blk_start = blk_end - nblk_e
    dest = jnp.sum(onehot * blk_start[None, :], axis=1) * tb + rank
    n_blocks = s_len // tb + N_EXPERTS
    n_used = blk_end[-1]
    jc = jnp.minimum(jnp.arange(n_blocks, dtype=I32), n_used - 1)
    blk_e = jnp.minimum(jnp.sum((blk_end[None, :] <= jc[:, None]).astype(I32), axis=1), N_EXPERTS - 1).astype(I32)
    j = jnp.arange(n_blocks, dtype=I32)
    used = j < n_used
    n_valid = jnp.where(used, jnp.clip(counts[blk_e] - (jc - blk_start[blk_e]) * tb, 0, tb), 0).astype(I32)
    order = jnp.argsort(flat_e).astype(I32)
    sorted_tok = jnp.concatenate([order // TOP_K, jnp.arange(tb, dtype=I32) % n])
    src = jnp.where(used, jnp.clip(start[blk_e] + (jc - blk_start[blk_e]) * tb, 0, s_len), (j * tb) % s_len)
    slot_tok = jax.vmap(lambda o: lax.dynamic_slice(sorted_tok, (o,), (tb,)))(src).reshape(n_blocks * tb)
    return dest.reshape(n, TOP_K), slot_tok, blk_e, jc, n_valid


def _take_rows(a, idx):
    return a.at[idx].get(mode="promise_in_bounds")


def _final_norm_kernel(x_ref, g_ref, o_ref):
    x = x_ref[...]
    o_ref[...] = x * lax.rsqrt(jnp.mean(x * x, axis=-1, keepdims=True) + NORM_EPS) * g_ref[...]


def final_norm(x, g, tm=512):
    n, d = x.shape
    return pl.pallas_call(
        _final_norm_kernel,
        out_shape=jax.ShapeDtypeStruct((n, d), F32),
        grid=(n // tm,),
        in_specs=[pl.BlockSpec((tm, d), lambda i: (i, 0)), pl.BlockSpec((1, d), lambda i: (0, 0))],
        out_specs=pl.BlockSpec((tm, d), lambda i: (i, 0)),
        compiler_params=_cparams(("arbitrary",)),
        name="final_norm",
    )(x, g.reshape(1, d))


def _pad_lanes(a):
    return jnp.pad(a, ((0, 0), (0, LANES - a.shape[1])))


def kernel(x_prompt, x_sample, cache_k, cache_v, state_mlstm_C, state_mlstm_n, state_mlstm_m, state_rglru_h, c, c_ctx, ada_w, ada_b, norm1_g, w_in, attn_sink, mlstm_gate_b, mlstm_norm_g, rg_conv_w, rg_conv_b, rg_wr, rg_br, rg_wi, rg_bi, rg_lam, w_out, norm2_g, router_wg, router_bg, router_we, router_be, exp_w_gate, exp_w_up, exp_w_down, final_norm_g):
    bc, tc, d = x_prompt.shape
    bl, tl, _ = x_sample.shape
    depth = w_in.shape[0]
    past = cache_k.shape[2]
    nc, nl = bc * tc, bl * tl
    tm = math.gcd(1024, math.gcd(tl, nc))

    x = jnp.concatenate([x_prompt.reshape(nc, d), x_sample.reshape(nl, d)], axis=0)
    cvec = jnp.concatenate([c_ctx[None, :], c, jnp.zeros((SUBLANES - 1 - bl, d), F32)], axis=0)
    mods = adaln_all(cvec, ada_w, ada_b).reshape(depth, SUBLANES, 6, 1, d)
    rope_c, rope_s = rope_tables(tl)
    gsplit = Z_MAIN - 2 * R_WIDTH
    w_main = jnp.concatenate([w_in[:, :, :gsplit], w_in[:, :, gsplit + M_GATES:]], axis=2).astype(BF16)
    w_gcol = jnp.pad(w_in[:, :, gsplit:gsplit + M_GATES], ((0, 0), (0, 0), (0, LANES - M_GATES))).astype(BF16)
    w_out_b = w_out.astype(BF16)
    ck = cache_k.reshape(bl, depth, past, KV_WIDTH)
    cv = cache_v.reshape(bl, depth, past, KV_WIDTH)
    cn0 = jnp.concatenate([state_mlstm_C, jnp.broadcast_to(state_mlstm_n[..., None], state_mlstm_C.shape)], axis=-1)

    new_k, new_v, new_c, new_n, new_m, new_h = [], [], [], [], [], []
    for l in range(depth):
        mod = mods[l]
        z, zg = in_proj(x, norm1_g[l], mod, w_main[l], w_gcol[l], nc, tl, tm=tm)

        a_ctx = attention_context(z, attn_sink[l], bc, tc)
        a_lat = attention_latent(z, ck, cv, l, attn_sink[l], rope_c, rope_s, nc, bl, tl)

        rows, tiles, kt = mlstm_gates(z, zg, _pad_lanes(mlstm_gate_b[l].reshape(1, M_GATES)), tile=tm)
        res = mlstm_scan(z, rows, tiles, kt, cn0, state_mlstm_m, l, bc, tc, bl, tl)
        mh = res[0:2]
        for dr in range(2):
            cnf, mf = res[2 + 2 * dr:4 + 2 * dr]
            new_c.append(cnf[:bc, :, :, :M_DK])
            new_n.append(cnf[:bc, :, :, M_DK])
            new_m.append(mf[:bc, :M_HEADS, 0])
        rh = []
        for dr in range(2):
            rd, hf = rglru_direction(z, rg_conv_w, rg_conv_b, rg_wr, rg_br, rg_wi, rg_bi, rg_lam, state_rglru_h,
                                     l, dr, bc, tc, bl, tl)
            rh.append(rd)
            new_h.append(hf[:bc, 0])
        new_k.append(z[:nc, _C_K:_C_K + KV_WIDTH].reshape(bc, tc, KV_HEADS, HEAD_DIM))
        new_v.append(z[:nc, _C_V:_C_V + KV_WIDTH].reshape(bc, tc, KV_HEADS, HEAD_DIM))

        x = out_proj(x, a_ctx, a_lat, mh[0], mh[1], rh[0], rh[1], z, mlstm_norm_g[l], mod, w_out_b[l], nc, tl)

        w_router = _pad_lanes(jnp.concatenate([router_wg[l], router_we[l]], axis=1))
        b_router = _pad_lanes(jnp.concatenate([router_bg[l], router_be[l]])[None, :])
        xp, route_g, route_e = moe_pre(x, norm2_g[l], mod, w_router, b_router, nc, tl)
        dest, slot_tok, blk_e, blk_src, n_valid = moe_layout(route_e[:, :TOP_K])
        yb = moe_experts(_take_rows(xp, slot_tok), blk_e, blk_src, n_valid, exp_w_gate, exp_w_up, exp_w_down, l)
        x = moe_combine(x, _take_rows(yb, dest[:, 0]), _take_rows(yb, dest[:, 1]), route_g, mod, nc, tl)

    y = final_norm(x, final_norm_g)
    stack2 = lambda parts: jnp.stack([jnp.stack(parts[2 * l:2 * l + 2], axis=1) for l in range(depth)], axis=1)
    return (y[:nc].reshape(bc, tc, d), y[nc:].reshape(bl, tl, d),
            jnp.stack(new_k, axis=1), jnp.stack(new_v, axis=1),
            stack2(new_c), stack2(new_n), stack2(new_m), stack2(new_h))
```

```python
import functools
import math

import jax
import jax.numpy as jnp
from jax import lax
from jax.experimental import pallas as pl
from jax.experimental.pallas import tpu as pltpu

F32 = jnp.float32
BF16 = jnp.bfloat16
U32 = jnp.uint32
I32 = jnp.int32

D_MODEL = 2048
HEAD_DIM = 128
A_WIDTH = D_MODEL // 2
N_HEADS = A_WIDTH // HEAD_DIM
KV_HEADS = 2
GQA_GROUP = N_HEADS // KV_HEADS
KV_WIDTH = KV_HEADS * HEAD_DIM
WINDOW = 128
Q_BLOCK = 128
GRID_W = 64
ROPE_BASE = 10000.0
ROPE_AXIS = HEAD_DIM // 2
ATTN_SCALE = HEAD_DIM ** -0.5
NEG_INF = -1e30
M_WIDTH = D_MODEL // 4
M_HEADS = 4
M_DK = M_WIDTH // M_HEADS
M_GATES = 2 * 2 * M_HEADS
R_WIDTH = D_MODEL // 4
R_BLOCKS = 4
R_BW = R_WIDTH // R_BLOCKS
CONV_W = 4
RG_C = 8.0
N_GROUPS = 4
EXPERTS_PER_GROUP = 8
N_EXPERTS = N_GROUPS * EXPERTS_PER_GROUP
TOP_K = 2
D_EXPERT = D_MODEL // 2
NORM_EPS = 1e-6

LANES = 128
SUBLANES = 8
Z_MAIN = A_WIDTH + 2 * KV_WIDTH + 4 * M_WIDTH + 2 * R_WIDTH
_C_Q, _C_K, _C_V = 0, A_WIDTH, A_WIDTH + KV_WIDTH
_C_QM = A_WIDTH + 2 * KV_WIDTH
_C_KM, _C_VM, _C_OM = _C_QM + M_WIDTH, _C_QM + 2 * M_WIDTH, _C_QM + 3 * M_WIDTH
_C_XR, _C_YR = _C_QM + 4 * M_WIDTH, _C_QM + 4 * M_WIDTH + R_WIDTH

M_CHUNK = LANES
R_CHUNK = 256
MOE_TB = 512
VMEM_LIMIT = 56 * 1024 * 1024
HALF = D_MODEL // 2


def _cparams(sem):
    return pltpu.CompilerParams(dimension_semantics=sem, vmem_limit_bytes=VMEM_LIMIT)


def _group_of_tile(row0, nc, tl):
    return jnp.where(row0 < nc, 0, 1 + (jnp.maximum(row0 - nc, 0)) // tl)


def _adaln_kernel(c_ref, w_ref, b_ref, o_ref):
    c = c_ref[...]
    s = (c * jax.nn.sigmoid(c)).astype(BF16)
    o_ref[...] = jnp.dot(s, w_ref[...].astype(BF16), preferred_element_type=F32) + b_ref[...]


def adaln_all(cvec, ada_w, ada_b, tn=1024):
    depth, d, d6 = ada_w.shape
    return pl.pallas_call(
        _adaln_kernel,
        out_shape=jax.ShapeDtypeStruct((depth, SUBLANES, d6), F32),
        grid=(depth, d6 // tn),
        in_specs=[
            pl.BlockSpec((SUBLANES, d), lambda l, j: (0, 0)),
            pl.BlockSpec((None, d, tn), lambda l, j: (l, 0, j)),
            pl.BlockSpec((None, 1, tn), lambda l, j: (l, 0, j)),
        ],
        out_specs=pl.BlockSpec((None, SUBLANES, tn), lambda l, j: (l, 0, j)),
        compiler_params=_cparams(("arbitrary", "arbitrary")),
        name="adaln",
    )(cvec, ada_w, ada_b.reshape(depth, 1, d6))


def _in_kernel(x_ref, g_ref, sh_ref, sc_ref, w_ref, wg_ref, z_ref, zg_ref, xn_ref):
    @pl.when(pl.program_id(1) == 0)
    def _():
        x = x_ref[...]
        y = x * lax.rsqrt(jnp.mean(x * x, axis=-1, keepdims=True) + NORM_EPS) * g_ref[...]
        xn = (y * (1.0 + sc_ref[...]) + sh_ref[...]).astype(BF16)
        xn_ref[...] = xn
        zg_ref[...] = jnp.dot(xn, wg_ref[...], preferred_element_type=F32)

    z_ref[...] = jnp.dot(xn_ref[...], w_ref[...], preferred_element_type=F32)


def in_proj(x, norm_g, mod, w_main, w_gate, nc, tl, tm=1024, tn=768):
    n, d = x.shape
    zw = w_main.shape[1]
    grp = lambda i: _group_of_tile(i * tm, nc, tl)
    return pl.pallas_call(
        _in_kernel,
        out_shape=(jax.ShapeDtypeStruct((n, zw), F32), jax.ShapeDtypeStruct((n, LANES), F32)),
        grid=(n // tm, zw // tn),
        in_specs=[
            pl.BlockSpec((tm, d), lambda i, j: (i, 0)),
            pl.BlockSpec((1, d), lambda i, j: (0, 0)),
            pl.BlockSpec((None, None, 1, d), lambda i, j: (grp(i), 0, 0, 0)),
            pl.BlockSpec((None, None, 1, d), lambda i, j: (grp(i), 1, 0, 0)),
            pl.BlockSpec((d, tn), lambda i, j: (0, j)),
            pl.BlockSpec((d, LANES), lambda i, j: (0, 0)),
        ],
        out_specs=(pl.BlockSpec((tm, tn), lambda i, j: (i, j)), pl.BlockSpec((tm, LANES), lambda i, j: (i, 0))),
        scratch_shapes=[pltpu.VMEM((tm, d), BF16)],
        compiler_params=_cparams(("arbitrary", "arbitrary")),
        name="in_proj",
    )(x, norm_g.reshape(1, d), mod, mod, w_main, w_gate)


def _rope(x, c, s):
    lane = lax.broadcasted_iota(I32, x.shape, 1)
    half = ROPE_AXIS // 2
    partner = jnp.where((lane & (ROPE_AXIS - 1)) < half, pltpu.roll(x, HEAD_DIM - half, 1), pltpu.roll(x, half, 1))
    return x * c + partner * s


def _softmax_pv(parts, sink_col):
    m = sink_col
    for s, _ in parts:
        m = jnp.maximum(m, jnp.max(s, axis=-1, keepdims=True))
    den = jnp.exp(sink_col - m)
    acc = None
    for s, v in parts:
        p = jnp.exp(s - m)
        den = den + jnp.sum(p, axis=-1, keepdims=True)
        pv = jnp.dot(p.astype(BF16), v, preferred_element_type=F32)
        acc = pv if acc is None else acc + pv
    return acc / den


def _qk(q, k):
    return lax.dot_general(q, k, (((1,), (1,)), ((), ())), preferred_element_type=F32) * ATTN_SCALE


def _sink_column(sink_ref, n, rows):
    ridx = lax.broadcasted_iota(I32, (GQA_GROUP * rows, 1), 0)
    col = jnp.full((GQA_GROUP * rows, 1), sink_ref[n * GQA_GROUP], F32)
    for g in range(1, GQA_GROUP):
        col = jnp.where(ridx >= g * rows, sink_ref[n * GQA_GROUP + g], col)
    return col


def _attn_ctx_kernel(sink_ref, q_ref, k_ref, v_ref, o_ref):
    t = q_ref.shape[0]
    for n in range(KV_HEADS):
        k = k_ref[:, n * HEAD_DIM:(n + 1) * HEAD_DIM].astype(BF16)
        v = v_ref[:, n * HEAD_DIM:(n + 1) * HEAD_DIM].astype(BF16)
        q = jnp.concatenate(
            [q_ref[:, (n * GQA_GROUP + g) * HEAD_DIM:(n * GQA_GROUP + g + 1) * HEAD_DIM] for g in range(GQA_GROUP)],
            axis=0).astype(BF16)
        out = _softmax_pv([(_qk(q, k), v)], _sink_column(sink_ref, n, t))
        for g in range(GQA_GROUP):
            h = n * GQA_GROUP + g
            o_ref[:, h * HEAD_DIM:(h + 1) * HEAD_DIM] = out[g * t:(g + 1) * t].astype(o_ref.dtype)


def attention_context(z, sink, b, t):
    return pl.pallas_call(
        _attn_ctx_kernel,
        out_shape=jax.ShapeDtypeStruct((b * t, A_WIDTH), BF16),
        grid=(b,),
        in_specs=[
            pl.BlockSpec(memory_space=pltpu.SMEM),
            pl.BlockSpec((t, A_WIDTH), lambda i: (i, _C_Q // A_WIDTH)),
            pl.BlockSpec((t, KV_WIDTH), lambda i: (i, _C_K // KV_WIDTH)),
            pl.BlockSpec((t, KV_WIDTH), lambda i: (i, _C_V // KV_WIDTH)),
        ],
        out_specs=pl.BlockSpec((t, A_WIDTH), lambda i: (i, 0)),
        compiler_params=_cparams(("arbitrary",)),
        name="attn_ctx",
    )(sink, z, z, z)


def _attn_lat_kernel(sink_ref, q_ref, k_ref, v_ref, kc_ref, vc_ref, cq_ref, sq_ref, ck_ref, sk_ref, o_ref,
                     kr_ref, vp_ref, *, t):
    i = pl.program_id(1)
    rope_rows = 512

    @pl.when(i == 0)
    def _():
        zpad = jnp.zeros((WINDOW, KV_WIDTH), BF16)
        kr_ref[0:WINDOW, :] = zpad
        kr_ref[WINDOW + t:2 * WINDOW + t, :] = zpad
        vp_ref[0:WINDOW, :] = zpad
        vp_ref[WINDOW + t:2 * WINDOW + t, :] = zpad

        def body(c, carry):
            r0 = pl.multiple_of(c * rope_rows, rope_rows)
            cs, sn = ck_ref[pl.ds(r0, rope_rows), :], sk_ref[pl.ds(r0, rope_rows), :]
            for n in range(KV_HEADS):
                kk = k_ref[pl.ds(r0, rope_rows), n * HEAD_DIM:(n + 1) * HEAD_DIM]
                kr_ref[pl.ds(WINDOW + r0, rope_rows), n * HEAD_DIM:(n + 1) * HEAD_DIM] = _rope(kk, cs, sn).astype(BF16)
            vp_ref[pl.ds(WINDOW + r0, rope_rows), :] = v_ref[pl.ds(r0, rope_rows), :].astype(BF16)
            return carry

        lax.fori_loop(0, t // rope_rows, body, 0)

    span = Q_BLOCK + 2 * WINDOW
    rows = GQA_GROUP * Q_BLOCK
    r = lax.broadcasted_iota(I32, (rows, span), 0) & (Q_BLOCK - 1)
    c = lax.broadcasted_iota(I32, (rows, span), 1)
    kpos = (i - 1) * Q_BLOCK + c
    mask = (c >= r) & (c <= r + 2 * WINDOW) & (kpos >= 0) & (kpos < t)
    w0 = pl.multiple_of(i * Q_BLOCK, Q_BLOCK)
    cq, sq = cq_ref[...], sq_ref[...]
    for n in range(KV_HEADS):
        hs = slice(n * HEAD_DIM, (n + 1) * HEAD_DIM)
        q = jnp.concatenate(
            [_rope(q_ref[:, (n * GQA_GROUP + g) * HEAD_DIM:(n * GQA_GROUP + g + 1) * HEAD_DIM], cq, sq)
             for g in range(GQA_GROUP)], axis=0).astype(BF16)
        s_win = jnp.where(mask, _qk(q, kr_ref[pl.ds(w0, span), hs]), NEG_INF)
        s_ctx = _qk(q, kc_ref[:, hs].astype(BF16))
        out = _softmax_pv([(s_win, vp_ref[pl.ds(w0, span), hs]), (s_ctx, vc_ref[:, hs].astype(BF16))],
                          _sink_column(sink_ref, n, Q_BLOCK))
        for g in range(GQA_GROUP):
            h = n * GQA_GROUP + g
            o_ref[:, h * HEAD_DIM:(h + 1) * HEAD_DIM] = out[g * Q_BLOCK:(g + 1) * Q_BLOCK].astype(o_ref.dtype)


def attention_latent(z, cache_k, cache_v, layer, sink, rope_c, rope_s, row0, b, t):
    nqb = t // Q_BLOCK
    past = cache_k.shape[2]
    qb0 = row0 // Q_BLOCK
    tb0 = row0 // t
    kern = functools.partial(_attn_lat_kernel, t=t)
    return pl.pallas_call(
        kern,
        out_shape=jax.ShapeDtypeStruct((b * t, A_WIDTH), BF16),
        grid=(b, nqb),
        in_specs=[
            pl.BlockSpec(memory_space=pltpu.SMEM),
            pl.BlockSpec((Q_BLOCK, A_WIDTH), lambda bi, i: (qb0 + bi * nqb + i, _C_Q // A_WIDTH)),
            pl.BlockSpec((t, KV_WIDTH), lambda bi, i: (tb0 + bi, _C_K // KV_WIDTH)),
            pl.BlockSpec((t, KV_WIDTH), lambda bi, i: (tb0 + bi, _C_V // KV_WIDTH)),
            pl.BlockSpec((None, None, past, KV_WIDTH), lambda bi, i: (bi, layer, 0, 0)),
            pl.BlockSpec((None, None, past, KV_WIDTH), lambda bi, i: (bi, layer, 0, 0)),
            pl.BlockSpec((Q_BLOCK, HEAD_DIM), lambda bi, i: (i, 0)),
            pl.BlockSpec((Q_BLOCK, HEAD_DIM), lambda bi, i: (i, 0)),
            pl.BlockSpec((t, HEAD_DIM), lambda bi, i: (0, 0)),
            pl.BlockSpec((t, HEAD_DIM), lambda bi, i: (0, 0)),
        ],
        out_specs=pl.BlockSpec((Q_BLOCK, A_WIDTH), lambda bi, i: (bi * nqb + i, 0)),
        scratch_shapes=[pltpu.VMEM((t + 2 * WINDOW, KV_WIDTH), BF16), pltpu.VMEM((t + 2 * WINDOW, KV_WIDTH), BF16)],
        compiler_params=_cparams(("arbitrary", "arbitrary")),
        name="attn_lat",
    )(sink, z, z, z, cache_k, cache_v, rope_c, rope_s, rope_c, rope_s)


def rope_tables(t):
    pos = jnp.arange(t)
    inv = ROPE_BASE ** (-jnp.arange(0, ROPE_AXIS, 2, dtype=F32) / ROPE_AXIS)

    def cs(p):
        ang = p.astype(F32)[:, None] * inv[None, :]
        return jnp.cos(ang), jnp.sin(ang)

    cr, sr = cs(pos // GRID_W)
    cc, sc = cs(pos % GRID_W)
    return (jnp.concatenate([cr, cr, cc, cc], axis=-1), jnp.concatenate([-sr, sr, -sc, sc], axis=-1))


class _Seqs:
    def __init__(self, n_ctx, t_ctx, n_lat, t_lat, chunk, reverse):
        self.n_ctx, self.t_ctx, self.n_lat, self.t_lat = n_ctx, t_ctx, n_lat, t_lat
        self.chunk, self.reverse = chunk, reverse
        self.cpc, self.cpl = t_ctx // chunk, t_lat // chunk
        self.ctx_chunks = n_ctx * self.cpc
        self.n_chunks = self.ctx_chunks + n_lat * self.cpl

    def chunk_of_step(self, s):
        return (self.n_chunks - 1 - s) if self.reverse else s

    def info(self, g):
        is_ctx = g < self.ctx_chunks
        gl = jnp.maximum(g - self.ctx_chunks, 0)
        gc = jnp.minimum(g, self.ctx_chunks - 1)
        pos = jnp.where(is_ctx, gc % self.cpc, gl % self.cpl)
        per = jnp.where(is_ctx, self.cpc, self.cpl)
        lat = gl // self.cpl
        seq = jnp.where(is_ctx, gc // self.cpc, self.n_ctx + lat)
        head, tail = pos == 0, pos == per - 1
        return is_ctx, seq, lat, (tail if self.reverse else head), (head if self.reverse else tail)


def _lane_scan(x, op, reverse):
    n = x.shape[1]
    lane = lax.broadcasted_iota(I32, x.shape, 1)
    s = 1
    while s < n:
        if reverse:
            x = jnp.where(lane < n - s, op(x, pltpu.roll(x, n - s, 1)), x)
        else:
            x = jnp.where(lane >= s, op(x, pltpu.roll(x, s, 1)), x)
        s *= 2
    return x


_GROWS = 2 * SUBLANES


def _mlstm_gate_kernel(g_ref, gb_ref, k_ref, rows_ref, tiles_ref, kt_ref):
    L = M_CHUNK
    nck = g_ref.shape[0] // L
    g = g_ref[...] + gb_ref[...]
    s = jnp.concatenate([g[c * L:(c + 1) * L].T[0:_GROWS] for c in range(nck)], axis=0)
    row = lax.broadcasted_iota(I32, s.shape, 0)
    is_rev = (row & SUBLANES) != 0
    lf = jax.nn.log_sigmoid(s)
    f = jnp.where(is_rev, _lane_scan(lf, jnp.add, True), _lane_scan(lf, jnp.add, False))
    f = pltpu.roll(f, s.shape[0] - M_HEADS, 0)
    r = s - f
    cm = jnp.where(is_rev, _lane_scan(r, jnp.maximum, True), _lane_scan(r, jnp.maximum, False))
    rows_ref[...] = r
    pad = jnp.zeros((LANES - 2 * _GROWS, L), F32)
    scale = M_DK ** -0.5
    for c in range(nck):
        sl = slice(c * _GROWS, (c + 1) * _GROWS)
        cols = jnp.concatenate([cm[sl], f[sl], pad], axis=0).T
        for d in range(2):
            for h in range(M_HEADS):
                gi = d * SUBLANES + h
                tiles_ref[c, d, h] = jnp.broadcast_to(cols[:, gi:gi + 1], (L, LANES))
                tiles_ref[c, d, M_HEADS + h] = jnp.broadcast_to(cols[:, _GROWS + gi:_GROWS + gi + 1], (L, LANES))
        for h in range(M_HEADS):
            kt_ref[c, h] = (k_ref[c * L:(c + 1) * L, h * M_DK:(h + 1) * M_DK] * scale).T


def mlstm_gates(z, zg, gate_b, tile=512):
    n = zg.shape[0]
    L = M_CHUNK
    nck = tile // L
    return pl.pallas_call(
        _mlstm_gate_kernel,
        out_shape=(jax.ShapeDtypeStruct((n // L * _GROWS, L), F32),
                   jax.ShapeDtypeStruct((n // L, 2, 2 * M_HEADS, L, LANES), F32),
                   jax.ShapeDtypeStruct((n // L, M_HEADS, M_DK, L), F32)),
        grid=(n // tile,),
        in_specs=[pl.BlockSpec((tile, LANES), lambda i: (i, 0)), pl.BlockSpec((1, LANES), lambda i: (0, 0)),
                  pl.BlockSpec((tile, M_WIDTH), lambda i: (i, _C_KM // M_WIDTH))],
        out_specs=(pl.BlockSpec((nck * _GROWS, L), lambda i: (i, 0)),
                   pl.BlockSpec((nck, 2, 2 * M_HEADS, L, LANES), lambda i: (i, 0, 0, 0, 0)),
                   pl.BlockSpec((nck, M_HEADS, M_DK, L), lambda i: (i, 0, 0, 0))),
        compiler_params=_cparams(("arbitrary",)),
        name="mlstm_gates",
    )(zg, gate_b, z)


def _mlstm_kernel(m0_ref, qf_ref, vf_ref, ktf_ref, rf_ref, tf_ref, qb_ref, vb_ref, ktb_ref, rb_ref, tb_ref,
                  cn0f_ref, cn0b_ref, hf_ref, hb_ref, ocnf_ref, omf_ref, ocnb_ref, omb_ref,
                  cn_s, m_s, *, seqs, layer, depth):
    L = M_CHUNK
    step = pl.program_id(0)
    ti = lax.broadcasted_iota(I32, (L, L), 0)
    si = lax.broadcasted_iota(I32, (L, L), 1)
    ones = jnp.ones((L, M_DK), BF16)
    streams = (
        (0, seqs[0], qf_ref, vf_ref, ktf_ref, rf_ref, tf_ref, cn0f_ref, hf_ref, ocnf_ref, omf_ref),
        (1, seqs[1], qb_ref, vb_ref, ktb_ref, rb_ref, tb_ref, cn0b_ref, hb_ref, ocnb_ref, omb_ref),
    )
    for d, sq, q_ref, v_ref, kt_ref, rows_ref, tiles_ref, cn0_ref, h_ref, ocn_ref, om_ref in streams:
        rev = d == 1
        is_ctx, _, lat, first, last = sq.info(sq.chunk_of_step(step))

        @pl.when(first)
        def _(d=d, is_ctx=is_ctx, lat=lat, cn0_ref=cn0_ref):
            keep = jnp.where(is_ctx, 0.0, 1.0).astype(F32)
            cn_s[d] = cn0_ref[...] * keep
            m_s[d] = jnp.zeros((SUBLANES, LANES), F32)
            for h in range(M_HEADS):
                m0 = m0_ref[((lat * depth + layer) * 2 + d) * M_HEADS + h]
                m_s[d, h:h + 1, :] = jnp.full((1, LANES), m0, F32) * keep

        end = 0 if rev else L - 1
        causal = (si >= ti) if rev else (si <= ti)
        for h in range(M_HEADS):
            hs = slice(h * M_DK, (h + 1) * M_DK)
            gi = d * SUBLANES + h
            q = q_ref[:, hs].astype(BF16)
            vaug = jnp.concatenate([v_ref[:, hs].astype(BF16), ones], axis=1)
            kt = kt_ref[h]
            r_row = rows_ref[gi:gi + 1, :]
            f_b = tiles_ref[M_HEADS + h]
            m_old = m_s[d, h:h + 1, :]
            m_b = jnp.maximum(m_old, tiles_ref[h])
            dmat = jnp.where(causal, jnp.exp(jnp.where(causal, r_row - m_b, 0.0)), 0.0)
            s = jnp.dot(q, kt.astype(BF16), preferred_element_type=F32) * dmat
            wpq = (jnp.exp(m_old - m_b) * q.astype(F32)).astype(BF16)
            cn_old = cn_s[d, h]
            out = jnp.dot(jnp.concatenate([s.astype(BF16), wpq], axis=1),
                          jnp.concatenate([vaug, cn_old.astype(BF16)], axis=0), preferred_element_type=F32)
            h_ref[:, hs] = out[:, :M_DK] / jnp.maximum(jnp.abs(out[:, M_DK:]), jnp.exp(-(f_b + m_b)))
            m_end = m_b[end:end + 1, :]
            dec = jnp.exp(m_old - m_end)
            kwt = (kt * jnp.exp(r_row - m_end)).astype(BF16)
            cn_s[d, h] = jnp.concatenate([dec, dec], axis=1) * cn_old + jnp.dot(kwt, vaug, preferred_element_type=F32)
            m_s[d, h:h + 1, :] = f_b[end:end + 1, :] + m_end

        @pl.when(last)
        def _(d=d, ocn_ref=ocn_ref, om_ref=om_ref):
            ocn_ref[...] = cn_s[d]
            om_ref[...] = m_s[d]


def mlstm_scan(z, rows, tiles, kt, cn0, st_m, layer, n_ctx, t_ctx, n_lat, t_lat):
    L = M_CHUNK
    seqs = (_Seqs(n_ctx, t_ctx, n_lat, t_lat, L, False), _Seqs(n_ctx, t_ctx, n_lat, t_lat, L, True))
    n_seq = n_ctx + n_lat
    n_chunks = seqs[0].n_chunks
    kern = functools.partial(_mlstm_kernel, seqs=seqs, layer=layer, depth=st_m.shape[1])
    in_specs = [pl.BlockSpec(memory_space=pltpu.SMEM)]
    for d, sq in enumerate(seqs):
        cm = lambda s, sq=sq: sq.chunk_of_step(s)
        in_specs += [
            pl.BlockSpec((L, M_WIDTH), lambda s, cm=cm: (cm(s), _C_QM // M_WIDTH)),
            pl.BlockSpec((L, M_WIDTH), lambda s, cm=cm: (cm(s), _C_VM // M_WIDTH)),
            pl.BlockSpec((None, M_HEADS, M_DK, L), lambda s, cm=cm: (cm(s), 0, 0, 0)),
            pl.BlockSpec((_GROWS, L), lambda s, cm=cm: (cm(s), 0)),
            pl.BlockSpec((None, None, 2 * M_HEADS, L, LANES), lambda s, cm=cm, d=d: (cm(s), d, 0, 0, 0)),
        ]
    for d, sq in enumerate(seqs):
        lat_of = lambda s, sq=sq: sq.info(sq.chunk_of_step(s))[2]
        in_specs.append(pl.BlockSpec((None, None, None, M_HEADS, M_DK, 2 * M_DK),
                                     lambda s, f=lat_of, d=d: (f(s), layer, d, 0, 0, 0)))
    out_shape = [jax.ShapeDtypeStruct((n_chunks * L, M_WIDTH), F32)] * 2
    out_specs = [pl.BlockSpec((L, M_WIDTH), lambda s, sq=sq: (sq.chunk_of_step(s), 0)) for sq in seqs]
    for sq in seqs:
        seq_of = lambda s, sq=sq: sq.info(sq.chunk_of_step(s))[1]
        out_shape += [jax.ShapeDtypeStruct((n_seq, M_HEADS, M_DK, 2 * M_DK), F32),
                      jax.ShapeDtypeStruct((n_seq, SUBLANES, LANES), F32)]
        out_specs += [pl.BlockSpec((None, M_HEADS, M_DK, 2 * M_DK), lambda s, f=seq_of: (f(s), 0, 0, 0)),
                      pl.BlockSpec((None, SUBLANES, LANES), lambda s, f=seq_of: (f(s), 0, 0))]
    args = [st_m.reshape(-1), z, z, kt, rows, tiles, z, z, kt, rows, tiles, cn0, cn0]
    return pl.pallas_call(
        kern,
        out_shape=tuple(out_shape),
        grid=(n_chunks,),
        in_specs=in_specs,
        out_specs=tuple(out_specs),
        scratch_shapes=[pltpu.VMEM((2, M_HEADS, M_DK, 2 * M_DK), F32), pltpu.VMEM((2, SUBLANES, LANES), F32)],
        compiler_params=_cparams(("arbitrary",)),
        name="mlstm_scan",
    )(*args)


def _rglru_kernel(x_ref, xp_ref, xn_ref, cw_ref, cb_ref, wr_ref, br_ref, wi_ref, bi_ref, lam_ref, h0_ref,
                  h_ref, hf_ref, xpad, carry, *, seqs):
    L = seqs.chunk
    rev = seqs.reverse
    g = seqs.chunk_of_step(pl.program_id(0))
    is_ctx, _, _, first, last = seqs.info(g)
    head = last if rev else first
    tail = first if rev else last
    halo = SUBLANES
    xpad[0:halo, :] = xp_ref[...] * jnp.where(head, 0.0, 1.0).astype(F32)
    xpad[halo:halo + L, :] = x_ref[...]
    xpad[halo + L:2 * halo + L, :] = xn_ref[...] * jnp.where(tail, 0.0, 1.0).astype(F32)
    x = cb_ref[...]
    for j in range(CONV_W):
        x = x + cw_ref[j:j + 1, :] * xpad[halo - 2 + j:halo - 2 + j + L, :]

    rs, is_ = [], []
    for n in range(R_BLOCKS):
        xb = x[:, n * R_BW:(n + 1) * R_BW].astype(BF16)
        rs.append(jnp.dot(xb, wr_ref[n].astype(BF16), preferred_element_type=F32))
        is_.append(jnp.dot(xb, wi_ref[n].astype(BF16), preferred_element_type=F32))
    rg = jax.nn.sigmoid(jnp.concatenate(rs, axis=-1) + br_ref[...])
    ig = jax.nn.sigmoid(jnp.concatenate(is_, axis=-1) + bi_ref[...])
    log_a = -RG_C * rg * jax.nn.softplus(-lam_ref[...])
    a = jnp.exp(log_a)
    u = jnp.sqrt(1.0 - jnp.exp(2.0 * log_a)) * (ig * x)

    sub = lax.broadcasted_iota(I32, (L, R_WIDTH), 0) & (SUBLANES - 1)
    s = 1
    while s < SUBLANES:
        if rev:
            ok = sub < SUBLANES - s
            a_sh, u_sh = pltpu.roll(a, L - s, 0), pltpu.roll(u, L - s, 0)
        else:
            ok = sub >= s
            a_sh, u_sh = pltpu.roll(a, s, 0), pltpu.roll(u, s, 0)
        u = jnp.where(ok, a * u_sh + u, u)
        a = jnp.where(ok, a * a_sh, a)
        s *= 2

    @pl.when(first)
    def _():
        carry[...] = h0_ref[...] * jnp.where(is_ctx, 0.0, 1.0).astype(F32)

    n_grp = L // SUBLANES
    edge = 0 if rev else SUBLANES - 1
    hprev = carry[...]
    for k in range(n_grp):
        gidx = (n_grp - 1 - k) if rev else k
        rows = slice(gidx * SUBLANES, (gidx + 1) * SUBLANES)
        hg = a[rows] * hprev + u[rows]
        h_ref[rows, :] = hg
        hprev = hg[edge:edge + 1, :]
    carry[...] = hprev

    @pl.when(last)
    def _():
        hf_ref[...] = hprev


def rglru_direction(z, conv_w, conv_b, wr, br, wi, bi, lam, h0, layer, direction, n_ctx, t_ctx, n_lat, t_lat):
    seqs = _Seqs(n_ctx, t_ctx, n_lat, t_lat, R_CHUNK, direction == 1)
    n_seq = n_ctx + n_lat
    L = R_CHUNK
    n_rows = seqs.n_chunks * L
    hb = L // SUBLANES
    nb8 = n_rows // SUBLANES
    cm = lambda s: seqs.chunk_of_step(s)
    lat_of = lambda s: seqs.info(cm(s))[2]
    seq_of = lambda s: seqs.info(cm(s))[1]
    xcol = _C_XR // R_WIDTH
    vec_ld = lambda: pl.BlockSpec((None, None, 1, R_WIDTH), lambda s: (layer, direction, 0, 0))
    mat_ld = lambda: pl.BlockSpec((None, None, R_BLOCKS, R_BW, R_BW), lambda s: (layer, direction, 0, 0, 0))
    kern = functools.partial(_rglru_kernel, seqs=seqs)
    depth = conv_w.shape[0]
    r4 = lambda a: a.reshape(depth, 2, 1, R_WIDTH)
    return pl.pallas_call(
        kern,
        out_shape=(jax.ShapeDtypeStruct((n_rows, R_WIDTH), F32), jax.ShapeDtypeStruct((n_seq, 1, R_WIDTH), F32)),
        grid=(seqs.n_chunks,),
        in_specs=[
            pl.BlockSpec((L, R_WIDTH), lambda s: (cm(s), xcol)),
            pl.BlockSpec((SUBLANES, R_WIDTH), lambda s: (jnp.maximum(cm(s) * hb - 1, 0), xcol)),
            pl.BlockSpec((SUBLANES, R_WIDTH), lambda s: (jnp.minimum((cm(s) + 1) * hb, nb8 - 1), xcol)),
            pl.BlockSpec((None, CONV_W, R_WIDTH), lambda s: (layer, 0, 0)),
            pl.BlockSpec((None, 1, R_WIDTH), lambda s: (layer, 0, 0)),
            mat_ld(), vec_ld(), mat_ld(), vec_ld(), vec_ld(),
            pl.BlockSpec((None, None, None, 1, R_WIDTH), lambda s: (lat_of(s), layer, direction, 0, 0)),
        ],
        out_specs=(pl.BlockSpec((L, R_WIDTH), lambda s: (cm(s), 0)),
                   pl.BlockSpec((None, 1, R_WIDTH), lambda s: (seq_of(s), 0, 0))),
        scratch_shapes=[pltpu.VMEM((L + 2 * SUBLANES, R_WIDTH), F32), pltpu.VMEM((1, R_WIDTH), F32)],
        compiler_params=_cparams(("arbitrary",)),
        name=f"rglru_d{direction}",
    )(z, z, z, conv_w, conv_b.reshape(depth, 1, R_WIDTH), wr, r4(br), wi, r4(bi), r4(lam),
      h0.reshape(h0.shape[0], depth, 2, 1, R_WIDTH))


def _gelu_tanh(x):
    return 0.5 * x * (1.0 + jnp.tanh(math.sqrt(2.0 / math.pi) * (x + 0.044715 * (x * x * x))))


def _out_kernel(x_ref, ac_ref, al_ref, mf_ref, mb_ref, om_ref, rf_ref, rb_ref, yr_ref, mg_ref, gate_ref, w_ref,
                o_ref, *, ctx_tiles, sub):
    is_ctx = pl.program_id(0) < ctx_tiles
    for r0 in range(0, x_ref.shape[0], sub):
        rows = slice(r0, r0 + sub)
        att = jnp.where(is_ctx, ac_ref[rows, :], al_ref[rows, :])
        hs = mf_ref[rows, :] + mb_ref[rows, :]
        parts = [att]
        for h in range(M_HEADS):
            cs = slice(h * M_DK, (h + 1) * M_DK)
            hh = hs[:, cs]
            hn = hh * lax.rsqrt(jnp.mean(hh * hh, axis=-1, keepdims=True) + NORM_EPS) * mg_ref[:, cs]
            parts.append((hn * jax.nn.sigmoid(om_ref[rows, cs])).astype(BF16))
        parts.append(((rf_ref[rows, :] + rb_ref[rows, :]) * _gelu_tanh(yr_ref[rows, :])).astype(BF16))
        mix = jnp.concatenate(parts, axis=1)
        o_ref[rows, :] = x_ref[rows, :] + gate_ref[...] * jnp.dot(mix, w_ref[...], preferred_element_type=F32)


def out_proj(x, a_ctx, a_lat, mf, mb, rf, rb, z, mnorm_g, mod, w_out, nc, tl, tm=512, sub=256):
    n, d = x.shape
    grp = lambda i: _group_of_tile(i * tm, nc, tl)
    ctx_tiles = nc // tm
    lat_tiles = (n - nc) // tm
    rowblk = lambda w, col: pl.BlockSpec((tm, w), lambda i: (i, col))
    kern = functools.partial(_out_kernel, ctx_tiles=ctx_tiles, sub=sub)
    return pl.pallas_call(
        kern,
        out_shape=jax.ShapeDtypeStruct((n, d), F32),
        grid=(n // tm,),
        in_specs=[
            rowblk(d, 0),
            pl.BlockSpec((tm, A_WIDTH), lambda i: (jnp.minimum(i, ctx_tiles - 1), 0)),
            pl.BlockSpec((tm, A_WIDTH), lambda i: (jnp.clip(i - ctx_tiles, 0, lat_tiles - 1), 0)),
            rowblk(M_WIDTH, 0), rowblk(M_WIDTH, 0), rowblk(M_WIDTH, _C_OM // M_WIDTH),
            rowblk(R_WIDTH, 0), rowblk(R_WIDTH, 0), rowblk(R_WIDTH, _C_YR // R_WIDTH),
            pl.BlockSpec((1, M_WIDTH), lambda i: (0, 0)),
            pl.BlockSpec((None, None, 1, d), lambda i: (grp(i), 2, 0, 0)),
            pl.BlockSpec((d, d), lambda i: (0, 0), pipeline_mode=pl.Buffered(1)),
        ],
        out_specs=rowblk(d, 0),
        compiler_params=_cparams(("arbitrary",)),
        name="out_proj",
    )(x, a_ctx, a_lat, mf, mb, z, rf, rb, z, mnorm_g.reshape(1, -1), mod, w_out)


def _bits(x):
    return lax.bitcast_convert_type(x, U32)


def _moe_pre_kernel(x_ref, g_ref, sh_ref, sc_ref, wh_ref, wl_ref, br_ref, xp_ref, rg_ref, re_ref):
    x = x_ref[...]
    y = x * lax.rsqrt(jnp.mean(x * x, axis=-1, keepdims=True) + NORM_EPS) * g_ref[...]
    xn = y * (1.0 + sc_ref[...]) + sh_ref[...]
    xb = xn.astype(BF16)
    xb32 = xb.astype(F32)
    xp_ref[...] = (_bits(xb32[:, HALF:]) & jnp.uint32(0xFFFF0000)) | (_bits(xb32[:, :HALF]) >> 16)

    xl = (xn - xb32).astype(BF16)
    lg = (jnp.dot(xb, wh_ref[...], preferred_element_type=F32) + jnp.dot(xb, wl_ref[...], preferred_element_type=F32)
          + jnp.dot(xl, wh_ref[...], preferred_element_type=F32) + br_ref[...])
    lane = lax.broadcasted_iota(I32, lg.shape, 1).astype(F32)
    ninf = jnp.float32(-jnp.inf)

    def top(mask):
        val = jnp.max(jnp.where(mask, lg, ninf), axis=-1, keepdims=True)
        idx = jnp.min(jnp.where(mask & (lg == val), lane, float(LANES)), axis=-1, keepdims=True)
        return val, idx

    is_g = lane < N_GROUPS
    g_val, g_idx = top(is_g)
    g_w = 1.0 / jnp.sum(jnp.where(is_g, jnp.exp(lg - g_val), 0.0), axis=-1, keepdims=True)
    e_lo = N_GROUPS + g_idx * EXPERTS_PER_GROUP
    in_grp = (lane >= e_lo) & (lane < e_lo + EXPERTS_PER_GROUP)
    v1, i1 = top(in_grp)
    v2, i2 = top(in_grp & (lane != i1))
    t = jnp.exp(v2 - v1)
    w1 = g_w / (1.0 + t)
    rg_ref[...] = jnp.where(lane == 0, w1, jnp.where(lane == 1, w1 * t, 0.0))
    re_ref[...] = jnp.where(lane == 0, i1 - N_GROUPS, jnp.where(lane == 1, i2 - N_GROUPS, 0.0)).astype(I32)


def moe_pre(x, norm_g, mod, w_router, b_router, nc, tl, tm=512):
    n, d = x.shape
    grp = lambda i: _group_of_tile(i * tm, nc, tl)
    lane_out = lambda: pl.BlockSpec((tm, LANES), lambda i: (i, 0))
    w_hi = w_router.astype(BF16)
    return pl.pallas_call(
        _moe_pre_kernel,
        out_shape=(jax.ShapeDtypeStruct((n, HALF), U32), jax.ShapeDtypeStruct((n, LANES), F32),
                   jax.ShapeDtypeStruct((n, LANES), I32)),
        grid=(n // tm,),
        in_specs=[
            pl.BlockSpec((tm, d), lambda i: (i, 0)),
            pl.BlockSpec((1, d), lambda i: (0, 0)),
            pl.BlockSpec((None, None, 1, d), lambda i: (grp(i), 3, 0, 0)),
            pl.BlockSpec((None, None, 1, d), lambda i: (grp(i), 4, 0, 0)),
            pl.BlockSpec((d, LANES), lambda i: (0, 0)),
            pl.BlockSpec((d, LANES), lambda i: (0, 0)),
            pl.BlockSpec((1, LANES), lambda i: (0, 0)),
        ],
        out_specs=(pl.BlockSpec((tm, HALF), lambda i: (i, 0)), lane_out(), lane_out()),
        compiler_params=_cparams(("arbitrary",)),
        name="moe_pre",
    )(x, norm_g.reshape(1, d), mod, mod, w_hi, (w_router - w_hi.astype(F32)).astype(BF16), b_router)


def _expert_changed(be_ref, j):
    return (j == 0) | (be_ref[j] != be_ref[jnp.maximum(j - 1, 0)])


_ROW_STEP = LANES


def _for_row_count(nv, tb, body, out_ref):
    for r in range(_ROW_STEP, tb + 1, _ROW_STEP):
        @pl.when((nv > r - _ROW_STEP) & (nv <= r))
        def _(r=r):
            body(r)

    @pl.when(nv == 0)
    def _():
        out_ref[...] = jnp.zeros(out_ref.shape, out_ref.dtype)


def _moe_up_kernel(be_ref, bs_ref, nv_ref, nx_ref, x_ref, wg_hbm, wu_hbm, h_ref, wbuf, sem, wg_s, wu_s, slot_ref,
                   *, layer, tc):
    c, j = pl.program_id(0), pl.program_id(1)
    nv = nv_ref[j]
    tb = x_ref.shape[0]

    def weight_copies(e, s):
        cols = pl.ds(pl.multiple_of(c * tc, tc), tc)
        return (pltpu.make_async_copy(wg_hbm.at[layer, e, :, cols], wbuf.at[s, 0], sem.at[s, 0]),
                pltpu.make_async_copy(wu_hbm.at[layer, e, :, cols], wbuf.at[s, 1], sem.at[s, 1]))

    @pl.when(j == 0)
    def _():
        slot_ref[0] = 0
        for cp in weight_copies(be_ref[0], 0):
            cp.start()

    @pl.when((nv > 0) & _expert_changed(be_ref, j))
    def _():
        s = slot_ref[0]
        for cp in weight_copies(be_ref[j], s):
            cp.wait()
        wg_s[...] = wbuf[s, 0].astype(BF16)
        wu_s[...] = wbuf[s, 1].astype(BF16)

        @pl.when(nx_ref[j] >= 0)
        def _():
            for cp in weight_copies(nx_ref[j], 1 - s):
                cp.start()

        slot_ref[0] = 1 - s

    def body(r):
        w = x_ref[0:r, :]
        lo = lax.bitcast_convert_type(w << 16, F32).astype(BF16)
        hi = lax.bitcast_convert_type(w & jnp.uint32(0xFFFF0000), F32).astype(BF16)

        def mm(w_s):
            return (jnp.dot(lo, w_s[0:HALF, :], preferred_element_type=F32)
                    + jnp.dot(hi, w_s[HALF:, :], preferred_element_type=F32))

        gt, up = mm(wg_s), mm(wu_s)
        h_ref[0:r, :] = (gt * jax.nn.sigmoid(gt) * up).astype(BF16)
        if r < tb:
            h_ref[r:tb, :] = jnp.zeros((tb - r, h_ref.shape[1]), BF16)

    _for_row_count(nv, tb, body, h_ref)


def _pack_halves(y):
    yb = y.astype(BF16).astype(F32)
    return (_bits(yb[:, HALF:]) & jnp.uint32(0xFFFF0000)) | (_bits(yb[:, :HALF]) >> 16)


def _unpack_halves(w):
    return (lax.bitcast_convert_type(w << 16, F32), lax.bitcast_convert_type(w & jnp.uint32(0xFFFF0000), F32))


def _moe_down_kernel(be_ref, bs_ref, nv_ref, nx_ref, h_ref, wd_hbm, y_ref, wbuf, sem, wd_s, slot_ref, *, layer):
    j = pl.program_id(0)
    nv = nv_ref[j]
    tb = h_ref.shape[0]

    def weight_copy(e, s):
        return pltpu.make_async_copy(wd_hbm.at[layer, e], wbuf.at[s], sem.at[s])

    @pl.when(j == 0)
    def _():
        slot_ref[0] = 0
        weight_copy(be_ref[0], 0).start()

    @pl.when((nv > 0) & _expert_changed(be_ref, j))
    def _():
        s = slot_ref[0]
        weight_copy(be_ref[j], s).wait()
        wd_s[...] = wbuf[s].astype(BF16)

        @pl.when(nx_ref[j] >= 0)
        def _():
            weight_copy(nx_ref[j], 1 - s).start()

        slot_ref[0] = 1 - s

    def body(r):
        y_ref[0:r, :] = _pack_halves(jnp.dot(h_ref[0:r, :], wd_s[...], preferred_element_type=F32))
        if r < tb:
            y_ref[r:tb, :] = jnp.zeros((tb - r, y_ref.shape[1]), U32)

    _for_row_count(nv, tb, body, y_ref)


def moe_experts(xb, blk_e, blk_src, n_valid, nxt_e, w_gate, w_up, w_down, layer, tb=MOE_TB, tc=512):
    rows = xb.shape[0]
    n_blocks = rows // tb
    d, de = w_gate.shape[-2:]
    any_spec = pl.BlockSpec(memory_space=pl.ANY)
    h = pl.pallas_call(
        functools.partial(_moe_up_kernel, layer=layer, tc=tc),
        out_shape=jax.ShapeDtypeStruct((rows, de), BF16),
        grid_spec=pltpu.PrefetchScalarGridSpec(
            num_scalar_prefetch=4,
            grid=(de // tc, n_blocks),
            in_specs=[pl.BlockSpec((tb, HALF), lambda c, j, be, bs, nv, nx: (bs[j], 0)), any_spec, any_spec],
            out_specs=pl.BlockSpec((tb, tc), lambda c, j, be, bs, nv, nx: (j, c)),
            scratch_shapes=[pltpu.VMEM((2, 2, d, tc), F32), pltpu.SemaphoreType.DMA((2, 2)),
                            pltpu.VMEM((d, tc), BF16), pltpu.VMEM((d, tc), BF16), pltpu.SMEM((1,), I32)],
        ),
        compiler_params=_cparams(("arbitrary", "arbitrary")),
        name="moe_up",
    )(blk_e, blk_src, n_valid, nxt_e, xb, w_gate, w_up)
    return pl.pallas_call(
        functools.partial(_moe_down_kernel, layer=layer),
        out_shape=jax.ShapeDtypeStruct((rows, HALF), U32),
        grid_spec=pltpu.PrefetchScalarGridSpec(
            num_scalar_prefetch=4,
            grid=(n_blocks,),
            in_specs=[pl.BlockSpec((tb, de), lambda j, be, bs, nv, nx: (bs[j], 0)), any_spec],
            out_specs=pl.BlockSpec((tb, HALF), lambda j, be, bs, nv, nx: (j, 0)),
            scratch_shapes=[pltpu.VMEM((2, de, d), F32), pltpu.SemaphoreType.DMA((2,)),
                            pltpu.VMEM((de, d), BF16), pltpu.SMEM((1,), I32)],
        ),
        compiler_params=_cparams(("arbitrary",)),
        name="moe_down",
    )(blk_e, blk_src, n_valid, nxt_e, h, w_down)


def _moe_combine_kernel(x_ref, y0_ref, y1_ref, gt_ref, gate_ref, o_ref):
    g0, g1 = gt_ref[:, 0:1], gt_ref[:, 1:2]
    for half, (a0, a1) in enumerate(zip(_unpack_halves(y0_ref[...]), _unpack_halves(y1_ref[...]))):
        cs = slice(half * HALF, (half + 1) * HALF)
        o_ref[:, cs] = x_ref[:, cs] + gate_ref[:, cs] * (g0 * a0 + g1 * a1)


def moe_combine(x, y0, y1, gates, mod, nc, tl, tm=512):
    n, d = x.shape
    grp = lambda i: _group_of_tile(i * tm, nc, tl)
    blk = lambda w: pl.BlockSpec((tm, w), lambda i: (i, 0))
    return pl.pallas_call(
        _moe_combine_kernel,
        out_shape=jax.ShapeDtypeStruct((n, d), F32),
        grid=(n // tm,),
        in_specs=[blk(d), blk(HALF), blk(HALF), blk(LANES),
                  pl.BlockSpec((None, None, 1, d), lambda i: (grp(i), 5, 0, 0))],
        out_specs=blk(d),
        compiler_params=_cparams(("arbitrary",)),
        name="moe_combine",
    )(x, y0, y1, gates, mod)


def moe_layout(eid, tb=MOE_TB):
    n = eid.shape[0]
    s_len = n * TOP_K
    flat_e = eid.reshape(s_len)
    onehot = (flat_e[:, None] == jnp.arange(N_EXPERTS, dtype=I32)[None, :]).astype(I32)
    rank = jnp.sum((jnp.cumsum(onehot, axis=0) - onehot) * onehot, axis=1)
    counts = jnp.sum(onehot, axis=0)
    nblk_e = (counts + tb - 1) // tb
    blk_end = jnp.cumsum(nblk_e)
    blk_start = blk_end - nblk_e
    dest = jnp.sum(onehot * blk_start[None, :], axis=1) * tb + rank
    n_blocks = s_len // tb + N_EXPERTS
    n_used = blk_end[-1]
    jc = jnp.minimum(jnp.arange(n_blocks, dtype=I32), n_used - 1)
    blk_e = jnp.minimum(jnp.sum((blk_end[None, :] <= jc[:, None]).astype(I32), axis=1), N_EXPERTS - 1).astype(I32)
    used = jnp.arange(n_blocks, dtype=I32) < n_used
    nxt_blk = blk_end[blk_e]
    nxt_e = jnp.where(used & (nxt_blk < n_used), blk_e[jnp.minimum(nxt_blk, n_blocks - 1)], -1).astype(I32)
    n_valid = jnp.where(used, jnp.clip(counts[blk_e] - (jc - blk_start[blk_e]) * tb, 0, tb), 0).astype(I32)
    pad_e = nblk_e * tb - counts
    pad_incl = jnp.cumsum(pad_e)
    n_fill = n_blocks * tb - s_len
    k = jnp.arange(n_fill, dtype=I32)
    e_k = jnp.minimum(jnp.sum((pad_incl[None, :] <= k[:, None]).astype(I32), axis=1), N_EXPERTS - 1)
    oh_k = (e_k[:, None] == jnp.arange(N_EXPERTS, dtype=I32)[None, :]).astype(I32)
    pick = lambda v: jnp.sum(oh_k * v[None, :], axis=1)
    in_used = k < pad_incl[-1]
    fill_pos = jnp.where(in_used, pick(blk_start * tb + counts) + k - pick(pad_incl - pad_e),
                         n_used * tb + k - pad_incl[-1])
    keys = jnp.concatenate([dest, fill_pos])
    vals = jnp.concatenate([jnp.arange(s_len, dtype=I32) // TOP_K, k % n])
    slot_tok = lax.sort_key_val(keys, vals)[1]
    return dest.reshape(n, TOP_K), slot_tok, blk_e, jc, n_valid, nxt_e


def _take_rows(a, idx):
    return a.at[idx].get(mode="promise_in_bounds")


def _final_norm_kernel(x_ref, g_ref, o_ref):
    x = x_ref[...]
    o_ref[...] = x * lax.rsqrt(jnp.mean(x * x, axis=-1, keepdims=True) + NORM_EPS) * g_ref[...]


def final_norm(x, g, tm=512):
    n, d = x.shape
    return pl.pallas_call(
        _final_norm_kernel,
        out_shape=jax.ShapeDtypeStruct((n, d), F32),
        grid=(n // tm,),
        in_specs=[pl.BlockSpec((tm, d), lambda i: (i, 0)), pl.BlockSpec((1, d), lambda i: (0, 0))],
        out_specs=pl.BlockSpec((tm, d), lambda i: (i, 0)),
        compiler_params=_cparams(("arbitrary",)),
        name="final_norm",
    )(x, g.reshape(1, d))


def _pad_lanes(a):
    return jnp.pad(a, ((0, 0), (0, LANES - a.shape[1])))


def kernel(x_prompt, x_sample, cache_k, cache_v, state_mlstm_C, state_mlstm_n, state_mlstm_m, state_rglru_h, c, c_ctx, ada_w, ada_b, norm1_g, w_in, attn_sink, mlstm_gate_b, mlstm_norm_g, rg_conv_w, rg_conv_b, rg_wr, rg_br, rg_wi, rg_bi, rg_lam, w_out, norm2_g, router_wg, router_bg, router_we, router_be, exp_w_gate, exp_w_up, exp_w_down, final_norm_g):
    bc, tc, d = x_prompt.shape
    bl, tl, _ = x_sample.shape
    depth = w_in.shape[0]
    past = cache_k.shape[2]
    nc, nl = bc * tc, bl * tl
    tm = math.gcd(1024, math.gcd(tl, nc))

    x = jnp.concatenate([x_prompt.reshape(nc, d), x_sample.reshape(nl, d)], axis=0)
    cvec = jnp.concatenate([c_ctx[None, :], c, jnp.zeros((SUBLANES - 1 - bl, d), F32)], axis=0)
    mods = adaln_all(cvec, ada_w, ada_b).reshape(depth, SUBLANES, 6, 1, d)
    rope_c, rope_s = rope_tables(tl)
    gsplit = Z_MAIN - 2 * R_WIDTH
    w_main = jnp.concatenate([w_in[:, :, :gsplit], w_in[:, :, gsplit + M_GATES:]], axis=2).astype(BF16)
    w_gcol = jnp.pad(w_in[:, :, gsplit:gsplit + M_GATES], ((0, 0), (0, 0), (0, LANES - M_GATES))).astype(BF16)
    w_out_b = w_out.astype(BF16)
    ck = cache_k.reshape(bl, depth, past, KV_WIDTH)
    cv = cache_v.reshape(bl, depth, past, KV_WIDTH)
    cn0 = jnp.concatenate([state_mlstm_C, jnp.broadcast_to(state_mlstm_n[..., None], state_mlstm_C.shape)], axis=-1)

    new_k, new_v, new_c, new_n, new_m, new_h = [], [], [], [], [], []
    for l in range(depth):
        mod = mods[l]
        z, zg = in_proj(x, norm1_g[l], mod, w_main[l], w_gcol[l], nc, tl, tm=tm)

        a_ctx = attention_context(z, attn_sink[l], bc, tc)
        a_lat = attention_latent(z, ck, cv, l, attn_sink[l], rope_c, rope_s, nc, bl, tl)

        rows, tiles, kt = mlstm_gates(z, zg, _pad_lanes(mlstm_gate_b[l].reshape(1, M_GATES)), tile=tm)
        res = mlstm_scan(z, rows, tiles, kt, cn0, state_mlstm_m, l, bc, tc, bl, tl)
        mh = res[0:2]
        for dr in range(2):
            cnf, mf = res[2 + 2 * dr:4 + 2 * dr]
            new_c.append(cnf[:bc, :, :, :M_DK])
            new_n.append(cnf[:bc, :, :, M_DK])
            new_m.append(mf[:bc, :M_HEADS, 0])
        rh = []
        for dr in range(2):
            rd, hf = rglru_direction(z, rg_conv_w, rg_conv_b, rg_wr, rg_br, rg_wi, rg_bi, rg_lam, state_rglru_h,
                                     l, dr, bc, tc, bl, tl)
            rh.append(rd)
            new_h.append(hf[:bc, 0])
        new_k.append(z[:nc, _C_K:_C_K + KV_WIDTH].reshape(bc, tc, KV_HEADS, HEAD_DIM))
        new_v.append(z[:nc, _C_V:_C_V + KV_WIDTH].reshape(bc, tc, KV_HEADS, HEAD_DIM))

        x = out_proj(x, a_ctx, a_lat, mh[0], mh[1], rh[0], rh[1], z, mlstm_norm_g[l], mod, w_out_b[l], nc, tl)

        w_router = _pad_lanes(jnp.concatenate([router_wg[l], router_we[l]], axis=1))
        b_router = _pad_lanes(jnp.concatenate([router_bg[l], router_be[l]])[None, :])
        xp, route_g, route_e = moe_pre(x, norm2_g[l], mod, w_router, b_router, nc, tl)
        dest, slot_tok, blk_e, blk_src, n_valid, nxt_e = moe_layout(route_e[:, :TOP_K])
        yb = moe_experts(_take_rows(xp, slot_tok), blk_e, blk_src, n_valid, nxt_e, exp_w_gate, exp_w_up, exp_w_down, l)
        x = moe_combine(x, _take_rows(yb, dest[:, 0]), _take_rows(yb, dest[:, 1]), route_g, mod, nc, tl)

    y = final_norm(x, final_norm_g)
    stack2 = lambda parts: jnp.stack([jnp.stack(parts[2 * l:2 * l + 2], axis=1) for l in range(depth)], axis=1)
    return (y[:nc].reshape(bc, tc, d), y[nc:].reshape(bl, tl, d),
            jnp.stack(new_k, axis=1), jnp.stack(new_v, axis=1),
            stack2(new_c), stack2(new_n), stack2(new_m), stack2(new_h))
```

```python
import functools
import math

import jax
import jax.numpy as jnp
from jax import lax
from jax.experimental import pallas as pl
from jax.experimental.pallas import tpu as pltpu

F32 = jnp.float32
BF16 = jnp.bfloat16
U32 = jnp.uint32
I32 = jnp.int32

D_MODEL = 2048
HEAD_DIM = 128
A_WIDTH = D_MODEL // 2
N_HEADS = A_WIDTH // HEAD_DIM
KV_HEADS = 2
GQA_GROUP = N_HEADS // KV_HEADS
KV_WIDTH = KV_HEADS * HEAD_DIM
WINDOW = 128
Q_BLOCK = 128
GRID_W = 64
ROPE_BASE = 10000.0
ROPE_AXIS = HEAD_DIM // 2
ATTN_SCALE = HEAD_DIM ** -0.5
NEG_INF = -1e30
M_WIDTH = D_MODEL // 4
M_HEADS = 4
M_DK = M_WIDTH // M_HEADS
M_GATES = 2 * 2 * M_HEADS
R_WIDTH = D_MODEL // 4
R_BLOCKS = 4
R_BW = R_WIDTH // R_BLOCKS
CONV_W = 4
RG_C = 8.0
N_GROUPS = 4
EXPERTS_PER_GROUP = 8
N_EXPERTS = N_GROUPS * EXPERTS_PER_GROUP
TOP_K = 2
D_EXPERT = D_MODEL // 2
NORM_EPS = 1e-6

LANES = 128
SUBLANES = 8
Z_MAIN = A_WIDTH + 2 * KV_WIDTH + 4 * M_WIDTH + 2 * R_WIDTH
_C_Q, _C_K, _C_V = 0, A_WIDTH, A_WIDTH + KV_WIDTH
_C_QM = A_WIDTH + 2 * KV_WIDTH
_C_KM, _C_VM, _C_OM = _C_QM + M_WIDTH, _C_QM + 2 * M_WIDTH, _C_QM + 3 * M_WIDTH
_C_XR, _C_YR = _C_QM + 4 * M_WIDTH, _C_QM + 4 * M_WIDTH + R_WIDTH

M_CHUNK = LANES
R_CHUNK = 256
MOE_TB = 512
VMEM_LIMIT = 56 * 1024 * 1024
HALF = D_MODEL // 2


def _cparams(sem):
    return pltpu.CompilerParams(dimension_semantics=sem, vmem_limit_bytes=VMEM_LIMIT)


def _group_of_tile(row0, nc, tl):
    return jnp.where(row0 < nc, 0, 1 + (jnp.maximum(row0 - nc, 0)) // tl)


def _adaln_kernel(c_ref, w_ref, b_ref, o_ref):
    c = c_ref[...]
    s = (c * jax.nn.sigmoid(c)).astype(BF16)
    o_ref[...] = jnp.dot(s, w_ref[...].astype(BF16), preferred_element_type=F32) + b_ref[...]


def adaln_all(cvec, ada_w, ada_b, tn=1024):
    depth, d, d6 = ada_w.shape
    return pl.pallas_call(
        _adaln_kernel,
        out_shape=jax.ShapeDtypeStruct((depth, SUBLANES, d6), F32),
        grid=(depth, d6 // tn),
        in_specs=[
            pl.BlockSpec((SUBLANES, d), lambda l, j: (0, 0)),
            pl.BlockSpec((None, d, tn), lambda l, j: (l, 0, j)),
            pl.BlockSpec((None, 1, tn), lambda l, j: (l, 0, j)),
        ],
        out_specs=pl.BlockSpec((None, SUBLANES, tn), lambda l, j: (l, 0, j)),
        compiler_params=_cparams(("arbitrary", "arbitrary")),
        name="adaln",
    )(cvec, ada_w, ada_b.reshape(depth, 1, d6))


def _combine_into(o_ref, x_ref, y0_ref, y1_ref, gt_ref, gate_ref):
    g0, g1 = gt_ref[:, 0:1], gt_ref[:, 1:2]
    for half, (a0, a1) in enumerate(zip(_unpack_halves(y0_ref[...]), _unpack_halves(y1_ref[...]))):
        cs = slice(half * HALF, (half + 1) * HALF)
        o_ref[:, cs] = x_ref[:, cs] + gate_ref[:, cs] * (g0 * a0 + g1 * a1)


def _in_kernel(*refs, combine):
    if combine:
        x_ref, y0_ref, y1_ref, gt_ref, g2_ref, g_ref, sh_ref, sc_ref, w_ref, wg_ref, z_ref, zg_ref, xo_ref, xn_ref = refs
    else:
        x_ref, g_ref, sh_ref, sc_ref, w_ref, wg_ref, z_ref, zg_ref, xn_ref = refs

    @pl.when(pl.program_id(1) == 0)
    def _():
        if combine:
            _combine_into(xo_ref, x_ref, y0_ref, y1_ref, gt_ref, g2_ref)
            x = xo_ref[...]
        else:
            x = x_ref[...]
        y = x * lax.rsqrt(jnp.mean(x * x, axis=-1, keepdims=True) + NORM_EPS) * g_ref[...]
        xn = (y * (1.0 + sc_ref[...]) + sh_ref[...]).astype(BF16)
        xn_ref[...] = xn
        zg_ref[...] = jnp.dot(xn, wg_ref[...], preferred_element_type=F32)

    z_ref[...] = jnp.dot(xn_ref[...], w_ref[...], preferred_element_type=F32)


def in_proj(x, pend, norm_g, mod, w_main, w_gate, nc, tl, tm, tn=768):
    n, d = x.shape
    zw = w_main.shape[1]
    grp = lambda i: _group_of_tile(i * tm, nc, tl)
    row = lambda w: pl.BlockSpec((tm, w), lambda i, j: (i, 0))
    modrow = lambda k: pl.BlockSpec((None, None, 1, d), lambda i, j: (grp(i), k, 0, 0))
    in_specs, args = [row(d)], [x]
    out_shape = [jax.ShapeDtypeStruct((n, zw), F32), jax.ShapeDtypeStruct((n, LANES), F32)]
    out_specs = [pl.BlockSpec((tm, tn), lambda i, j: (i, j)), row(LANES)]
    if pend is not None:
        y0, y1, gates, mod_prev = pend
        in_specs += [row(HALF), row(HALF), row(LANES), modrow(5)]
        args += [y0, y1, gates, mod_prev]
        out_shape.append(jax.ShapeDtypeStruct((n, d), F32))
        out_specs.append(row(d))
    in_specs += [pl.BlockSpec((1, d), lambda i, j: (0, 0)), modrow(0), modrow(1),
                 pl.BlockSpec((d, tn), lambda i, j: (0, j)), pl.BlockSpec((d, LANES), lambda i, j: (0, 0))]
    args += [norm_g.reshape(1, d), mod, mod, w_main, w_gate]
    return pl.pallas_call(
        functools.partial(_in_kernel, combine=pend is not None),
        out_shape=tuple(out_shape),
        grid=(n // tm, zw // tn),
        in_specs=in_specs,
        out_specs=tuple(out_specs),
        scratch_shapes=[pltpu.VMEM((tm, d), BF16)],
        compiler_params=_cparams(("arbitrary", "arbitrary")),
        name="in_proj",
    )(*args)


def _rope(x, c, s):
    lane = lax.broadcasted_iota(I32, x.shape, 1)
    half = ROPE_AXIS // 2
    partner = jnp.where((lane & (ROPE_AXIS - 1)) < half, pltpu.roll(x, HEAD_DIM - half, 1), pltpu.roll(x, half, 1))
    return x * c + partner * s


def _softmax_pv(parts, sink_col):
    m = sink_col
    for s, _ in parts:
        m = jnp.maximum(m, jnp.max(s, axis=-1, keepdims=True))
    den = jnp.exp(sink_col - m)
    acc = None
    for s, v in parts:
        p = jnp.exp(s - m)
        den = den + jnp.sum(p, axis=-1, keepdims=True)
        pv = jnp.dot(p.astype(BF16), v, preferred_element_type=F32)
        acc = pv if acc is None else acc + pv
    return acc / den


def _qk(q, k):
    return lax.dot_general(q, k, (((1,), (1,)), ((), ())), preferred_element_type=F32) * ATTN_SCALE


def _sink_column(sink_ref, n, rows):
    ridx = lax.broadcasted_iota(I32, (GQA_GROUP * rows, 1), 0)
    col = jnp.full((GQA_GROUP * rows, 1), sink_ref[n * GQA_GROUP], F32)
    for g in range(1, GQA_GROUP):
        col = jnp.where(ridx >= g * rows, sink_ref[n * GQA_GROUP + g], col)
    return col


def _attn_ctx_kernel(sink_ref, q_ref, k_ref, v_ref, o_ref):
    t = q_ref.shape[0]
    for n in range(KV_HEADS):
        k = k_ref[:, n * HEAD_DIM:(n + 1) * HEAD_DIM].astype(BF16)
        v = v_ref[:, n * HEAD_DIM:(n + 1) * HEAD_DIM].astype(BF16)
        q = jnp.concatenate(
            [q_ref[:, (n * GQA_GROUP + g) * HEAD_DIM:(n * GQA_GROUP + g + 1) * HEAD_DIM] for g in range(GQA_GROUP)],
            axis=0).astype(BF16)
        out = _softmax_pv([(_qk(q, k), v)], _sink_column(sink_ref, n, t))
        for g in range(GQA_GROUP):
            h = n * GQA_GROUP + g
            o_ref[:, h * HEAD_DIM:(h + 1) * HEAD_DIM] = out[g * t:(g + 1) * t].astype(o_ref.dtype)


def attention_context(z, sink, b, t):
    return pl.pallas_call(
        _attn_ctx_kernel,
        out_shape=jax.ShapeDtypeStruct((b * t, A_WIDTH), BF16),
        grid=(b,),
        in_specs=[
            pl.BlockSpec(memory_space=pltpu.SMEM),
            pl.BlockSpec((t, A_WIDTH), lambda i: (i, _C_Q // A_WIDTH)),
            pl.BlockSpec((t, KV_WIDTH), lambda i: (i, _C_K // KV_WIDTH)),
            pl.BlockSpec((t, KV_WIDTH), lambda i: (i, _C_V // KV_WIDTH)),
        ],
        out_specs=pl.BlockSpec((t, A_WIDTH), lambda i: (i, 0)),
        compiler_params=_cparams(("arbitrary",)),
        name="attn_ctx",
    )(sink, z, z, z)


def _attn_lat_kernel(sink_ref, q_ref, k_ref, v_ref, kc_ref, vc_ref, cq_ref, sq_ref, ck_ref, sk_ref, o_ref,
                     kr_ref, vp_ref, *, t):
    i = pl.program_id(1)
    rope_rows = 512

    @pl.when(i == 0)
    def _():
        zpad = jnp.zeros((WINDOW, KV_WIDTH), BF16)
        kr_ref[0:WINDOW, :] = zpad
        kr_ref[WINDOW + t:2 * WINDOW + t, :] = zpad
        vp_ref[0:WINDOW, :] = zpad
        vp_ref[WINDOW + t:2 * WINDOW + t, :] = zpad

        def body(c, carry):
            r0 = pl.multiple_of(c * rope_rows, rope_rows)
            cs, sn = ck_ref[pl.ds(r0, rope_rows), :], sk_ref[pl.ds(r0, rope_rows), :]
            for n in range(KV_HEADS):
                kk = k_ref[pl.ds(r0, rope_rows), n * HEAD_DIM:(n + 1) * HEAD_DIM]
                kr_ref[pl.ds(WINDOW + r0, rope_rows), n * HEAD_DIM:(n + 1) * HEAD_DIM] = _rope(kk, cs, sn).astype(BF16)
            vp_ref[pl.ds(WINDOW + r0, rope_rows), :] = v_ref[pl.ds(r0, rope_rows), :].astype(BF16)
            return carry

        lax.fori_loop(0, t // rope_rows, body, 0)

    span = Q_BLOCK + 2 * WINDOW
    rows = GQA_GROUP * Q_BLOCK
    r = lax.broadcasted_iota(I32, (rows, span), 0) & (Q_BLOCK - 1)
    c = lax.broadcasted_iota(I32, (rows, span), 1)
    kpos = (i - 1) * Q_BLOCK + c
    mask = (c >= r) & (c <= r + 2 * WINDOW) & (kpos >= 0) & (kpos < t)
    w0 = pl.multiple_of(i * Q_BLOCK, Q_BLOCK)
    cq, sq = cq_ref[...], sq_ref[...]
    for n in range(KV_HEADS):
        hs = slice(n * HEAD_DIM, (n + 1) * HEAD_DIM)
        q = jnp.concatenate(
            [_rope(q_ref[:, (n * GQA_GROUP + g) * HEAD_DIM:(n * GQA_GROUP + g + 1) * HEAD_DIM], cq, sq)
             for g in range(GQA_GROUP)], axis=0).astype(BF16)
        s_win = jnp.where(mask, _qk(q, kr_ref[pl.ds(w0, span), hs]), NEG_INF)
        s_ctx = _qk(q, kc_ref[:, hs].astype(BF16))
        out = _softmax_pv([(s_win, vp_ref[pl.ds(w0, span), hs]), (s_ctx, vc_ref[:, hs].astype(BF16))],
                          _sink_column(sink_ref, n, Q_BLOCK))
        for g in range(GQA_GROUP):
            h = n * GQA_GROUP + g
            o_ref[:, h * HEAD_DIM:(h + 1) * HEAD_DIM] = out[g * Q_BLOCK:(g + 1) * Q_BLOCK].astype(o_ref.dtype)


def attention_latent(z, cache_k, cache_v, layer, sink, rope_c, rope_s, row0, b, t):
    nqb = t // Q_BLOCK
    past = cache_k.shape[2]
    qb0 = row0 // Q_BLOCK
    tb0 = row0 // t
    kern = functools.partial(_attn_lat_kernel, t=t)
    return pl.pallas_call(
        kern,
        out_shape=jax.ShapeDtypeStruct((b * t, A_WIDTH), BF16),
        grid=(b, nqb),
        in_specs=[
            pl.BlockSpec(memory_space=pltpu.SMEM),
            pl.BlockSpec((Q_BLOCK, A_WIDTH), lambda bi, i: (qb0 + bi * nqb + i, _C_Q // A_WIDTH)),
            pl.BlockSpec((t, KV_WIDTH), lambda bi, i: (tb0 + bi, _C_K // KV_WIDTH)),
            pl.BlockSpec((t, KV_WIDTH), lambda bi, i: (tb0 + bi, _C_V // KV_WIDTH)),
            pl.BlockSpec((None, None, past, KV_WIDTH), lambda bi, i: (bi, layer, 0, 0)),
            pl.BlockSpec((None, None, past, KV_WIDTH), lambda bi, i: (bi, layer, 0, 0)),
            pl.BlockSpec((Q_BLOCK, HEAD_DIM), lambda bi, i: (i, 0)),
            pl.BlockSpec((Q_BLOCK, HEAD_DIM), lambda bi, i: (i, 0)),
            pl.BlockSpec((t, HEAD_DIM), lambda bi, i: (0, 0)),
            pl.BlockSpec((t, HEAD_DIM), lambda bi, i: (0, 0)),
        ],
        out_specs=pl.BlockSpec((Q_BLOCK, A_WIDTH), lambda bi, i: (bi * nqb + i, 0)),
        scratch_shapes=[pltpu.VMEM((t + 2 * WINDOW, KV_WIDTH), BF16), pltpu.VMEM((t + 2 * WINDOW, KV_WIDTH), BF16)],
        compiler_params=_cparams(("arbitrary", "arbitrary")),
        name="attn_lat",
    )(sink, z, z, z, cache_k, cache_v, rope_c, rope_s, rope_c, rope_s)


def rope_tables(t):
    pos = jnp.arange(t)
    inv = ROPE_BASE ** (-jnp.arange(0, ROPE_AXIS, 2, dtype=F32) / ROPE_AXIS)

    def cs(p):
        ang = p.astype(F32)[:, None] * inv[None, :]
        return jnp.cos(ang), jnp.sin(ang)

    cr, sr = cs(pos // GRID_W)
    cc, sc = cs(pos % GRID_W)
    return (jnp.concatenate([cr, cr, cc, cc], axis=-1), jnp.concatenate([-sr, sr, -sc, sc], axis=-1))


class _Seqs:
    def __init__(self, n_ctx, t_ctx, n_lat, t_lat, chunk, reverse):
        self.n_ctx, self.t_ctx, self.n_lat, self.t_lat = n_ctx, t_ctx, n_lat, t_lat
        self.chunk, self.reverse = chunk, reverse
        self.cpc, self.cpl = t_ctx // chunk, t_lat // chunk
        self.ctx_chunks = n_ctx * self.cpc
        self.n_chunks = self.ctx_chunks + n_lat * self.cpl

    def chunk_of_step(self, s):
        return (self.n_chunks - 1 - s) if self.reverse else s

    def info(self, g):
        is_ctx = g < self.ctx_chunks
        gl = jnp.maximum(g - self.ctx_chunks, 0)
        gc = jnp.minimum(g, self.ctx_chunks - 1)
        pos = jnp.where(is_ctx, gc % self.cpc, gl % self.cpl)
        per = jnp.where(is_ctx, self.cpc, self.cpl)
        lat = gl // self.cpl
        seq = jnp.where(is_ctx, gc // self.cpc, self.n_ctx + lat)
        head, tail = pos == 0, pos == per - 1
        return is_ctx, seq, lat, (tail if self.reverse else head), (head if self.reverse else tail)


def _lane_scan(x, op, reverse):
    n = x.shape[1]
    lane = lax.broadcasted_iota(I32, x.shape, 1)
    s = 1
    while s < n:
        if reverse:
            x = jnp.where(lane < n - s, op(x, pltpu.roll(x, n - s, 1)), x)
        else:
            x = jnp.where(lane >= s, op(x, pltpu.roll(x, s, 1)), x)
        s *= 2
    return x


_GROWS = 2 * SUBLANES


def _mlstm_gate_kernel(g_ref, gb_ref, k_ref, rows_ref, tiles_ref, kt_ref):
    L = M_CHUNK
    nck = g_ref.shape[0] // L
    g = g_ref[...] + gb_ref[...]
    s = jnp.concatenate([g[c * L:(c + 1) * L].T[0:_GROWS] for c in range(nck)], axis=0)
    row = lax.broadcasted_iota(I32, s.shape, 0)
    is_rev = (row & SUBLANES) != 0
    lf = jax.nn.log_sigmoid(s)
    f = jnp.where(is_rev, _lane_scan(lf, jnp.add, True), _lane_scan(lf, jnp.add, False))
    f = pltpu.roll(f, s.shape[0] - M_HEADS, 0)
    r = s - f
    cm = jnp.where(is_rev, _lane_scan(r, jnp.maximum, True), _lane_scan(r, jnp.maximum, False))
    rows_ref[...] = r
    pad = jnp.zeros((LANES - 2 * _GROWS, L), F32)
    scale = M_DK ** -0.5
    for c in range(nck):
        sl = slice(c * _GROWS, (c + 1) * _GROWS)
        cols = jnp.concatenate([cm[sl], f[sl], pad], axis=0).T
        for d in range(2):
            for h in range(M_HEADS):
                gi = d * SUBLANES + h
                tiles_ref[c, d, h] = jnp.broadcast_to(cols[:, gi:gi + 1], (L, LANES))
                tiles_ref[c, d, M_HEADS + h] = jnp.broadcast_to(cols[:, _GROWS + gi:_GROWS + gi + 1], (L, LANES))
        for h in range(M_HEADS):
            kt_ref[c, h] = (k_ref[c * L:(c + 1) * L, h * M_DK:(h + 1) * M_DK] * scale).T


def mlstm_gates(z, zg, gate_b, tile=512):
    n = zg.shape[0]
    L = M_CHUNK
    nck = tile // L
    return pl.pallas_call(
        _mlstm_gate_kernel,
        out_shape=(jax.ShapeDtypeStruct((n // L * _GROWS, L), F32),
                   jax.ShapeDtypeStruct((n // L, 2, 2 * M_HEADS, L, LANES), F32),
                   jax.ShapeDtypeStruct((n // L, M_HEADS, M_DK, L), F32)),
        grid=(n // tile,),
        in_specs=[pl.BlockSpec((tile, LANES), lambda i: (i, 0)), pl.BlockSpec((1, LANES), lambda i: (0, 0)),
                  pl.BlockSpec((tile, M_WIDTH), lambda i: (i, _C_KM // M_WIDTH))],
        out_specs=(pl.BlockSpec((nck * _GROWS, L), lambda i: (i, 0)),
                   pl.BlockSpec((nck, 2, 2 * M_HEADS, L, LANES), lambda i: (i, 0, 0, 0, 0)),
                   pl.BlockSpec((nck, M_HEADS, M_DK, L), lambda i: (i, 0, 0, 0))),
        compiler_params=_cparams(("arbitrary",)),
        name="mlstm_gates",
    )(zg, gate_b, z)


def _mlstm_kernel(m0_ref, qf_ref, vf_ref, ktf_ref, rf_ref, tf_ref, qb_ref, vb_ref, ktb_ref, rb_ref, tb_ref,
                  cn0f_ref, cn0b_ref, hf_ref, hb_ref, ocnf_ref, omf_ref, ocnb_ref, omb_ref,
                  cn_s, m_s, *, seqs, layer, depth):
    L = M_CHUNK
    step = pl.program_id(0)
    ti = lax.broadcasted_iota(I32, (L, L), 0)
    si = lax.broadcasted_iota(I32, (L, L), 1)
    ones = jnp.ones((L, M_DK), BF16)
    streams = (
        (0, seqs[0], qf_ref, vf_ref, ktf_ref, rf_ref, tf_ref, cn0f_ref, hf_ref, ocnf_ref, omf_ref),
        (1, seqs[1], qb_ref, vb_ref, ktb_ref, rb_ref, tb_ref, cn0b_ref, hb_ref, ocnb_ref, omb_ref),
    )
    for d, sq, q_ref, v_ref, kt_ref, rows_ref, tiles_ref, cn0_ref, h_ref, ocn_ref, om_ref in streams:
        rev = d == 1
        is_ctx, _, lat, first, last = sq.info(sq.chunk_of_step(step))

        @pl.when(first)
        def _(d=d, is_ctx=is_ctx, lat=lat, cn0_ref=cn0_ref):
            keep = jnp.where(is_ctx, 0.0, 1.0).astype(F32)
            cn_s[d] = cn0_ref[...] * keep
            m_s[d] = jnp.zeros((SUBLANES, LANES), F32)
            for h in range(M_HEADS):
                m0 = m0_ref[((lat * depth + layer) * 2 + d) * M_HEADS + h]
                m_s[d, h:h + 1, :] = jnp.full((1, LANES), m0, F32) * keep

        end = 0 if rev else L - 1
        causal = (si >= ti) if rev else (si <= ti)
        for h in range(M_HEADS):
            hs = slice(h * M_DK, (h + 1) * M_DK)
            gi = d * SUBLANES + h
            q = q_ref[:, hs].astype(BF16)
            vaug = jnp.concatenate([v_ref[:, hs].astype(BF16), ones], axis=1)
            kt = kt_ref[h]
            r_row = rows_ref[gi:gi + 1, :]
            f_b = tiles_ref[M_HEADS + h]
            m_old = m_s[d, h:h + 1, :]
            m_b = jnp.maximum(m_old, tiles_ref[h])
            dmat = jnp.where(causal, jnp.exp(jnp.where(causal, r_row - m_b, 0.0)), 0.0)
            s = jnp.dot(q, kt.astype(BF16), preferred_element_type=F32) * dmat
            wpq = (jnp.exp(m_old - m_b) * q.astype(F32)).astype(BF16)
            cn_old = cn_s[d, h]
            out = jnp.dot(jnp.concatenate([s.astype(BF16), wpq], axis=1),
                          jnp.concatenate([vaug, cn_old.astype(BF16)], axis=0), preferred_element_type=F32)
            h_ref[:, hs] = out[:, :M_DK] / jnp.maximum(jnp.abs(out[:, M_DK:]), jnp.exp(-(f_b + m_b)))
            m_end = m_b[end:end + 1, :]
            dec = jnp.exp(m_old - m_end)
            kwt = (kt * jnp.exp(r_row - m_end)).astype(BF16)
            cn_s[d, h] = jnp.concatenate([dec, dec], axis=1) * cn_old + jnp.dot(kwt, vaug, preferred_element_type=F32)
            m_s[d, h:h + 1, :] = f_b[end:end + 1, :] + m_end

        @pl.when(last)
        def _(d=d, ocn_ref=ocn_ref, om_ref=om_ref):
            ocn_ref[...] = cn_s[d]
            om_ref[...] = m_s[d]


def mlstm_scan(z, rows, tiles, kt, cn0, st_m, layer, n_ctx, t_ctx, n_lat, t_lat):
    L = M_CHUNK
    seqs = (_Seqs(n_ctx, t_ctx, n_lat, t_lat, L, False), _Seqs(n_ctx, t_ctx, n_lat, t_lat, L, True))
    n_seq = n_ctx + n_lat
    n_chunks = seqs[0].n_chunks
    kern = functools.partial(_mlstm_kernel, seqs=seqs, layer=layer, depth=st_m.shape[1])
    in_specs = [pl.BlockSpec(memory_space=pltpu.SMEM)]
    for d, sq in enumerate(seqs):
        cm = lambda s, sq=sq: sq.chunk_of_step(s)
        in_specs += [
            pl.BlockSpec((L, M_WIDTH), lambda s, cm=cm: (cm(s), _C_QM // M_WIDTH)),
            pl.BlockSpec((L, M_WIDTH), lambda s, cm=cm: (cm(s), _C_VM // M_WIDTH)),
            pl.BlockSpec((None, M_HEADS, M_DK, L), lambda s, cm=cm: (cm(s), 0, 0, 0)),
            pl.BlockSpec((_GROWS, L), lambda s, cm=cm: (cm(s), 0)),
            pl.BlockSpec((None, None, 2 * M_HEADS, L, LANES), lambda s, cm=cm, d=d: (cm(s), d, 0, 0, 0)),
        ]
    for d, sq in enumerate(seqs):
        lat_of = lambda s, sq=sq: sq.info(sq.chunk_of_step(s))[2]
        in_specs.append(pl.BlockSpec((None, None, None, M_HEADS, M_DK, 2 * M_DK),
                                     lambda s, f=lat_of, d=d: (f(s), layer, d, 0, 0, 0)))
    out_shape = [jax.ShapeDtypeStruct((n_chunks * L, M_WIDTH), F32)] * 2
    out_specs = [pl.BlockSpec((L, M_WIDTH), lambda s, sq=sq: (sq.chunk_of_step(s), 0)) for sq in seqs]
    for sq in seqs:
        seq_of = lambda s, sq=sq: sq.info(sq.chunk_of_step(s))[1]
        out_shape += [jax.ShapeDtypeStruct((n_seq, M_HEADS, M_DK, 2 * M_DK), F32),
                      jax.ShapeDtypeStruct((n_seq, SUBLANES, LANES), F32)]
        out_specs += [pl.BlockSpec((None, M_HEADS, M_DK, 2 * M_DK), lambda s, f=seq_of: (f(s), 0, 0, 0)),
                      pl.BlockSpec((None, SUBLANES, LANES), lambda s, f=seq_of: (f(s), 0, 0))]
    args = [st_m.reshape(-1), z, z, kt, rows, tiles, z, z, kt, rows, tiles, cn0, cn0]
    return pl.pallas_call(
        kern,
        out_shape=tuple(out_shape),
        grid=(n_chunks,),
        in_specs=in_specs,
        out_specs=tuple(out_specs),
        scratch_shapes=[pltpu.VMEM((2, M_HEADS, M_DK, 2 * M_DK), F32), pltpu.VMEM((2, SUBLANES, LANES), F32)],
        compiler_params=_cparams(("arbitrary",)),
        name="mlstm_scan",
    )(*args)


def _rglru_kernel(x_ref, xp_ref, xn_ref, cw_ref, cb_ref, wr_ref, br_ref, wi_ref, bi_ref, lam_ref, h0_ref,
                  h_ref, hf_ref, xpad, carry, *, seqs):
    L = seqs.chunk
    rev = seqs.reverse
    g = seqs.chunk_of_step(pl.program_id(0))
    is_ctx, _, _, first, last = seqs.info(g)
    head = last if rev else first
    tail = first if rev else last
    halo = SUBLANES
    xpad[0:halo, :] = xp_ref[...] * jnp.where(head, 0.0, 1.0).astype(F32)
    xpad[halo:halo + L, :] = x_ref[...]
    xpad[halo + L:2 * halo + L, :] = xn_ref[...] * jnp.where(tail, 0.0, 1.0).astype(F32)
    x = cb_ref[...]
    for j in range(CONV_W):
        x = x + cw_ref[j:j + 1, :] * xpad[halo - 2 + j:halo - 2 + j + L, :]

    rs, is_ = [], []
    for n in range(R_BLOCKS):
        xb = x[:, n * R_BW:(n + 1) * R_BW].astype(BF16)
        rs.append(jnp.dot(xb, wr_ref[n].astype(BF16), preferred_element_type=F32))
        is_.append(jnp.dot(xb, wi_ref[n].astype(BF16), preferred_element_type=F32))
    rg = jax.nn.sigmoid(jnp.concatenate(rs, axis=-1) + br_ref[...])
    ig = jax.nn.sigmoid(jnp.concatenate(is_, axis=-1) + bi_ref[...])
    log_a = -RG_C * rg * jax.nn.softplus(-lam_ref[...])
    a = jnp.exp(log_a)
    u = jnp.sqrt(1.0 - jnp.exp(2.0 * log_a)) * (ig * x)

    sub = lax.broadcasted_iota(I32, (L, R_WIDTH), 0) & (SUBLANES - 1)
    s = 1
    while s < SUBLANES:
        if rev:
            ok = sub < SUBLANES - s
            a_sh, u_sh = pltpu.roll(a, L - s, 0), pltpu.roll(u, L - s, 0)
        else:
            ok = sub >= s
            a_sh, u_sh = pltpu.roll(a, s, 0), pltpu.roll(u, s, 0)
        u = jnp.where(ok, a * u_sh + u, u)
        a = jnp.where(ok, a * a_sh, a)
        s *= 2

    @pl.when(first)
    def _():
        carry[...] = h0_ref[...] * jnp.where(is_ctx, 0.0, 1.0).astype(F32)

    n_grp = L // SUBLANES
    edge = 0 if rev else SUBLANES - 1
    hprev = carry[...]
    for k in range(n_grp):
        gidx = (n_grp - 1 - k) if rev else k
        rows = slice(gidx * SUBLANES, (gidx + 1) * SUBLANES)
        hg = a[rows] * hprev + u[rows]
        h_ref[rows, :] = hg
        hprev = hg[edge:edge + 1, :]
    carry[...] = hprev

    @pl.when(last)
    def _():
        hf_ref[...] = hprev


def rglru_direction(z, conv_w, conv_b, wr, br, wi, bi, lam, h0, layer, direction, n_ctx, t_ctx, n_lat, t_lat):
    seqs = _Seqs(n_ctx, t_ctx, n_lat, t_lat, R_CHUNK, direction == 1)
    n_seq = n_ctx + n_lat
    L = R_CHUNK
    n_rows = seqs.n_chunks * L
    hb = L // SUBLANES
    nb8 = n_rows // SUBLANES
    cm = lambda s: seqs.chunk_of_step(s)
    lat_of = lambda s: seqs.info(cm(s))[2]
    seq_of = lambda s: seqs.info(cm(s))[1]
    xcol = _C_XR // R_WIDTH
    vec_ld = lambda: pl.BlockSpec((None, None, 1, R_WIDTH), lambda s: (layer, direction, 0, 0))
    mat_ld = lambda: pl.BlockSpec((None, None, R_BLOCKS, R_BW, R_BW), lambda s: (layer, direction, 0, 0, 0))
    kern = functools.partial(_rglru_kernel, seqs=seqs)
    depth = conv_w.shape[0]
    r4 = lambda a: a.reshape(depth, 2, 1, R_WIDTH)
    return pl.pallas_call(
        kern,
        out_shape=(jax.ShapeDtypeStruct((n_rows, R_WIDTH), F32), jax.ShapeDtypeStruct((n_seq, 1, R_WIDTH), F32)),
        grid=(seqs.n_chunks,),
        in_specs=[
            pl.BlockSpec((L, R_WIDTH), lambda s: (cm(s), xcol)),
            pl.BlockSpec((SUBLANES, R_WIDTH), lambda s: (jnp.maximum(cm(s) * hb - 1, 0), xcol)),
            pl.BlockSpec((SUBLANES, R_WIDTH), lambda s: (jnp.minimum((cm(s) + 1) * hb, nb8 - 1), xcol)),
            pl.BlockSpec((None, CONV_W, R_WIDTH), lambda s: (layer, 0, 0)),
            pl.BlockSpec((None, 1, R_WIDTH), lambda s: (layer, 0, 0)),
            mat_ld(), vec_ld(), mat_ld(), vec_ld(), vec_ld(),
            pl.BlockSpec((None, None, None, 1, R_WIDTH), lambda s: (lat_of(s), layer, direction, 0, 0)),
        ],
        out_specs=(pl.BlockSpec((L, R_WIDTH), lambda s: (cm(s), 0)),
                   pl.BlockSpec((None, 1, R_WIDTH), lambda s: (seq_of(s), 0, 0))),
        scratch_shapes=[pltpu.VMEM((L + 2 * SUBLANES, R_WIDTH), F32), pltpu.VMEM((1, R_WIDTH), F32)],
        compiler_params=_cparams(("arbitrary",)),
        name=f"rglru_d{direction}",
    )(z, z, z, conv_w, conv_b.reshape(depth, 1, R_WIDTH), wr, r4(br), wi, r4(bi), r4(lam),
      h0.reshape(h0.shape[0], depth, 2, 1, R_WIDTH))


def _gelu_tanh(x):
    return 0.5 * x * (1.0 + jnp.tanh(math.sqrt(2.0 / math.pi) * (x + 0.044715 * (x * x * x))))


def _out_kernel(x_ref, ac_ref, al_ref, mf_ref, mb_ref, om_ref, rf_ref, rb_ref, yr_ref, mg_ref, gate_ref, w_ref,
                o_ref, *, ctx_tiles, sub):
    is_ctx = pl.program_id(0) < ctx_tiles
    for r0 in range(0, x_ref.shape[0], sub):
        rows = slice(r0, r0 + sub)
        att = jnp.where(is_ctx, ac_ref[rows, :], al_ref[rows, :])
        hs = mf_ref[rows, :] + mb_ref[rows, :]
        parts = [att]
        for h in range(M_HEADS):
            cs = slice(h * M_DK, (h + 1) * M_DK)
            hh = hs[:, cs]
            hn = hh * lax.rsqrt(jnp.mean(hh * hh, axis=-1, keepdims=True) + NORM_EPS) * mg_ref[:, cs]
            parts.append((hn * jax.nn.sigmoid(om_ref[rows, cs])).astype(BF16))
        parts.append(((rf_ref[rows, :] + rb_ref[rows, :]) * _gelu_tanh(yr_ref[rows, :])).astype(BF16))
        mix = jnp.concatenate(parts, axis=1)
        o_ref[rows, :] = x_ref[rows, :] + gate_ref[...] * jnp.dot(mix, w_ref[...], preferred_element_type=F32)


def out_proj(x, a_ctx, a_lat, mf, mb, rf, rb, z, mnorm_g, mod, w_out, nc, tl, tm=512, sub=256):
    n, d = x.shape
    grp = lambda i: _group_of_tile(i * tm, nc, tl)
    ctx_tiles = nc // tm
    lat_tiles = (n - nc) // tm
    rowblk = lambda w, col: pl.BlockSpec((tm, w), lambda i: (i, col))
    kern = functools.partial(_out_kernel, ctx_tiles=ctx_tiles, sub=sub)
    return pl.pallas_call(
        kern,
        out_shape=jax.ShapeDtypeStruct((n, d), F32),
        grid=(n // tm,),
        in_specs=[
            rowblk(d, 0),
            pl.BlockSpec((tm, A_WIDTH), lambda i: (jnp.minimum(i, ctx_tiles - 1), 0)),
            pl.BlockSpec((tm, A_WIDTH), lambda i: (jnp.clip(i - ctx_tiles, 0, lat_tiles - 1), 0)),
            rowblk(M_WIDTH, 0), rowblk(M_WIDTH, 0), rowblk(M_WIDTH, _C_OM // M_WIDTH),
            rowblk(R_WIDTH, 0), rowblk(R_WIDTH, 0), rowblk(R_WIDTH, _C_YR // R_WIDTH),
            pl.BlockSpec((1, M_WIDTH), lambda i: (0, 0)),
            pl.BlockSpec((None, None, 1, d), lambda i: (grp(i), 2, 0, 0)),
            pl.BlockSpec((d, d), lambda i: (0, 0), pipeline_mode=pl.Buffered(1)),
        ],
        out_specs=rowblk(d, 0),
        compiler_params=_cparams(("arbitrary",)),
        name="out_proj",
    )(x, a_ctx, a_lat, mf, mb, z, rf, rb, z, mnorm_g.reshape(1, -1), mod, w_out)


def _bits(x):
    return lax.bitcast_convert_type(x, U32)


def _moe_pre_kernel(x_ref, g_ref, sh_ref, sc_ref, wh_ref, wl_ref, br_ref, xp_ref, rg_ref, re_ref):
    x = x_ref[...]
    y = x * lax.rsqrt(jnp.mean(x * x, axis=-1, keepdims=True) + NORM_EPS) * g_ref[...]
    xn = y * (1.0 + sc_ref[...]) + sh_ref[...]
    xb = xn.astype(BF16)
    xb32 = xb.astype(F32)
    xp_ref[...] = (_bits(xb32[:, HALF:]) & jnp.uint32(0xFFFF0000)) | (_bits(xb32[:, :HALF]) >> 16)

    xl = (xn - xb32).astype(BF16)
    lg = (jnp.dot(xb, wh_ref[...], preferred_element_type=F32) + jnp.dot(xb, wl_ref[...], preferred_element_type=F32)
          + jnp.dot(xl, wh_ref[...], preferred_element_type=F32) + br_ref[...])
    lane = lax.broadcasted_iota(I32, lg.shape, 1).astype(F32)
    ninf = jnp.float32(-jnp.inf)

    def top(mask):
        val = jnp.max(jnp.where(mask, lg, ninf), axis=-1, keepdims=True)
        idx = jnp.min(jnp.where(mask & (lg == val), lane, float(LANES)), axis=-1, keepdims=True)
        return val, idx

    is_g = lane < N_GROUPS
    g_val, g_idx = top(is_g)
    g_w = 1.0 / jnp.sum(jnp.where(is_g, jnp.exp(lg - g_val), 0.0), axis=-1, keepdims=True)
    e_lo = N_GROUPS + g_idx * EXPERTS_PER_GROUP
    in_grp = (lane >= e_lo) & (lane < e_lo + EXPERTS_PER_GROUP)
    v1, i1 = top(in_grp)
    v2, i2 = top(in_grp & (lane != i1))
    t = jnp.exp(v2 - v1)
    w1 = g_w / (1.0 + t)
    rg_ref[...] = jnp.where(lane == 0, w1, jnp.where(lane == 1, w1 * t, 0.0))
    re_ref[...] = jnp.where(lane == 0, i1 - N_GROUPS, jnp.where(lane == 1, i2 - N_GROUPS, 0.0)).astype(I32)


def moe_pre(x, norm_g, mod, w_router, b_router, nc, tl, tm=512):
    n, d = x.shape
    grp = lambda i: _group_of_tile(i * tm, nc, tl)
    lane_out = lambda: pl.BlockSpec((tm, LANES), lambda i: (i, 0))
    w_hi = w_router.astype(BF16)
    return pl.pallas_call(
        _moe_pre_kernel,
        out_shape=(jax.ShapeDtypeStruct((n, HALF), U32), jax.ShapeDtypeStruct((n, LANES), F32),
                   jax.ShapeDtypeStruct((n, LANES), I32)),
        grid=(n // tm,),
        in_specs=[
            pl.BlockSpec((tm, d), lambda i: (i, 0)),
            pl.BlockSpec((1, d), lambda i: (0, 0)),
            pl.BlockSpec((None, None, 1, d), lambda i: (grp(i), 3, 0, 0)),
            pl.BlockSpec((None, None, 1, d), lambda i: (grp(i), 4, 0, 0)),
            pl.BlockSpec((d, LANES), lambda i: (0, 0)),
            pl.BlockSpec((d, LANES), lambda i: (0, 0)),
            pl.BlockSpec((1, LANES), lambda i: (0, 0)),
        ],
        out_specs=(pl.BlockSpec((tm, HALF), lambda i: (i, 0)), lane_out(), lane_out()),
        compiler_params=_cparams(("arbitrary",)),
        name="moe_pre",
    )(x, norm_g.reshape(1, d), mod, mod, w_hi, (w_router - w_hi.astype(F32)).astype(BF16), b_router)


def _expert_changed(be_ref, j):
    return (j == 0) | (be_ref[j] != be_ref[jnp.maximum(j - 1, 0)])


_ROW_STEP = LANES


def _for_row_count(nv, tb, body, out_ref):
    for r in range(_ROW_STEP, tb + 1, _ROW_STEP):
        @pl.when((nv > r - _ROW_STEP) & (nv <= r))
        def _(r=r):
            body(r)

    @pl.when(nv == 0)
    def _():
        out_ref[...] = jnp.zeros(out_ref.shape, out_ref.dtype)


def _moe_up_kernel(be_ref, bs_ref, nv_ref, nx_ref, x_ref, wg_hbm, wu_hbm, h_ref, wbuf, sem, wg_s, wu_s, *, layer):
    j = pl.program_id(0)
    nv = nv_ref[j]
    tb = x_ref.shape[0]

    def weight_copies(e):
        return (pltpu.make_async_copy(wg_hbm.at[layer, e], wbuf.at[0], sem.at[0]),
                pltpu.make_async_copy(wu_hbm.at[layer, e], wbuf.at[1], sem.at[1]))

    @pl.when(j == 0)
    def _():
        for cp in weight_copies(be_ref[0]):
            cp.start()

    @pl.when((nv > 0) & _expert_changed(be_ref, j))
    def _():
        for cp in weight_copies(be_ref[j]):
            cp.wait()
        wg_s[...] = wbuf[0].astype(BF16)
        wu_s[...] = wbuf[1].astype(BF16)

        @pl.when(nx_ref[j] >= 0)
        def _():
            for cp in weight_copies(nx_ref[j]):
                cp.start()

    def body(r):
        w = x_ref[0:r, :]
        lo = lax.bitcast_convert_type(w << 16, F32).astype(BF16)
        hi = lax.bitcast_convert_type(w & jnp.uint32(0xFFFF0000), F32).astype(BF16)

        def mm(w_s):
            return (jnp.dot(lo, w_s[0:HALF, :], preferred_element_type=F32)
                    + jnp.dot(hi, w_s[HALF:, :], preferred_element_type=F32))

        gt, up = mm(wg_s), mm(wu_s)
        h_ref[0:r, :] = (gt * jax.nn.sigmoid(gt) * up).astype(BF16)
        if r < tb:
            h_ref[r:tb, :] = jnp.zeros((tb - r, h_ref.shape[1]), BF16)

    _for_row_count(nv, tb, body, h_ref)


def _pack_halves(y):
    yb = y.astype(BF16).astype(F32)
    return (_bits(yb[:, HALF:]) & jnp.uint32(0xFFFF0000)) | (_bits(yb[:, :HALF]) >> 16)


def _unpack_halves(w):
    return (lax.bitcast_convert_type(w << 16, F32), lax.bitcast_convert_type(w & jnp.uint32(0xFFFF0000), F32))


def _moe_down_kernel(be_ref, bs_ref, nv_ref, nx_ref, h_ref, wd_hbm, y_ref, wbuf, sem, wd_s, *, layer):
    j = pl.program_id(0)
    nv = nv_ref[j]
    tb = h_ref.shape[0]

    def weight_copy(e):
        return pltpu.make_async_copy(wd_hbm.at[layer, e], wbuf, sem.at[0])

    @pl.when(j == 0)
    def _():
        weight_copy(be_ref[0]).start()

    @pl.when((nv > 0) & _expert_changed(be_ref, j))
    def _():
        weight_copy(be_ref[j]).wait()
        wd_s[...] = wbuf[...].astype(BF16)

        @pl.when(nx_ref[j] >= 0)
        def _():
            weight_copy(nx_ref[j]).start()

    def body(r):
        y_ref[0:r, :] = _pack_halves(jnp.dot(h_ref[0:r, :], wd_s[...], preferred_element_type=F32))
        if r < tb:
            y_ref[r:tb, :] = jnp.zeros((tb - r, y_ref.shape[1]), U32)

    _for_row_count(nv, tb, body, y_ref)


def moe_experts(xb, blk_e, blk_src, n_valid, nxt_e, w_gate, w_up, w_down, layer, tb=MOE_TB):
    rows = xb.shape[0]
    n_blocks = rows // tb
    d, de = w_gate.shape[-2:]
    any_spec = pl.BlockSpec(memory_space=pl.ANY)
    h = pl.pallas_call(
        functools.partial(_moe_up_kernel, layer=layer),
        out_shape=jax.ShapeDtypeStruct((rows, de), BF16),
        grid_spec=pltpu.PrefetchScalarGridSpec(
            num_scalar_prefetch=4,
            grid=(n_blocks,),
            in_specs=[pl.BlockSpec((tb, HALF), lambda j, be, bs, nv, nx: (bs[j], 0)), any_spec, any_spec],
            out_specs=pl.BlockSpec((tb, de), lambda j, be, bs, nv, nx: (j, 0)),
            scratch_shapes=[pltpu.VMEM((2, d, de), F32), pltpu.SemaphoreType.DMA((2,)),
                            pltpu.VMEM((d, de), BF16), pltpu.VMEM((d, de), BF16)],
        ),
        compiler_params=_cparams(("arbitrary",)),
        name="moe_up",
    )(blk_e, blk_src, n_valid, nxt_e, xb, w_gate, w_up)
    return pl.pallas_call(
        functools.partial(_moe_down_kernel, layer=layer),
        out_shape=jax.ShapeDtypeStruct((rows, HALF), U32),
        grid_spec=pltpu.PrefetchScalarGridSpec(
            num_scalar_prefetch=4,
            grid=(n_blocks,),
            in_specs=[pl.BlockSpec((tb, de), lambda j, be, bs, nv, nx: (bs[j], 0)), any_spec],
            out_specs=pl.BlockSpec((tb, HALF), lambda j, be, bs, nv, nx: (j, 0)),
            scratch_shapes=[pltpu.VMEM((de, d), F32), pltpu.SemaphoreType.DMA((1,)), pltpu.VMEM((de, d), BF16)],
        ),
        compiler_params=_cparams(("arbitrary",)),
        name="moe_down",
    )(blk_e, blk_src, n_valid, nxt_e, h, w_down)


def moe_layout(eid, tb=MOE_TB):
    n = eid.shape[0]
    s_len = n * TOP_K
    flat_e = eid.reshape(s_len)
    onehot = (flat_e[:, None] == jnp.arange(N_EXPERTS, dtype=I32)[None, :]).astype(I32)
    rank = jnp.sum((jnp.cumsum(onehot, axis=0) - onehot) * onehot, axis=1)
    counts = jnp.sum(onehot, axis=0)
    nblk_e = (counts + tb - 1) // tb
    blk_end = jnp.cumsum(nblk_e)
    blk_start = blk_end - nblk_e
    dest = jnp.sum(onehot * blk_start[None, :], axis=1) * tb + rank
    n_blocks = s_len // tb + N_EXPERTS
    n_used = blk_end[-1]
    jc = jnp.minimum(jnp.arange(n_blocks, dtype=I32), n_used - 1)
    blk_e = jnp.minimum(jnp.sum((blk_end[None, :] <= jc[:, None]).astype(I32), axis=1), N_EXPERTS - 1).astype(I32)
    used = jnp.arange(n_blocks, dtype=I32) < n_used
    nxt_blk = blk_end[blk_e]
    nxt_e = jnp.where(used & (nxt_blk < n_used), blk_e[jnp.minimum(nxt_blk, n_blocks - 1)], -1).astype(I32)
    n_valid = jnp.where(used, jnp.clip(counts[blk_e] - (jc - blk_start[blk_e]) * tb, 0, tb), 0).astype(I32)
    pad_e = nblk_e * tb - counts
    pad_incl = jnp.cumsum(pad_e)
    n_fill = n_blocks * tb - s_len
    k = jnp.arange(n_fill, dtype=I32)
    e_k = jnp.minimum(jnp.sum((pad_incl[None, :] <= k[:, None]).astype(I32), axis=1), N_EXPERTS - 1)
    oh_k = (e_k[:, None] == jnp.arange(N_EXPERTS, dtype=I32)[None, :]).astype(I32)
    pick = lambda v: jnp.sum(oh_k * v[None, :], axis=1)
    in_used = k < pad_incl[-1]
    fill_pos = jnp.where(in_used, pick(blk_start * tb + counts) + k - pick(pad_incl - pad_e),
                         n_used * tb + k - pad_incl[-1])
    keys = jnp.concatenate([dest, fill_pos])
    vals = jnp.concatenate([jnp.arange(s_len, dtype=I32) // TOP_K, k % n])
    slot_tok = lax.sort_key_val(keys, vals)[1]
    return dest.reshape(n, TOP_K), slot_tok, blk_e, jc, n_valid, nxt_e


def _take_rows(a, idx):
    return a.at[idx].get(mode="promise_in_bounds")


def _final_norm_kernel(x_ref, y0_ref, y1_ref, gt_ref, g2_ref, g_ref, oc_ref, ol_ref, xs_ref, *, ctx_tiles):
    _combine_into(xs_ref, x_ref, y0_ref, y1_ref, gt_ref, g2_ref)
    x = xs_ref[...]
    y = x * lax.rsqrt(jnp.mean(x * x, axis=-1, keepdims=True) + NORM_EPS) * g_ref[...]

    @pl.when(pl.program_id(0) < ctx_tiles)
    def _():
        oc_ref[...] = y

    @pl.when(pl.program_id(0) >= ctx_tiles)
    def _():
        ol_ref[...] = y


def final_norm(x, pend, g, nc, tl, tm=512):
    n, d = x.shape
    y0, y1, gates, mod_prev = pend
    grp = lambda i: _group_of_tile(i * tm, nc, tl)
    ctx_tiles, lat_tiles = nc // tm, (n - nc) // tm
    row = lambda w: pl.BlockSpec((tm, w), lambda i: (i, 0))
    return pl.pallas_call(
        functools.partial(_final_norm_kernel, ctx_tiles=ctx_tiles),
        out_shape=(jax.ShapeDtypeStruct((nc, d), F32), jax.ShapeDtypeStruct((n - nc, d), F32)),
        grid=(n // tm,),
        in_specs=[row(d), row(HALF), row(HALF), row(LANES),
                  pl.BlockSpec((None, None, 1, d), lambda i: (grp(i), 5, 0, 0)),
                  pl.BlockSpec((1, d), lambda i: (0, 0))],
        out_specs=(pl.BlockSpec((tm, d), lambda i: (jnp.minimum(i, ctx_tiles - 1), 0)),
                   pl.BlockSpec((tm, d), lambda i: (jnp.clip(i - ctx_tiles, 0, lat_tiles - 1), 0))),
        scratch_shapes=[pltpu.VMEM((tm, d), F32)],
        compiler_params=_cparams(("arbitrary",)),
        name="final_norm",
    )(x, y0, y1, gates, mod_prev, g.reshape(1, d))


def _pad_lanes(a):
    return jnp.pad(a, ((0, 0), (0, LANES - a.shape[1])))


def kernel(x_prompt, x_sample, cache_k, cache_v, state_mlstm_C, state_mlstm_n, state_mlstm_m, state_rglru_h, c, c_ctx, ada_w, ada_b, norm1_g, w_in, attn_sink, mlstm_gate_b, mlstm_norm_g, rg_conv_w, rg_conv_b, rg_wr, rg_br, rg_wi, rg_bi, rg_lam, w_out, norm2_g, router_wg, router_bg, router_we, router_be, exp_w_gate, exp_w_up, exp_w_down, final_norm_g):
    bc, tc, d = x_prompt.shape
    bl, tl, _ = x_sample.shape
    depth = w_in.shape[0]
    past = cache_k.shape[2]
    nc, nl = bc * tc, bl * tl
    tm = math.gcd(1024, math.gcd(tl, nc))

    x = jnp.concatenate([x_prompt.reshape(nc, d), x_sample.reshape(nl, d)], axis=0)
    cvec = jnp.concatenate([c_ctx[None, :], c, jnp.zeros((SUBLANES - 1 - bl, d), F32)], axis=0)
    mods = adaln_all(cvec, ada_w, ada_b).reshape(depth, SUBLANES, 6, 1, d)
    rope_c, rope_s = rope_tables(tl)
    gsplit = Z_MAIN - 2 * R_WIDTH
    w_main = jnp.concatenate([w_in[:, :, :gsplit], w_in[:, :, gsplit + M_GATES:]], axis=2).astype(BF16)
    w_gcol = jnp.pad(w_in[:, :, gsplit:gsplit + M_GATES], ((0, 0), (0, 0), (0, LANES - M_GATES))).astype(BF16)
    w_out_b = w_out.astype(BF16)
    ck = cache_k.reshape(bl, depth, past, KV_WIDTH)
    cv = cache_v.reshape(bl, depth, past, KV_WIDTH)
    cn0 = jnp.concatenate([state_mlstm_C, jnp.broadcast_to(state_mlstm_n[..., None], state_mlstm_C.shape)], axis=-1)

    new_k, new_v, new_c, new_n, new_m, new_h = [], [], [], [], [], []
    pend = None
    for l in range(depth):
        mod = mods[l]
        if pend is None:
            z, zg = in_proj(x, None, norm1_g[l], mod, w_main[l], w_gcol[l], nc, tl, tm)
        else:
            z, zg, x = in_proj(x, pend, norm1_g[l], mod, w_main[l], w_gcol[l], nc, tl, tm // 2)

        a_ctx = attention_context(z, attn_sink[l], bc, tc)
        a_lat = attention_latent(z, ck, cv, l, attn_sink[l], rope_c, rope_s, nc, bl, tl)

        rows, tiles, kt = mlstm_gates(z, zg, _pad_lanes(mlstm_gate_b[l].reshape(1, M_GATES)), tile=tm)
        res = mlstm_scan(z, rows, tiles, kt, cn0, state_mlstm_m, l, bc, tc, bl, tl)
        mh = res[0:2]
        for dr in range(2):
            cnf, mf = res[2 + 2 * dr:4 + 2 * dr]
            new_c.append(cnf[:bc, :, :, :M_DK])
            new_n.append(cnf[:bc, :, :, M_DK])
            new_m.append(mf[:bc, :M_HEADS, 0])
        rh = []
        for dr in range(2):
            rd, hf = rglru_direction(z, rg_conv_w, rg_conv_b, rg_wr, rg_br, rg_wi, rg_bi, rg_lam, state_rglru_h,
                                     l, dr, bc, tc, bl, tl)
            rh.append(rd)
            new_h.append(hf[:bc, 0])
        new_k.append(z[:nc, _C_K:_C_K + KV_WIDTH].reshape(bc, tc, KV_HEADS, HEAD_DIM))
        new_v.append(z[:nc, _C_V:_C_V + KV_WIDTH].reshape(bc, tc, KV_HEADS, HEAD_DIM))

        x = out_proj(x, a_ctx, a_lat, mh[0], mh[1], rh[0], rh[1], z, mlstm_norm_g[l], mod, w_out_b[l], nc, tl)

        w_router = _pad_lanes(jnp.concatenate([router_wg[l], router_we[l]], axis=1))
        b_router = _pad_lanes(jnp.concatenate([router_bg[l], router_be[l]])[None, :])
        xp, route_g, route_e = moe_pre(x, norm2_g[l], mod, w_router, b_router, nc, tl)
        dest, slot_tok, blk_e, blk_src, n_valid, nxt_e = moe_layout(route_e[:, :TOP_K])
        yb = moe_experts(_take_rows(xp, slot_tok), blk_e, blk_src, n_valid, nxt_e, exp_w_gate, exp_w_up, exp_w_down, l)
        pend = (_take_rows(yb, dest[:, 0]), _take_rows(yb, dest[:, 1]), route_g, mod)

    y_ctx, y_lat = final_norm(x, pend, final_norm_g, nc, tl)
    stack2 = lambda parts: jnp.stack([jnp.stack(parts[2 * l:2 * l + 2], axis=1) for l in range(depth)], axis=1)
    return (y_ctx.reshape(bc, tc, d), y_lat.reshape(bl, tl, d),
            jnp.stack(new_k, axis=1), jnp.stack(new_v, axis=1),
            stack2(new_c), stack2(new_n), stack2(new_m), stack2(new_h))
```

```python
import functools
import math

import jax
import jax.numpy as jnp
from jax import lax
from jax.experimental import pallas as pl
from jax.experimental.pallas import tpu as pltpu

F32 = jnp.float32
BF16 = jnp.bfloat16
U32 = jnp.uint32
I32 = jnp.int32

D_MODEL = 2048
HEAD_DIM = 128
A_WIDTH = D_MODEL // 2
N_HEADS = A_WIDTH // HEAD_DIM
KV_HEADS = 2
GQA_GROUP = N_HEADS // KV_HEADS
KV_WIDTH = KV_HEADS * HEAD_DIM
WINDOW = 128
Q_BLOCK = 128
GRID_W = 64
ROPE_BASE = 10000.0
ROPE_AXIS = HEAD_DIM // 2
ATTN_SCALE = HEAD_DIM ** -0.5
NEG_INF = -1e30
M_WIDTH = D_MODEL // 4
M_HEADS = 4
M_DK = M_WIDTH // M_HEADS
M_GATES = 2 * 2 * M_HEADS
R_WIDTH = D_MODEL // 4
R_BLOCKS = 4
R_BW = R_WIDTH // R_BLOCKS
CONV_W = 4
RG_C = 8.0
N_GROUPS = 4
EXPERTS_PER_GROUP = 8
N_EXPERTS = N_GROUPS * EXPERTS_PER_GROUP
TOP_K = 2
D_EXPERT = D_MODEL // 2
NORM_EPS = 1e-6

LANES = 128
SUBLANES = 8
Z_MAIN = A_WIDTH + 2 * KV_WIDTH + 4 * M_WIDTH + 2 * R_WIDTH
_C_Q, _C_K, _C_V = 0, A_WIDTH, A_WIDTH + KV_WIDTH
_C_QM = A_WIDTH + 2 * KV_WIDTH
_C_KM, _C_VM, _C_OM = _C_QM + M_WIDTH, _C_QM + 2 * M_WIDTH, _C_QM + 3 * M_WIDTH
_C_XR, _C_YR = _C_QM + 4 * M_WIDTH, _C_QM + 4 * M_WIDTH + R_WIDTH

M_CHUNK = LANES
R_CHUNK = 256
MOE_TB = 512
VMEM_LIMIT = 56 * 1024 * 1024
HALF = D_MODEL // 2


def _cparams(sem):
    return pltpu.CompilerParams(dimension_semantics=sem, vmem_limit_bytes=VMEM_LIMIT)


def _group_of_tile(row0, nc, tl):
    return jnp.where(row0 < nc, 0, 1 + (jnp.maximum(row0 - nc, 0)) // tl)


def _adaln_kernel(c_ref, w_ref, b_ref, o_ref):
    c = c_ref[...]
    s = (c * jax.nn.sigmoid(c)).astype(BF16)
    o_ref[...] = jnp.dot(s, w_ref[...].astype(BF16), preferred_element_type=F32) + b_ref[...]


def adaln_all(cvec, ada_w, ada_b, tn=1024):
    depth, d, d6 = ada_w.shape
    return pl.pallas_call(
        _adaln_kernel,
        out_shape=jax.ShapeDtypeStruct((depth, SUBLANES, d6), F32),
        grid=(depth, d6 // tn),
        in_specs=[
            pl.BlockSpec((SUBLANES, d), lambda l, j: (0, 0)),
            pl.BlockSpec((None, d, tn), lambda l, j: (l, 0, j)),
            pl.BlockSpec((None, 1, tn), lambda l, j: (l, 0, j)),
        ],
        out_specs=pl.BlockSpec((None, SUBLANES, tn), lambda l, j: (l, 0, j)),
        compiler_params=_cparams(("arbitrary", "arbitrary")),
        name="adaln",
    )(cvec, ada_w, ada_b.reshape(depth, 1, d6))


def _combine_into(o_ref, x_ref, y0_ref, y1_ref, gt_ref, gate_ref):
    g0, g1 = gt_ref[:, 0:1], gt_ref[:, 1:2]
    for half, (a0, a1) in enumerate(zip(_unpack_halves(y0_ref[...]), _unpack_halves(y1_ref[...]))):
        cs = slice(half * HALF, (half + 1) * HALF)
        o_ref[:, cs] = x_ref[:, cs] + gate_ref[:, cs] * (g0 * a0 + g1 * a1)


def _in_kernel(*refs, combine, n_tiles):
    if combine:
        (x_hbm, y0_hbm, y1_hbm, gt_ref, g2_ref, g_ref, sh_ref, sc_ref, w_ref, wg_ref, z_ref, zg_ref, xo_ref,
         xn_ref, xbuf, y0buf, y1buf, sem) = refs
    else:
        x_ref, g_ref, sh_ref, sc_ref, w_ref, wg_ref, z_ref, zg_ref, xn_ref = refs
    i = pl.program_id(0)
    tm = xn_ref.shape[0]

    def row_copies(t):
        rows = pl.ds(pl.multiple_of(t * tm, tm), tm)
        return (pltpu.make_async_copy(x_hbm.at[rows], xbuf, sem.at[0]),
                pltpu.make_async_copy(y0_hbm.at[rows], y0buf, sem.at[1]),
                pltpu.make_async_copy(y1_hbm.at[rows], y1buf, sem.at[2]))

    @pl.when(pl.program_id(1) == 0)
    def _():
        if combine:
            @pl.when(i == 0)
            def _():
                for cp in row_copies(0):
                    cp.start()

            for cp in row_copies(i):
                cp.wait()
            _combine_into(xo_ref, xbuf, y0buf, y1buf, gt_ref, g2_ref)

            @pl.when(i + 1 < n_tiles)
            def _():
                for cp in row_copies(i + 1):
                    cp.start()

            x = xo_ref[...]
        else:
            x = x_ref[...]
        y = x * lax.rsqrt(jnp.mean(x * x, axis=-1, keepdims=True) + NORM_EPS) * g_ref[...]
        xn = (y * (1.0 + sc_ref[...]) + sh_ref[...]).astype(BF16)
        xn_ref[...] = xn
        zg_ref[...] = jnp.dot(xn, wg_ref[...], preferred_element_type=F32)

    z_ref[...] = jnp.dot(xn_ref[...], w_ref[...], preferred_element_type=F32)


def in_proj(x, pend, norm_g, mod, w_main, w_gate, nc, tl, tm, tn=768):
    n, d = x.shape
    zw = w_main.shape[1]
    grp = lambda i: _group_of_tile(i * tm, nc, tl)
    row = lambda w: pl.BlockSpec((tm, w), lambda i, j: (i, 0))
    modrow = lambda k: pl.BlockSpec((None, None, 1, d), lambda i, j: (grp(i), k, 0, 0))
    out_shape = [jax.ShapeDtypeStruct((n, zw), F32), jax.ShapeDtypeStruct((n, LANES), F32)]
    out_specs = [pl.BlockSpec((tm, tn), lambda i, j: (i, j)), row(LANES)]
    scratch = [pltpu.VMEM((tm, d), BF16)]
    if pend is None:
        in_specs, args = [row(d)], [x]
    else:
        y0, y1, gates, mod_prev = pend
        any_spec = pl.BlockSpec(memory_space=pl.ANY)
        in_specs = [any_spec, any_spec, any_spec, row(LANES), modrow(5)]
        args = [x, y0, y1, gates, mod_prev]
        out_shape.append(jax.ShapeDtypeStruct((n, d), F32))
        out_specs.append(row(d))
        scratch += [pltpu.VMEM((tm, d), F32), pltpu.VMEM((tm, HALF), U32), pltpu.VMEM((tm, HALF), U32),
                    pltpu.SemaphoreType.DMA((3,))]
    in_specs += [pl.BlockSpec((1, d), lambda i, j: (0, 0)), modrow(0), modrow(1),
                 pl.BlockSpec((d, tn), lambda i, j: (0, j)), pl.BlockSpec((d, LANES), lambda i, j: (0, 0))]
    args += [norm_g.reshape(1, d), mod, mod, w_main, w_gate]
    return pl.pallas_call(
        functools.partial(_in_kernel, combine=pend is not None, n_tiles=n // tm),
        out_shape=tuple(out_shape),
        grid=(n // tm, zw // tn),
        in_specs=in_specs,
        out_specs=tuple(out_specs),
        scratch_shapes=scratch,
        compiler_params=_cparams(("arbitrary", "arbitrary")),
        name="in_proj",
    )(*args)


def _rope(x, c, s):
    lane = lax.broadcasted_iota(I32, x.shape, 1)
    half = ROPE_AXIS // 2
    partner = jnp.where((lane & (ROPE_AXIS - 1)) < half, pltpu.roll(x, HEAD_DIM - half, 1), pltpu.roll(x, half, 1))
    return x * c + partner * s


def _softmax_pv(parts, sink_col):
    m = sink_col
    for s, _ in parts:
        m = jnp.maximum(m, jnp.max(s, axis=-1, keepdims=True))
    den = jnp.exp(sink_col - m)
    acc = None
    for s, v in parts:
        p = jnp.exp(s - m)
        den = den + jnp.sum(p, axis=-1, keepdims=True)
        pv = jnp.dot(p.astype(BF16), v, preferred_element_type=F32)
        acc = pv if acc is None else acc + pv
    return acc / den


def _qk(q, k):
    return lax.dot_general(q, k, (((1,), (1,)), ((), ())), preferred_element_type=F32) * ATTN_SCALE


def _sink_column(sink_ref, n, rows):
    ridx = lax.broadcasted_iota(I32, (GQA_GROUP * rows, 1), 0)
    col = jnp.full((GQA_GROUP * rows, 1), sink_ref[n * GQA_GROUP], F32)
    for g in range(1, GQA_GROUP):
        col = jnp.where(ridx >= g * rows, sink_ref[n * GQA_GROUP + g], col)
    return col


def _attn_ctx_kernel(sink_ref, q_ref, k_ref, v_ref, o_ref):
    t = q_ref.shape[0]
    for n in range(KV_HEADS):
        k = k_ref[:, n * HEAD_DIM:(n + 1) * HEAD_DIM].astype(BF16)
        v = v_ref[:, n * HEAD_DIM:(n + 1) * HEAD_DIM].astype(BF16)
        q = jnp.concatenate(
            [q_ref[:, (n * GQA_GROUP + g) * HEAD_DIM:(n * GQA_GROUP + g + 1) * HEAD_DIM] for g in range(GQA_GROUP)],
            axis=0).astype(BF16)
        out = _softmax_pv([(_qk(q, k), v)], _sink_column(sink_ref, n, t))
        for g in range(GQA_GROUP):
            h = n * GQA_GROUP + g
            o_ref[:, h * HEAD_DIM:(h + 1) * HEAD_DIM] = out[g * t:(g + 1) * t].astype(o_ref.dtype)


def attention_context(z, sink, b, t):
    return pl.pallas_call(
        _attn_ctx_kernel,
        out_shape=jax.ShapeDtypeStruct((b * t, A_WIDTH), BF16),
        grid=(b,),
        in_specs=[
            pl.BlockSpec(memory_space=pltpu.SMEM),
            pl.BlockSpec((t, A_WIDTH), lambda i: (i, _C_Q // A_WIDTH)),
            pl.BlockSpec((t, KV_WIDTH), lambda i: (i, _C_K // KV_WIDTH)),
            pl.BlockSpec((t, KV_WIDTH), lambda i: (i, _C_V // KV_WIDTH)),
        ],
        out_specs=pl.BlockSpec((t, A_WIDTH), lambda i: (i, 0)),
        compiler_params=_cparams(("arbitrary",)),
        name="attn_ctx",
    )(sink, z, z, z)


def _attn_lat_kernel(sink_ref, q_ref, k_ref, v_ref, kc_ref, vc_ref, cq_ref, sq_ref, ck_ref, sk_ref, o_ref,
                     kr_ref, vp_ref, *, t):
    i = pl.program_id(1)
    rope_rows = 512

    @pl.when(i == 0)
    def _():
        zpad = jnp.zeros((WINDOW, KV_WIDTH), BF16)
        kr_ref[0:WINDOW, :] = zpad
        kr_ref[WINDOW + t:2 * WINDOW + t, :] = zpad
        vp_ref[0:WINDOW, :] = zpad
        vp_ref[WINDOW + t:2 * WINDOW + t, :] = zpad

        def body(c, carry):
            r0 = pl.multiple_of(c * rope_rows, rope_rows)
            cs, sn = ck_ref[pl.ds(r0, rope_rows), :], sk_ref[pl.ds(r0, rope_rows), :]
            for n in range(KV_HEADS):
                kk = k_ref[pl.ds(r0, rope_rows), n * HEAD_DIM:(n + 1) * HEAD_DIM]
                kr_ref[pl.ds(WINDOW + r0, rope_rows), n * HEAD_DIM:(n + 1) * HEAD_DIM] = _rope(kk, cs, sn).astype(BF16)
            vp_ref[pl.ds(WINDOW + r0, rope_rows), :] = v_ref[pl.ds(r0, rope_rows), :].astype(BF16)
            return carry

        lax.fori_loop(0, t // rope_rows, body, 0)

    span = Q_BLOCK + 2 * WINDOW
    rows = GQA_GROUP * Q_BLOCK
    r = lax.broadcasted_iota(I32, (rows, span), 0) & (Q_BLOCK - 1)
    c = lax.broadcasted_iota(I32, (rows, span), 1)
    kpos = (i - 1) * Q_BLOCK + c
    mask = (c >= r) & (c <= r + 2 * WINDOW) & (kpos >= 0) & (kpos < t)
    w0 = pl.multiple_of(i * Q_BLOCK, Q_BLOCK)
    cq, sq = cq_ref[...], sq_ref[...]
    for n in range(KV_HEADS):
        hs = slice(n * HEAD_DIM, (n + 1) * HEAD_DIM)
        q = jnp.concatenate(
            [_rope(q_ref[:, (n * GQA_GROUP + g) * HEAD_DIM:(n * GQA_GROUP + g + 1) * HEAD_DIM], cq, sq)
             for g in range(GQA_GROUP)], axis=0).astype(BF16)
        s_win = jnp.where(mask, _qk(q, kr_ref[pl.ds(w0, span), hs]), NEG_INF)
        s_ctx = _qk(q, kc_ref[:, hs].astype(BF16))
        out = _softmax_pv([(s_win, vp_ref[pl.ds(w0, span), hs]), (s_ctx, vc_ref[:, hs].astype(BF16))],
                          _sink_column(sink_ref, n, Q_BLOCK))
        for g in range(GQA_GROUP):
            h = n * GQA_GROUP + g
            o_ref[:, h * HEAD_DIM:(h + 1) * HEAD_DIM] = out[g * Q_BLOCK:(g + 1) * Q_BLOCK].astype(o_ref.dtype)


def attention_latent(z, cache_k, cache_v, layer, sink, rope_c, rope_s, row0, b, t):
    nqb = t // Q_BLOCK
    past = cache_k.shape[2]
    qb0 = row0 // Q_BLOCK
    tb0 = row0 // t
    kern = functools.partial(_attn_lat_kernel, t=t)
    return pl.pallas_call(
        kern,
        out_shape=jax.ShapeDtypeStruct((b * t, A_WIDTH), BF16),
        grid=(b, nqb),
        in_specs=[
            pl.BlockSpec(memory_space=pltpu.SMEM),
            pl.BlockSpec((Q_BLOCK, A_WIDTH), lambda bi, i: (qb0 + bi * nqb + i, _C_Q // A_WIDTH)),
            pl.BlockSpec((t, KV_WIDTH), lambda bi, i: (tb0 + bi, _C_K // KV_WIDTH)),
            pl.BlockSpec((t, KV_WIDTH), lambda bi, i: (tb0 + bi, _C_V // KV_WIDTH)),
            pl.BlockSpec((None, None, past, KV_WIDTH), lambda bi, i: (bi, layer, 0, 0)),
            pl.BlockSpec((None, None, past, KV_WIDTH), lambda bi, i: (bi, layer, 0, 0)),
            pl.BlockSpec((Q_BLOCK, HEAD_DIM), lambda bi, i: (i, 0)),
            pl.BlockSpec((Q_BLOCK, HEAD_DIM), lambda bi, i: (i, 0)),
            pl.BlockSpec((t, HEAD_DIM), lambda bi, i: (0, 0)),
            pl.BlockSpec((t, HEAD_DIM), lambda bi, i: (0, 0)),
        ],
        out_specs=pl.BlockSpec((Q_BLOCK, A_WIDTH), lambda bi, i: (bi * nqb + i, 0)),
        scratch_shapes=[pltpu.VMEM((t + 2 * WINDOW, KV_WIDTH), BF16), pltpu.VMEM((t + 2 * WINDOW, KV_WIDTH), BF16)],
        compiler_params=_cparams(("arbitrary", "arbitrary")),
        name="attn_lat",
    )(sink, z, z, z, cache_k, cache_v, rope_c, rope_s, rope_c, rope_s)


def rope_tables(t):
    pos = jnp.arange(t)
    inv = ROPE_BASE ** (-jnp.arange(0, ROPE_AXIS, 2, dtype=F32) / ROPE_AXIS)

    def cs(p):
        ang = p.astype(F32)[:, None] * inv[None, :]
        return jnp.cos(ang), jnp.sin(ang)

    cr, sr = cs(pos // GRID_W)
    cc, sc = cs(pos % GRID_W)
    return (jnp.concatenate([cr, cr, cc, cc], axis=-1), jnp.concatenate([-sr, sr, -sc, sc], axis=-1))


class _Seqs:
    def __init__(self, n_ctx, t_ctx, n_lat, t_lat, chunk, reverse):
        self.n_ctx, self.t_ctx, self.n_lat, self.t_lat = n_ctx, t_ctx, n_lat, t_lat
        self.chunk, self.reverse = chunk, reverse
        self.cpc, self.cpl = t_ctx // chunk, t_lat // chunk
        self.ctx_chunks = n_ctx * self.cpc
        self.n_chunks = self.ctx_chunks + n_lat * self.cpl

    def chunk_of_step(self, s):
        return (self.n_chunks - 1 - s) if self.reverse else s

    def info(self, g):
        is_ctx = g < self.ctx_chunks
        gl = jnp.maximum(g - self.ctx_chunks, 0)
        gc = jnp.minimum(g, self.ctx_chunks - 1)
        pos = jnp.where(is_ctx, gc % self.cpc, gl % self.cpl)
        per = jnp.where(is_ctx, self.cpc, self.cpl)
        lat = gl // self.cpl
        seq = jnp.where(is_ctx, gc // self.cpc, self.n_ctx + lat)
        head, tail = pos == 0, pos == per - 1
        return is_ctx, seq, lat, (tail if self.reverse else head), (head if self.reverse else tail)


def _lane_scan(x, op, reverse):
    n = x.shape[1]
    lane = lax.broadcasted_iota(I32, x.shape, 1)
    s = 1
    while s < n:
        if reverse:
            x = jnp.where(lane < n - s, op(x, pltpu.roll(x, n - s, 1)), x)
        else:
            x = jnp.where(lane >= s, op(x, pltpu.roll(x, s, 1)), x)
        s *= 2
    return x


_GROWS = 2 * SUBLANES


def _mlstm_gate_kernel(g_ref, gb_ref, k_ref, rows_ref, tiles_ref, kt_ref):
    L = M_CHUNK
    nck = g_ref.shape[0] // L
    g = g_ref[...] + gb_ref[...]
    s = jnp.concatenate([g[c * L:(c + 1) * L].T[0:_GROWS] for c in range(nck)], axis=0)
    row = lax.broadcasted_iota(I32, s.shape, 0)
    is_rev = (row & SUBLANES) != 0
    lf = jax.nn.log_sigmoid(s)
    f = jnp.where(is_rev, _lane_scan(lf, jnp.add, True), _lane_scan(lf, jnp.add, False))
    f = pltpu.roll(f, s.shape[0] - M_HEADS, 0)
    r = s - f
    cm = jnp.where(is_rev, _lane_scan(r, jnp.maximum, True), _lane_scan(r, jnp.maximum, False))
    rows_ref[...] = r
    pad = jnp.zeros((LANES - 2 * _GROWS, L), F32)
    scale = M_DK ** -0.5
    for c in range(nck):
        sl = slice(c * _GROWS, (c + 1) * _GROWS)
        cols = jnp.concatenate([cm[sl], f[sl], pad], axis=0).T
        for d in range(2):
            for h in range(M_HEADS):
                gi = d * SUBLANES + h
                tiles_ref[c, d, h] = jnp.broadcast_to(cols[:, gi:gi + 1], (L, LANES))
                tiles_ref[c, d, M_HEADS + h] = jnp.broadcast_to(cols[:, _GROWS + gi:_GROWS + gi + 1], (L, LANES))
        for h in range(M_HEADS):
            kt_ref[c, h] = (k_ref[c * L:(c + 1) * L, h * M_DK:(h + 1) * M_DK] * scale).T


def mlstm_gates(z, zg, gate_b, tile=512):
    n = zg.shape[0]
    L = M_CHUNK
    nck = tile // L
    return pl.pallas_call(
        _mlstm_gate_kernel,
        out_shape=(jax.ShapeDtypeStruct((n // L * _GROWS, L), F32),
                   jax.ShapeDtypeStruct((n // L, 2, 2 * M_HEADS, L, LANES), F32),
                   jax.ShapeDtypeStruct((n // L, M_HEADS, M_DK, L), F32)),
        grid=(n // tile,),
        in_specs=[pl.BlockSpec((tile, LANES), lambda i: (i, 0)), pl.BlockSpec((1, LANES), lambda i: (0, 0)),
                  pl.BlockSpec((tile, M_WIDTH), lambda i: (i, _C_KM // M_WIDTH))],
        out_specs=(pl.BlockSpec((nck * _GROWS, L), lambda i: (i, 0)),
                   pl.BlockSpec((nck, 2, 2 * M_HEADS, L, LANES), lambda i: (i, 0, 0, 0, 0)),
                   pl.BlockSpec((nck, M_HEADS, M_DK, L), lambda i: (i, 0, 0, 0))),
        compiler_params=_cparams(("arbitrary",)),
        name="mlstm_gates",
    )(zg, gate_b, z)


def _mlstm_kernel(m0_ref, qf_ref, vf_ref, ktf_ref, rf_ref, tf_ref, qb_ref, vb_ref, ktb_ref, rb_ref, tb_ref,
                  cn0f_ref, cn0b_ref, hf_ref, hb_ref, ocnf_ref, omf_ref, ocnb_ref, omb_ref,
                  cn_s, m_s, *, seqs, layer, depth):
    L = M_CHUNK
    step = pl.program_id(0)
    ti = lax.broadcasted_iota(I32, (L, L), 0)
    si = lax.broadcasted_iota(I32, (L, L), 1)
    ones = jnp.ones((L, M_DK), BF16)
    streams = (
        (0, seqs[0], qf_ref, vf_ref, ktf_ref, rf_ref, tf_ref, cn0f_ref, hf_ref, ocnf_ref, omf_ref),
        (1, seqs[1], qb_ref, vb_ref, ktb_ref, rb_ref, tb_ref, cn0b_ref, hb_ref, ocnb_ref, omb_ref),
    )
    for d, sq, q_ref, v_ref, kt_ref, rows_ref, tiles_ref, cn0_ref, h_ref, ocn_ref, om_ref in streams:
        rev = d == 1
        is_ctx, _, lat, first, last = sq.info(sq.chunk_of_step(step))

        @pl.when(first)
        def _(d=d, is_ctx=is_ctx, lat=lat, cn0_ref=cn0_ref):
            keep = jnp.where(is_ctx, 0.0, 1.0).astype(F32)
            cn_s[d] = cn0_ref[...] * keep
            m_s[d] = jnp.zeros((SUBLANES, LANES), F32)
            for h in range(M_HEADS):
                m0 = m0_ref[((lat * depth + layer) * 2 + d) * M_HEADS + h]
                m_s[d, h:h + 1, :] = jnp.full((1, LANES), m0, F32) * keep

        end = 0 if rev else L - 1
        causal = (si >= ti) if rev else (si <= ti)
        for h in range(M_HEADS):
            hs = slice(h * M_DK, (h + 1) * M_DK)
            gi = d * SUBLANES + h
            q = q_ref[:, hs].astype(BF16)
            vaug = jnp.concatenate([v_ref[:, hs].astype(BF16), ones], axis=1)
            kt = kt_ref[h]
            r_row = rows_ref[gi:gi + 1, :]
            f_b = tiles_ref[M_HEADS + h]
            m_old = m_s[d, h:h + 1, :]
            m_b = jnp.maximum(m_old, tiles_ref[h])
            dmat = jnp.where(causal, jnp.exp(jnp.where(causal, r_row - m_b, 0.0)), 0.0)
            s = jnp.dot(q, kt.astype(BF16), preferred_element_type=F32) * dmat
            wpq = (jnp.exp(m_old - m_b) * q.astype(F32)).astype(BF16)
            cn_old = cn_s[d, h]
            out = jnp.dot(jnp.concatenate([s.astype(BF16), wpq], axis=1),
                          jnp.concatenate([vaug, cn_old.astype(BF16)], axis=0), preferred_element_type=F32)
            h_ref[:, hs] = out[:, :M_DK] / jnp.maximum(jnp.abs(out[:, M_DK:]), jnp.exp(-(f_b + m_b)))
            m_end = m_b[end:end + 1, :]
            dec = jnp.exp(m_old - m_end)
            kwt = (kt * jnp.exp(r_row - m_end)).astype(BF16)
            cn_s[d, h] = jnp.concatenate([dec, dec], axis=1) * cn_old + jnp.dot(kwt, vaug, preferred_element_type=F32)
            m_s[d, h:h + 1, :] = f_b[end:end + 1, :] + m_end

        @pl.when(last)
        def _(d=d, ocn_ref=ocn_ref, om_ref=om_ref):
            ocn_ref[...] = cn_s[d]
            om_ref[...] = m_s[d]


def mlstm_scan(z, rows, tiles, kt, cn0, st_m, layer, n_ctx, t_ctx, n_lat, t_lat):
    L = M_CHUNK
    seqs = (_Seqs(n_ctx, t_ctx, n_lat, t_lat, L, False), _Seqs(n_ctx, t_ctx, n_lat, t_lat, L, True))
    n_seq = n_ctx + n_lat
    n_chunks = seqs[0].n_chunks
    kern = functools.partial(_mlstm_kernel, seqs=seqs, layer=layer, depth=st_m.shape[1])
    in_specs = [pl.BlockSpec(memory_space=pltpu.SMEM)]
    for d, sq in enumerate(seqs):
        cm = lambda s, sq=sq: sq.chunk_of_step(s)
        in_specs += [
            pl.BlockSpec((L, M_WIDTH), lambda s, cm=cm: (cm(s), _C_QM // M_WIDTH)),
            pl.BlockSpec((L, M_WIDTH), lambda s, cm=cm: (cm(s), _C_VM // M_WIDTH)),
            pl.BlockSpec((None, M_HEADS, M_DK, L), lambda s, cm=cm: (cm(s), 0, 0, 0)),
            pl.BlockSpec((_GROWS, L), lambda s, cm=cm: (cm(s), 0)),
            pl.BlockSpec((None, None, 2 * M_HEADS, L, LANES), lambda s, cm=cm, d=d: (cm(s), d, 0, 0, 0)),
        ]
    for d, sq in enumerate(seqs):
        lat_of = lambda s, sq=sq: sq.info(sq.chunk_of_step(s))[2]
        in_specs.append(pl.BlockSpec((None, None, None, M_HEADS, M_DK, 2 * M_DK),
                                     lambda s, f=lat_of, d=d: (f(s), layer, d, 0, 0, 0)))
    out_shape = [jax.ShapeDtypeStruct((n_chunks * L, M_WIDTH), F32)] * 2
    out_specs = [pl.BlockSpec((L, M_WIDTH), lambda s, sq=sq: (sq.chunk_of_step(s), 0)) for sq in seqs]
    for sq in seqs:
        seq_of = lambda s, sq=sq: sq.info(sq.chunk_of_step(s))[1]
        out_shape += [jax.ShapeDtypeStruct((n_seq, M_HEADS, M_DK, 2 * M_DK), F32),
                      jax.ShapeDtypeStruct((n_seq, SUBLANES, LANES), F32)]
        out_specs += [pl.BlockSpec((None, M_HEADS, M_DK, 2 * M_DK), lambda s, f=seq_of: (f(s), 0, 0, 0)),
                      pl.BlockSpec((None, SUBLANES, LANES), lambda s, f=seq_of: (f(s), 0, 0))]
    args = [st_m.reshape(-1), z, z, kt, rows, tiles, z, z, kt, rows, tiles, cn0, cn0]
    return pl.pallas_call(
        kern,
        out_shape=tuple(out_shape),
        grid=(n_chunks,),
        in_specs=in_specs,
        out_specs=tuple(out_specs),
        scratch_shapes=[pltpu.VMEM((2, M_HEADS, M_DK, 2 * M_DK), F32), pltpu.VMEM((2, SUBLANES, LANES), F32)],
        compiler_params=_cparams(("arbitrary",)),
        name="mlstm_scan",
    )(*args)


def _rglru_kernel(x_ref, xp_ref, xn_ref, cw_ref, cb_ref, wr_ref, br_ref, wi_ref, bi_ref, lam_ref, h0_ref,
                  h_ref, hf_ref, xpad, carry, *, seqs):
    L = seqs.chunk
    rev = seqs.reverse
    g = seqs.chunk_of_step(pl.program_id(0))
    is_ctx, _, _, first, last = seqs.info(g)
    head = last if rev else first
    tail = first if rev else last
    halo = SUBLANES
    xpad[0:halo, :] = xp_ref[...] * jnp.where(head, 0.0, 1.0).astype(F32)
    xpad[halo:halo + L, :] = x_ref[...]
    xpad[halo + L:2 * halo + L, :] = xn_ref[...] * jnp.where(tail, 0.0, 1.0).astype(F32)
    x = cb_ref[...]
    for j in range(CONV_W):
        x = x + cw_ref[j:j + 1, :] * xpad[halo - 2 + j:halo - 2 + j + L, :]

    rs, is_ = [], []
    for n in range(R_BLOCKS):
        xb = x[:, n * R_BW:(n + 1) * R_BW].astype(BF16)
        rs.append(jnp.dot(xb, wr_ref[n].astype(BF16), preferred_element_type=F32))
        is_.append(jnp.dot(xb, wi_ref[n].astype(BF16), preferred_element_type=F32))
    rg = jax.nn.sigmoid(jnp.concatenate(rs, axis=-1) + br_ref[...])
    ig = jax.nn.sigmoid(jnp.concatenate(is_, axis=-1) + bi_ref[...])
    log_a = -RG_C * rg * jax.nn.softplus(-lam_ref[...])
    a = jnp.exp(log_a)
    u = jnp.sqrt(1.0 - jnp.exp(2.0 * log_a)) * (ig * x)

    sub = lax.broadcasted_iota(I32, (L, R_WIDTH), 0) & (SUBLANES - 1)
    s = 1
    while s < SUBLANES:
        if rev:
            ok = sub < SUBLANES - s
            a_sh, u_sh = pltpu.roll(a, L - s, 0), pltpu.roll(u, L - s, 0)
        else:
            ok = sub >= s
            a_sh, u_sh = pltpu.roll(a, s, 0), pltpu.roll(u, s, 0)
        u = jnp.where(ok, a * u_sh + u, u)
        a = jnp.where(ok, a * a_sh, a)
        s *= 2

    @pl.when(first)
    def _():
        carry[...] = h0_ref[...] * jnp.where(is_ctx, 0.0, 1.0).astype(F32)

    n_grp = L // SUBLANES
    edge = 0 if rev else SUBLANES - 1
    hprev = carry[...]
    for k in range(n_grp):
        gidx = (n_grp - 1 - k) if rev else k
        rows = slice(gidx * SUBLANES, (gidx + 1) * SUBLANES)
        hg = a[rows] * hprev + u[rows]
        h_ref[rows, :] = hg
        hprev = hg[edge:edge + 1, :]
    carry[...] = hprev

    @pl.when(last)
    def _():
        hf_ref[...] = hprev


def rglru_direction(z, conv_w, conv_b, wr, br, wi, bi, lam, h0, layer, direction, n_ctx, t_ctx, n_lat, t_lat):
    seqs = _Seqs(n_ctx, t_ctx, n_lat, t_lat, R_CHUNK, direction == 1)
    n_seq = n_ctx + n_lat
    L = R_CHUNK
    n_rows = seqs.n_chunks * L
    hb = L // SUBLANES
    nb8 = n_rows // SUBLANES
    cm = lambda s: seqs.chunk_of_step(s)
    lat_of = lambda s: seqs.info(cm(s))[2]
    seq_of = lambda s: seqs.info(cm(s))[1]
    xcol = _C_XR // R_WIDTH
    vec_ld = lambda: pl.BlockSpec((None, None, 1, R_WIDTH), lambda s: (layer, direction, 0, 0))
    mat_ld = lambda: pl.BlockSpec((None, None, R_BLOCKS, R_BW, R_BW), lambda s: (layer, direction, 0, 0, 0))
    kern = functools.partial(_rglru_kernel, seqs=seqs)
    depth = conv_w.shape[0]
    r4 = lambda a: a.reshape(depth, 2, 1, R_WIDTH)
    return pl.pallas_call(
        kern,
        out_shape=(jax.ShapeDtypeStruct((n_rows, R_WIDTH), F32), jax.ShapeDtypeStruct((n_seq, 1, R_WIDTH), F32)),
        grid=(seqs.n_chunks,),
        in_specs=[
            pl.BlockSpec((L, R_WIDTH), lambda s: (cm(s), xcol)),
            pl.BlockSpec((SUBLANES, R_WIDTH), lambda s: (jnp.maximum(cm(s) * hb - 1, 0), xcol)),
            pl.BlockSpec((SUBLANES, R_WIDTH), lambda s: (jnp.minimum((cm(s) + 1) * hb, nb8 - 1), xcol)),
            pl.BlockSpec((None, CONV_W, R_WIDTH), lambda s: (layer, 0, 0)),
            pl.BlockSpec((None, 1, R_WIDTH), lambda s: (layer, 0, 0)),
            mat_ld(), vec_ld(), mat_ld(), vec_ld(), vec_ld(),
            pl.BlockSpec((None, None, None, 1, R_WIDTH), lambda s: (lat_of(s), layer, direction, 0, 0)),
        ],
        out_specs=(pl.BlockSpec((L, R_WIDTH), lambda s: (cm(s), 0)),
                   pl.BlockSpec((None, 1, R_WIDTH), lambda s: (seq_of(s), 0, 0))),
        scratch_shapes=[pltpu.VMEM((L + 2 * SUBLANES, R_WIDTH), F32), pltpu.VMEM((1, R_WIDTH), F32)],
        compiler_params=_cparams(("arbitrary",)),
        name=f"rglru_d{direction}",
    )(z, z, z, conv_w, conv_b.reshape(depth, 1, R_WIDTH), wr, r4(br), wi, r4(bi), r4(lam),
      h0.reshape(h0.shape[0], depth, 2, 1, R_WIDTH))


def _gelu_tanh(x):
    return 0.5 * x * (1.0 + jnp.tanh(math.sqrt(2.0 / math.pi) * (x + 0.044715 * (x * x * x))))


def _out_kernel(x_ref, ac_ref, al_ref, mf_ref, mb_ref, om_ref, rf_ref, rb_ref, yr_ref, mg_ref, gate_ref, w_ref,
                o_ref, *, ctx_tiles, sub):
    is_ctx = pl.program_id(0) < ctx_tiles
    for r0 in range(0, x_ref.shape[0], sub):
        rows = slice(r0, r0 + sub)
        att = jnp.where(is_ctx, ac_ref[rows, :], al_ref[rows, :])
        hs = mf_ref[rows, :] + mb_ref[rows, :]
        parts = [att]
        for h in range(M_HEADS):
            cs = slice(h * M_DK, (h + 1) * M_DK)
            hh = hs[:, cs]
            hn = hh * lax.rsqrt(jnp.mean(hh * hh, axis=-1, keepdims=True) + NORM_EPS) * mg_ref[:, cs]
            parts.append((hn * jax.nn.sigmoid(om_ref[rows, cs])).astype(BF16))
        parts.append(((rf_ref[rows, :] + rb_ref[rows, :]) * _gelu_tanh(yr_ref[rows, :])).astype(BF16))
        mix = jnp.concatenate(parts, axis=1)
        o_ref[rows, :] = x_ref[rows, :] + gate_ref[...] * jnp.dot(mix, w_ref[...], preferred_element_type=F32)


def out_proj(x, a_ctx, a_lat, mf, mb, rf, rb, z, mnorm_g, mod, w_out, nc, tl, tm=512, sub=256):
    n, d = x.shape
    grp = lambda i: _group_of_tile(i * tm, nc, tl)
    ctx_tiles = nc // tm
    lat_tiles = (n - nc) // tm
    rowblk = lambda w, col: pl.BlockSpec((tm, w), lambda i: (i, col))
    kern = functools.partial(_out_kernel, ctx_tiles=ctx_tiles, sub=sub)
    return pl.pallas_call(
        kern,
        out_shape=jax.ShapeDtypeStruct((n, d), F32),
        grid=(n // tm,),
        in_specs=[
            rowblk(d, 0),
            pl.BlockSpec((tm, A_WIDTH), lambda i: (jnp.minimum(i, ctx_tiles - 1), 0)),
            pl.BlockSpec((tm, A_WIDTH), lambda i: (jnp.clip(i - ctx_tiles, 0, lat_tiles - 1), 0)),
            rowblk(M_WIDTH, 0), rowblk(M_WIDTH, 0), rowblk(M_WIDTH, _C_OM // M_WIDTH),
            rowblk(R_WIDTH, 0), rowblk(R_WIDTH, 0), rowblk(R_WIDTH, _C_YR // R_WIDTH),
            pl.BlockSpec((1, M_WIDTH), lambda i: (0, 0)),
            pl.BlockSpec((None, None, 1, d), lambda i: (grp(i), 2, 0, 0)),
            pl.BlockSpec((d, d), lambda i: (0, 0), pipeline_mode=pl.Buffered(1)),
        ],
        out_specs=rowblk(d, 0),
        compiler_params=_cparams(("arbitrary",)),
        name="out_proj",
    )(x, a_ctx, a_lat, mf, mb, z, rf, rb, z, mnorm_g.reshape(1, -1), mod, w_out)


def _bits(x):
    return lax.bitcast_convert_type(x, U32)


def _moe_pre_kernel(x_ref, g_ref, sh_ref, sc_ref, wh_ref, wl_ref, br_ref, xp_ref, rg_ref, re_ref):
    x = x_ref[...]
    y = x * lax.rsqrt(jnp.mean(x * x, axis=-1, keepdims=True) + NORM_EPS) * g_ref[...]
    xn = y * (1.0 + sc_ref[...]) + sh_ref[...]
    xb = xn.astype(BF16)
    xb32 = xb.astype(F32)
    xp_ref[...] = (_bits(xb32[:, HALF:]) & jnp.uint32(0xFFFF0000)) | (_bits(xb32[:, :HALF]) >> 16)

    xl = (xn - xb32).astype(BF16)
    lg = (jnp.dot(xb, wh_ref[...], preferred_element_type=F32) + jnp.dot(xb, wl_ref[...], preferred_element_type=F32)
          + jnp.dot(xl, wh_ref[...], preferred_element_type=F32) + br_ref[...])
    lane = lax.broadcasted_iota(I32, lg.shape, 1).astype(F32)
    ninf = jnp.float32(-jnp.inf)

    def top(mask):
        val = jnp.max(jnp.where(mask, lg, ninf), axis=-1, keepdims=True)
        idx = jnp.min(jnp.where(mask & (lg == val), lane, float(LANES)), axis=-1, keepdims=True)
        return val, idx

    is_g = lane < N_GROUPS
    g_val, g_idx = top(is_g)
    g_w = 1.0 / jnp.sum(jnp.where(is_g, jnp.exp(lg - g_val), 0.0), axis=-1, keepdims=True)
    e_lo = N_GROUPS + g_idx * EXPERTS_PER_GROUP
    in_grp = (lane >= e_lo) & (lane < e_lo + EXPERTS_PER_GROUP)
    v1, i1 = top(in_grp)
    v2, i2 = top(in_grp & (lane != i1))
    t = jnp.exp(v2 - v1)
    w1 = g_w / (1.0 + t)
    rg_ref[...] = jnp.where(lane == 0, w1, jnp.where(lane == 1, w1 * t, 0.0))
    re_ref[...] = jnp.where(lane == 0, i1 - N_GROUPS, jnp.where(lane == 1, i2 - N_GROUPS, 0.0)).astype(I32)


def moe_pre(x, norm_g, mod, w_router, b_router, nc, tl, tm=512):
    n, d = x.shape
    grp = lambda i: _group_of_tile(i * tm, nc, tl)
    lane_out = lambda: pl.BlockSpec((tm, LANES), lambda i: (i, 0))
    w_hi = w_router.astype(BF16)
    return pl.pallas_call(
        _moe_pre_kernel,
        out_shape=(jax.ShapeDtypeStruct((n, HALF), U32), jax.ShapeDtypeStruct((n, LANES), F32),
                   jax.ShapeDtypeStruct((n, LANES), I32)),
        grid=(n // tm,),
        in_specs=[
            pl.BlockSpec((tm, d), lambda i: (i, 0)),
            pl.BlockSpec((1, d), lambda i: (0, 0)),
            pl.BlockSpec((None, None, 1, d), lambda i: (grp(i), 3, 0, 0)),
            pl.BlockSpec((None, None, 1, d), lambda i: (grp(i), 4, 0, 0)),
            pl.BlockSpec((d, LANES), lambda i: (0, 0)),
            pl.BlockSpec((d, LANES), lambda i: (0, 0)),
            pl.BlockSpec((1, LANES), lambda i: (0, 0)),
        ],
        out_specs=(pl.BlockSpec((tm, HALF), lambda i: (i, 0)), lane_out(), lane_out()),
        compiler_params=_cparams(("arbitrary",)),
        name="moe_pre",
    )(x, norm_g.reshape(1, d), mod, mod, w_hi, (w_router - w_hi.astype(F32)).astype(BF16), b_router)


def _expert_changed(be_ref, j):
    return (j == 0) | (be_ref[j] != be_ref[jnp.maximum(j - 1, 0)])


_ROW_STEP = LANES


def _for_row_count(nv, tb, body, out_ref):
    for r in range(_ROW_STEP, tb + 1, _ROW_STEP):
        @pl.when((nv > r - _ROW_STEP) & (nv <= r))
        def _(r=r):
            body(r)

    @pl.when(nv == 0)
    def _():
        out_ref[...] = jnp.zeros(out_ref.shape, out_ref.dtype)


def _moe_up_kernel(be_ref, bs_ref, nv_ref, nx_ref, x_ref, wg_hbm, wu_hbm, h_ref, wbuf, sem, wg_s, wu_s, *, layer):
    j = pl.program_id(0)
    nv = nv_ref[j]
    tb = x_ref.shape[0]

    def weight_copies(e):
        return (pltpu.make_async_copy(wg_hbm.at[layer, e], wbuf.at[0], sem.at[0]),
                pltpu.make_async_copy(wu_hbm.at[layer, e], wbuf.at[1], sem.at[1]))

    @pl.when(j == 0)
    def _():
        for cp in weight_copies(be_ref[0]):
            cp.start()

    @pl.when((nv > 0) & _expert_changed(be_ref, j))
    def _():
        for cp in weight_copies(be_ref[j]):
            cp.wait()
        wg_s[...] = wbuf[0].astype(BF16)
        wu_s[...] = wbuf[1].astype(BF16)

        @pl.when(nx_ref[j] >= 0)
        def _():
            for cp in weight_copies(nx_ref[j]):
                cp.start()

    def body(r):
        w = x_ref[0:r, :]
        lo = lax.bitcast_convert_type(w << 16, F32).astype(BF16)
        hi = lax.bitcast_convert_type(w & jnp.uint32(0xFFFF0000), F32).astype(BF16)

        def mm(w_s):
            return (jnp.dot(lo, w_s[0:HALF, :], preferred_element_type=F32)
                    + jnp.dot(hi, w_s[HALF:, :], preferred_element_type=F32))

        gt, up = mm(wg_s), mm(wu_s)
        h_ref[0:r, :] = (gt * jax.nn.sigmoid(gt) * up).astype(BF16)
        if r < tb:
            h_ref[r:tb, :] = jnp.zeros((tb - r, h_ref.shape[1]), BF16)

    _for_row_count(nv, tb, body, h_ref)


def _pack_halves(y):
    yb = y.astype(BF16).astype(F32)
    return (_bits(yb[:, HALF:]) & jnp.uint32(0xFFFF0000)) | (_bits(yb[:, :HALF]) >> 16)


def _unpack_halves(w):
    return (lax.bitcast_convert_type(w << 16, F32), lax.bitcast_convert_type(w & jnp.uint32(0xFFFF0000), F32))


def _moe_down_kernel(be_ref, bs_ref, nv_ref, nx_ref, h_ref, wd_hbm, y_ref, wbuf, sem, wd_s, *, layer):
    j = pl.program_id(0)
    nv = nv_ref[j]
    tb = h_ref.shape[0]

    def weight_copy(e):
        return pltpu.make_async_copy(wd_hbm.at[layer, e], wbuf, sem.at[0])

    @pl.when(j == 0)
    def _():
        weight_copy(be_ref[0]).start()

    @pl.when((nv > 0) & _expert_changed(be_ref, j))
    def _():
        weight_copy(be_ref[j]).wait()
        wd_s[...] = wbuf[...].astype(BF16)

        @pl.when(nx_ref[j] >= 0)
        def _():
            weight_copy(nx_ref[j]).start()

    def body(r):
        y_ref[0:r, :] = _pack_halves(jnp.dot(h_ref[0:r, :], wd_s[...], preferred_element_type=F32))
        if r < tb:
            y_ref[r:tb, :] = jnp.zeros((tb - r, y_ref.shape[1]), U32)

    _for_row_count(nv, tb, body, y_ref)


def moe_experts(xb, blk_e, blk_src, n_valid, nxt_e, w_gate, w_up, w_down, layer, tb=MOE_TB):
    rows = xb.shape[0]
    n_blocks = rows // tb
    d, de = w_gate.shape[-2:]
    any_spec = pl.BlockSpec(memory_space=pl.ANY)
    h = pl.pallas_call(
        functools.partial(_moe_up_kernel, layer=layer),
        out_shape=jax.ShapeDtypeStruct((rows, de), BF16),
        grid_spec=pltpu.PrefetchScalarGridSpec(
            num_scalar_prefetch=4,
            grid=(n_blocks,),
            in_specs=[pl.BlockSpec((tb, HALF), lambda j, be, bs, nv, nx: (bs[j], 0)), any_spec, any_spec],
            out_specs=pl.BlockSpec((tb, de), lambda j, be, bs, nv, nx: (j, 0)),
            scratch_shapes=[pltpu.VMEM((2, d, de), F32), pltpu.SemaphoreType.DMA((2,)),
                            pltpu.VMEM((d, de), BF16), pltpu.VMEM((d, de), BF16)],
        ),
        compiler_params=_cparams(("arbitrary",)),
        name="moe_up",
    )(blk_e, blk_src, n_valid, nxt_e, xb, w_gate, w_up)
    return pl.pallas_call(
        functools.partial(_moe_down_kernel, layer=layer),
        out_shape=jax.ShapeDtypeStruct((rows, HALF), U32),
        grid_spec=pltpu.PrefetchScalarGridSpec(
            num_scalar_prefetch=4,
            grid=(n_blocks,),
            in_specs=[pl.BlockSpec((tb, de), lambda j, be, bs, nv, nx: (bs[j], 0)), any_spec],
            out_specs=pl.BlockSpec((tb, HALF), lambda j, be, bs, nv, nx: (j, 0)),
            scratch_shapes=[pltpu.VMEM((de, d), F32), pltpu.SemaphoreType.DMA((1,)), pltpu.VMEM((de, d), BF16)],
        ),
        compiler_params=_cparams(("arbitrary",)),
        name="moe_down",
    )(blk_e, blk_src, n_valid, nxt_e, h, w_down)


def moe_layout(eid, tb=MOE_TB):
    n = eid.shape[0]
    s_len = n * TOP_K
    flat_e = eid.reshape(s_len)
    onehot = (flat_e[:, None] == jnp.arange(N_EXPERTS, dtype=I32)[None, :]).astype(I32)
    rank = jnp.sum((jnp.cumsum(onehot, axis=0) - onehot) * onehot, axis=1)
    counts = jnp.sum(onehot, axis=0)
    nblk_e = (counts + tb - 1) // tb
    blk_end = jnp.cumsum(nblk_e)
    blk_start = blk_end - nblk_e
    dest = jnp.sum(onehot * blk_start[None, :], axis=1) * tb + rank
    n_blocks = s_len // tb + N_EXPERTS
    n_used = blk_end[-1]
    jc = jnp.minimum(jnp.arange(n_blocks, dtype=I32), n_used - 1)
    blk_e = jnp.minimum(jnp.sum((blk_end[None, :] <= jc[:, None]).astype(I32), axis=1), N_EXPERTS - 1).astype(I32)
    used = jnp.arange(n_blocks, dtype=I32) < n_used
    nxt_blk = blk_end[blk_e]
    nxt_e = jnp.where(used & (nxt_blk < n_used), blk_e[jnp.minimum(nxt_blk, n_blocks - 1)], -1).astype(I32)
    n_valid = jnp.where(used, jnp.clip(counts[blk_e] - (jc - blk_start[blk_e]) * tb, 0, tb), 0).astype(I32)
    pad_e = nblk_e * tb - counts
    pad_incl = jnp.cumsum(pad_e)
    n_fill = n_blocks * tb - s_len
    k = jnp.arange(n_fill, dtype=I32)
    e_k = jnp.minimum(jnp.sum((pad_incl[None, :] <= k[:, None]).astype(I32), axis=1), N_EXPERTS - 1)
    oh_k = (e_k[:, None] == jnp.arange(N_EXPERTS, dtype=I32)[None, :]).astype(I32)
    pick = lambda v: jnp.sum(oh_k * v[None, :], axis=1)
    in_used = k < pad_incl[-1]
    fill_pos = jnp.where(in_used, pick(blk_start * tb + counts) + k - pick(pad_incl - pad_e),
                         n_used * tb + k - pad_incl[-1])
    keys = jnp.concatenate([dest, fill_pos])
    vals = jnp.concatenate([jnp.arange(s_len, dtype=I32) // TOP_K, k % n])
    slot_tok = lax.sort_key_val(keys, vals)[1]
    return dest.reshape(n, TOP_K), slot_tok, blk_e, jc, n_valid, nxt_e


def _take_rows(a, idx):
    return a.at[idx].get(mode="promise_in_bounds")


def _final_norm_kernel(x_ref, y0_ref, y1_ref, gt_ref, g2_ref, g_ref, oc_ref, ol_ref, xs_ref, *, ctx_tiles):
    _combine_into(xs_ref, x_ref, y0_ref, y1_ref, gt_ref, g2_ref)
    x = xs_ref[...]
    y = x * lax.rsqrt(jnp.mean(x * x, axis=-1, keepdims=True) + NORM_EPS) * g_ref[...]

    @pl.when(pl.program_id(0) < ctx_tiles)
    def _():
        oc_ref[...] = y

    @pl.when(pl.program_id(0) >= ctx_tiles)
    def _():
        ol_ref[...] = y


def final_norm(x, pend, g, nc, tl, tm=512):
    n, d = x.shape
    y0, y1, gates, mod_prev = pend
    grp = lambda i: _group_of_tile(i * tm, nc, tl)
    ctx_tiles, lat_tiles = nc // tm, (n - nc) // tm
    row = lambda w: pl.BlockSpec((tm, w), lambda i: (i, 0))
    return pl.pallas_call(
        functools.partial(_final_norm_kernel, ctx_tiles=ctx_tiles),
        out_shape=(jax.ShapeDtypeStruct((nc, d), F32), jax.ShapeDtypeStruct((n - nc, d), F32)),
        grid=(n // tm,),
        in_specs=[row(d), row(HALF), row(HALF), row(LANES),
                  pl.BlockSpec((None, None, 1, d), lambda i: (grp(i), 5, 0, 0)),
                  pl.BlockSpec((1, d), lambda i: (0, 0))],
        out_specs=(pl.BlockSpec((tm, d), lambda i: (jnp.minimum(i, ctx_tiles - 1), 0)),
                   pl.BlockSpec((tm, d), lambda i: (jnp.clip(i - ctx_tiles, 0, lat_tiles - 1), 0))),
        scratch_shapes=[pltpu.VMEM((tm, d), F32)],
        compiler_params=_cparams(("arbitrary",)),
        name="final_norm",
    )(x, y0, y1, gates, mod_prev, g.reshape(1, d))


def _pad_lanes(a):
    return jnp.pad(a, ((0, 0), (0, LANES - a.shape[1])))


def kernel(x_prompt, x_sample, cache_k, cache_v, state_mlstm_C, state_mlstm_n, state_mlstm_m, state_rglru_h, c, c_ctx, ada_w, ada_b, norm1_g, w_in, attn_sink, mlstm_gate_b, mlstm_norm_g, rg_conv_w, rg_conv_b, rg_wr, rg_br, rg_wi, rg_bi, rg_lam, w_out, norm2_g, router_wg, router_bg, router_we, router_be, exp_w_gate, exp_w_up, exp_w_down, final_norm_g):
    bc, tc, d = x_prompt.shape
    bl, tl, _ = x_sample.shape
    depth = w_in.shape[0]
    past = cache_k.shape[2]
    nc, nl = bc * tc, bl * tl
    tm = math.gcd(1024, math.gcd(tl, nc))

    x = jnp.concatenate([x_prompt.reshape(nc, d), x_sample.reshape(nl, d)], axis=0)
    cvec = jnp.concatenate([c_ctx[None, :], c, jnp.zeros((SUBLANES - 1 - bl, d), F32)], axis=0)
    mods = adaln_all(cvec, ada_w, ada_b).reshape(depth, SUBLANES, 6, 1, d)
    rope_c, rope_s = rope_tables(tl)
    gsplit = Z_MAIN - 2 * R_WIDTH
    w_main = jnp.concatenate([w_in[:, :, :gsplit], w_in[:, :, gsplit + M_GATES:]], axis=2).astype(BF16)
    w_gcol = jnp.pad(w_in[:, :, gsplit:gsplit + M_GATES], ((0, 0), (0, 0), (0, LANES - M_GATES))).astype(BF16)
    w_out_b = w_out.astype(BF16)
    ck = cache_k.reshape(bl, depth, past, KV_WIDTH)
    cv = cache_v.reshape(bl, depth, past, KV_WIDTH)
    cn0 = jnp.concatenate([state_mlstm_C, jnp.broadcast_to(state_mlstm_n[..., None], state_mlstm_C.shape)], axis=-1)

    new_k, new_v, new_c, new_n, new_m, new_h = [], [], [], [], [], []
    pend = None
    for l in range(depth):
        mod = mods[l]
        if pend is None:
            z, zg = in_proj(x, None, norm1_g[l], mod, w_main[l], w_gcol[l], nc, tl, tm)
        else:
            z, zg, x = in_proj(x, pend, norm1_g[l], mod, w_main[l], w_gcol[l], nc, tl, tm)

        a_ctx = attention_context(z, attn_sink[l], bc, tc)
        a_lat = attention_latent(z, ck, cv, l, attn_sink[l], rope_c, rope_s, nc, bl, tl)

        rows, tiles, kt = mlstm_gates(z, zg, _pad_lanes(mlstm_gate_b[l].reshape(1, M_GATES)), tile=tm)
        res = mlstm_scan(z, rows, tiles, kt, cn0, state_mlstm_m, l, bc, tc, bl, tl)
        mh = res[0:2]
        for dr in range(2):
            cnf, mf = res[2 + 2 * dr:4 + 2 * dr]
            new_c.append(cnf[:bc, :, :, :M_DK])
            new_n.append(cnf[:bc, :, :, M_DK])
            new_m.append(mf[:bc, :M_HEADS, 0])
        rh = []
        for dr in range(2):
            rd, hf = rglru_direction(z, rg_conv_w, rg_conv_b, rg_wr, rg_br, rg_wi, rg_bi, rg_lam, state_rglru_h,
                                     l, dr, bc, tc, bl, tl)
            rh.append(rd)
            new_h.append(hf[:bc, 0])
        new_k.append(z[:nc, _C_K:_C_K + KV_WIDTH].reshape(bc, tc, KV_HEADS, HEAD_DIM))
        new_v.append(z[:nc, _C_V:_C_V + KV_WIDTH].reshape(bc, tc, KV_HEADS, HEAD_DIM))

        x = out_proj(x, a_ctx, a_lat, mh[0], mh[1], rh[0], rh[1], z, mlstm_norm_g[l], mod, w_out_b[l], nc, tl)

        w_router = _pad_lanes(jnp.concatenate([router_wg[l], router_we[l]], axis=1))
        b_router = _pad_lanes(jnp.concatenate([router_bg[l], router_be[l]])[None, :])
        xp, route_g, route_e = moe_pre(x, norm2_g[l], mod, w_router, b_router, nc, tl)
        dest, slot_tok, blk_e, blk_src, n_valid, nxt_e = moe_layout(route_e[:, :TOP_K])
        yb = moe_experts(_take_rows(xp, slot_tok), blk_e, blk_src, n_valid, nxt_e, exp_w_gate, exp_w_up, exp_w_down, l)
        pend = (_take_rows(yb, dest[:, 0]), _take_rows(yb, dest[:, 1]), route_g, mod)

    y_ctx, y_lat = final_norm(x, pend, final_norm_g, nc, tl)
    stack2 = lambda parts: jnp.stack([jnp.stack(parts[2 * l:2 * l + 2], axis=1) for l in range(depth)], axis=1)
    return (y_ctx.reshape(bc, tc, d), y_lat.reshape(bl, tl, d),
            jnp.stack(new_k, axis=1), jnp.stack(new_v, axis=1),
            stack2(new_c), stack2(new_n), stack2(new_m), stack2(new_h))
```

```python
import functools
import math

import jax
import jax.numpy as jnp
from jax import lax
from jax.experimental import pallas as pl
from jax.experimental.pallas import tpu as pltpu

F32 = jnp.float32
BF16 = jnp.bfloat16
U32 = jnp.uint32
I32 = jnp.int32

D_MODEL = 2048
HEAD_DIM = 128
A_WIDTH = D_MODEL // 2
N_HEADS = A_WIDTH // HEAD_DIM
KV_HEADS = 2
GQA_GROUP = N_HEADS // KV_HEADS
KV_WIDTH = KV_HEADS * HEAD_DIM
WINDOW = 128
Q_BLOCK = 128
GRID_W = 64
ROPE_BASE = 10000.0
ROPE_AXIS = HEAD_DIM // 2
ATTN_SCALE = HEAD_DIM ** -0.5
NEG_INF = -1e30
M_WIDTH = D_MODEL // 4
M_HEADS = 4
M_DK = M_WIDTH // M_HEADS
M_GATES = 2 * 2 * M_HEADS
R_WIDTH = D_MODEL // 4
R_BLOCKS = 4
R_BW = R_WIDTH // R_BLOCKS
CONV_W = 4
RG_C = 8.0
N_GROUPS = 4
EXPERTS_PER_GROUP = 8
N_EXPERTS = N_GROUPS * EXPERTS_PER_GROUP
TOP_K = 2
D_EXPERT = D_MODEL // 2
NORM_EPS = 1e-6

LANES = 128
SUBLANES = 8
Z_MAIN = A_WIDTH + 2 * KV_WIDTH + 4 * M_WIDTH + 2 * R_WIDTH
_C_Q, _C_K, _C_V = 0, A_WIDTH, A_WIDTH + KV_WIDTH
_C_QM = A_WIDTH + 2 * KV_WIDTH
_C_KM, _C_VM, _C_OM = _C_QM + M_WIDTH, _C_QM + 2 * M_WIDTH, _C_QM + 3 * M_WIDTH
_C_XR, _C_YR = _C_QM + 4 * M_WIDTH, _C_QM + 4 * M_WIDTH + R_WIDTH

M_CHUNK = LANES
R_CHUNK = 256
MOE_TB = 512
VMEM_LIMIT = 56 * 1024 * 1024
HALF = D_MODEL // 2


def _cparams(sem):
    return pltpu.CompilerParams(dimension_semantics=sem, vmem_limit_bytes=VMEM_LIMIT)


def _group_of_tile(row0, nc, tl):
    return jnp.where(row0 < nc, 0, 1 + (jnp.maximum(row0 - nc, 0)) // tl)


def _adaln_kernel(c_ref, w_ref, b_ref, o_ref):
    c = c_ref[...]
    s = (c * jax.nn.sigmoid(c)).astype(BF16)
    o_ref[...] = jnp.dot(s, w_ref[...].astype(BF16), preferred_element_type=F32) + b_ref[...]


def adaln_all(cvec, ada_w, ada_b, tn=1024):
    depth, d, d6 = ada_w.shape
    return pl.pallas_call(
        _adaln_kernel,
        out_shape=jax.ShapeDtypeStruct((depth, SUBLANES, d6), F32),
        grid=(depth, d6 // tn),
        in_specs=[
            pl.BlockSpec((SUBLANES, d), lambda l, j: (0, 0)),
            pl.BlockSpec((None, d, tn), lambda l, j: (l, 0, j)),
            pl.BlockSpec((None, 1, tn), lambda l, j: (l, 0, j)),
        ],
        out_specs=pl.BlockSpec((None, SUBLANES, tn), lambda l, j: (l, 0, j)),
        compiler_params=_cparams(("arbitrary", "arbitrary")),
        name="adaln",
    )(cvec, ada_w, ada_b.reshape(depth, 1, d6))


def _combine_into(o_ref, x_ref, y0_ref, y1_ref, gt_ref, gate_ref):
    g0, g1 = gt_ref[:, 0:1], gt_ref[:, 1:2]
    for half, (a0, a1) in enumerate(zip(_unpack_halves(y0_ref[...]), _unpack_halves(y1_ref[...]))):
        cs = slice(half * HALF, (half + 1) * HALF)
        o_ref[:, cs] = x_ref[:, cs] + gate_ref[:, cs] * (g0 * a0 + g1 * a1)


def _in_kernel(*refs, combine, n_tiles):
    if combine:
        (x_hbm, y0_hbm, y1_hbm, gt_ref, g2_ref, g_ref, sh_ref, sc_ref, w_ref, wg_ref, z_ref, zg_ref, xo_ref,
         xn_ref, xbuf, y0buf, y1buf, sem) = refs
    else:
        x_ref, g_ref, sh_ref, sc_ref, w_ref, wg_ref, z_ref, zg_ref, xn_ref = refs
    i = pl.program_id(0)
    tm = xn_ref.shape[0]

    def row_copies(t):
        rows = pl.ds(pl.multiple_of(t * tm, tm), tm)
        return (pltpu.make_async_copy(x_hbm.at[rows], xbuf, sem.at[0]),
                pltpu.make_async_copy(y0_hbm.at[rows], y0buf, sem.at[1]),
                pltpu.make_async_copy(y1_hbm.at[rows], y1buf, sem.at[2]))

    @pl.when(pl.program_id(1) == 0)
    def _():
        if combine:
            @pl.when(i == 0)
            def _():
                for cp in row_copies(0):
                    cp.start()

            for cp in row_copies(i):
                cp.wait()
            _combine_into(xo_ref, xbuf, y0buf, y1buf, gt_ref, g2_ref)

            @pl.when(i + 1 < n_tiles)
            def _():
                for cp in row_copies(i + 1):
                    cp.start()

            x = xo_ref[...]
        else:
            x = x_ref[...]
        y = x * lax.rsqrt(jnp.mean(x * x, axis=-1, keepdims=True) + NORM_EPS) * g_ref[...]
        xn = (y * (1.0 + sc_ref[...]) + sh_ref[...]).astype(BF16)
        xn_ref[...] = xn
        zg_ref[...] = jnp.dot(xn, wg_ref[...], preferred_element_type=F32)

    z_ref[...] = jnp.dot(xn_ref[...], w_ref[...], preferred_element_type=F32)


def in_proj(x, pend, norm_g, mod, w_main, w_gate, nc, tl, tm, tn=768):
    n, d = x.shape
    zw = w_main.shape[1]
    grp = lambda i: _group_of_tile(i * tm, nc, tl)
    row = lambda w: pl.BlockSpec((tm, w), lambda i, j: (i, 0))
    modrow = lambda k: pl.BlockSpec((None, None, 1, d), lambda i, j: (grp(i), k, 0, 0))
    out_shape = [jax.ShapeDtypeStruct((n, zw), F32), jax.ShapeDtypeStruct((n, LANES), F32)]
    out_specs = [pl.BlockSpec((tm, tn), lambda i, j: (i, j)), row(LANES)]
    scratch = [pltpu.VMEM((tm, d), BF16)]
    if pend is None:
        in_specs, args = [row(d)], [x]
    else:
        y0, y1, gates, mod_prev = pend
        any_spec = pl.BlockSpec(memory_space=pl.ANY)
        in_specs = [any_spec, any_spec, any_spec, row(LANES), modrow(5)]
        args = [x, y0, y1, gates, mod_prev]
        out_shape.append(jax.ShapeDtypeStruct((n, d), F32))
        out_specs.append(row(d))
        scratch += [pltpu.VMEM((tm, d), F32), pltpu.VMEM((tm, HALF), U32), pltpu.VMEM((tm, HALF), U32),
                    pltpu.SemaphoreType.DMA((3,))]
    in_specs += [pl.BlockSpec((1, d), lambda i, j: (0, 0)), modrow(0), modrow(1),
                 pl.BlockSpec((d, tn), lambda i, j: (0, j)), pl.BlockSpec((d, LANES), lambda i, j: (0, 0))]
    args += [norm_g.reshape(1, d), mod, mod, w_main, w_gate]
    return pl.pallas_call(
        functools.partial(_in_kernel, combine=pend is not None, n_tiles=n // tm),
        out_shape=tuple(out_shape),
        grid=(n // tm, zw // tn),
        in_specs=in_specs,
        out_specs=tuple(out_specs),
        scratch_shapes=scratch,
        compiler_params=_cparams(("arbitrary", "arbitrary")),
        name="in_proj",
    )(*args)


def _rope(x, c, s):
    lane = lax.broadcasted_iota(I32, x.shape, 1)
    half = ROPE_AXIS // 2
    partner = jnp.where((lane & (ROPE_AXIS - 1)) < half, pltpu.roll(x, HEAD_DIM - half, 1), pltpu.roll(x, half, 1))
    return x * c + partner * s


def _softmax_pv(parts, sink_col):
    m = sink_col
    for s, _ in parts:
        m = jnp.maximum(m, jnp.max(s, axis=-1, keepdims=True))
    den = jnp.exp(sink_col - m)
    acc = None
    for s, v in parts:
        p = jnp.exp(s - m)
        den = den + jnp.sum(p, axis=-1, keepdims=True)
        pv = jnp.dot(p.astype(BF16), v, preferred_element_type=F32)
        acc = pv if acc is None else acc + pv
    return acc / den


def _qk(q, k):
    return lax.dot_general(q, k, (((1,), (1,)), ((), ())), preferred_element_type=F32) * ATTN_SCALE


def _sink_column(sink_ref, n, rows):
    ridx = lax.broadcasted_iota(I32, (GQA_GROUP * rows, 1), 0)
    col = jnp.full((GQA_GROUP * rows, 1), sink_ref[n * GQA_GROUP], F32)
    for g in range(1, GQA_GROUP):
        col = jnp.where(ridx >= g * rows, sink_ref[n * GQA_GROUP + g], col)
    return col


def _attn_ctx_kernel(sink_ref, q_ref, k_ref, v_ref, o_ref):
    t = q_ref.shape[0]
    for n in range(KV_HEADS):
        k = k_ref[:, n * HEAD_DIM:(n + 1) * HEAD_DIM].astype(BF16)
        v = v_ref[:, n * HEAD_DIM:(n + 1) * HEAD_DIM].astype(BF16)
        q = jnp.concatenate(
            [q_ref[:, (n * GQA_GROUP + g) * HEAD_DIM:(n * GQA_GROUP + g + 1) * HEAD_DIM] for g in range(GQA_GROUP)],
            axis=0).astype(BF16)
        out = _softmax_pv([(_qk(q, k), v)], _sink_column(sink_ref, n, t))
        for g in range(GQA_GROUP):
            h = n * GQA_GROUP + g
            o_ref[:, h * HEAD_DIM:(h + 1) * HEAD_DIM] = out[g * t:(g + 1) * t].astype(o_ref.dtype)


def attention_context(z, sink, b, t):
    return pl.pallas_call(
        _attn_ctx_kernel,
        out_shape=jax.ShapeDtypeStruct((b * t, A_WIDTH), BF16),
        grid=(b,),
        in_specs=[
            pl.BlockSpec(memory_space=pltpu.SMEM),
            pl.BlockSpec((t, A_WIDTH), lambda i: (i, _C_Q // A_WIDTH)),
            pl.BlockSpec((t, KV_WIDTH), lambda i: (i, _C_K // KV_WIDTH)),
            pl.BlockSpec((t, KV_WIDTH), lambda i: (i, _C_V // KV_WIDTH)),
        ],
        out_specs=pl.BlockSpec((t, A_WIDTH), lambda i: (i, 0)),
        compiler_params=_cparams(("arbitrary",)),
        name="attn_ctx",
    )(sink, z, z, z)


def _attn_lat_kernel(sink_ref, q_ref, k_ref, v_ref, kc_ref, vc_ref, cq_ref, sq_ref, ck_ref, sk_ref, o_ref,
                     kr_ref, vp_ref, *, t):
    i = pl.program_id(1)
    rope_rows = 512

    @pl.when(i == 0)
    def _():
        zpad = jnp.zeros((WINDOW, KV_WIDTH), BF16)
        kr_ref[0:WINDOW, :] = zpad
        kr_ref[WINDOW + t:2 * WINDOW + t, :] = zpad
        vp_ref[0:WINDOW, :] = zpad
        vp_ref[WINDOW + t:2 * WINDOW + t, :] = zpad

        def body(c, carry):
            r0 = pl.multiple_of(c * rope_rows, rope_rows)
            cs, sn = ck_ref[pl.ds(r0, rope_rows), :], sk_ref[pl.ds(r0, rope_rows), :]
            for n in range(KV_HEADS):
                kk = k_ref[pl.ds(r0, rope_rows), n * HEAD_DIM:(n + 1) * HEAD_DIM]
                kr_ref[pl.ds(WINDOW + r0, rope_rows), n * HEAD_DIM:(n + 1) * HEAD_DIM] = _rope(kk, cs, sn).astype(BF16)
            vp_ref[pl.ds(WINDOW + r0, rope_rows), :] = v_ref[pl.ds(r0, rope_rows), :].astype(BF16)
            return carry

        lax.fori_loop(0, t // rope_rows, body, 0)

    span = Q_BLOCK + 2 * WINDOW
    rows = GQA_GROUP * Q_BLOCK
    r = lax.broadcasted_iota(I32, (rows, span), 0) & (Q_BLOCK - 1)
    c = lax.broadcasted_iota(I32, (rows, span), 1)
    kpos = (i - 1) * Q_BLOCK + c
    mask = (c >= r) & (c <= r + 2 * WINDOW) & (kpos >= 0) & (kpos < t)
    w0 = pl.multiple_of(i * Q_BLOCK, Q_BLOCK)
    cq, sq = cq_ref[...], sq_ref[...]
    for n in range(KV_HEADS):
        hs = slice(n * HEAD_DIM, (n + 1) * HEAD_DIM)
        q = jnp.concatenate(
            [_rope(q_ref[:, (n * GQA_GROUP + g) * HEAD_DIM:(n * GQA_GROUP + g + 1) * HEAD_DIM], cq, sq)
             for g in range(GQA_GROUP)], axis=0).astype(BF16)
        s_win = jnp.where(mask, _qk(q, kr_ref[pl.ds(w0, span), hs]), NEG_INF)
        s_ctx = _qk(q, kc_ref[:, hs].astype(BF16))
        out = _softmax_pv([(s_win, vp_ref[pl.ds(w0, span), hs]), (s_ctx, vc_ref[:, hs].astype(BF16))],
                          _sink_column(sink_ref, n, Q_BLOCK))
        for g in range(GQA_GROUP):
            h = n * GQA_GROUP + g
            o_ref[:, h * HEAD_DIM:(h + 1) * HEAD_DIM] = out[g * Q_BLOCK:(g + 1) * Q_BLOCK].astype(o_ref.dtype)


def attention_latent(z, cache_k, cache_v, layer, sink, rope_c, rope_s, row0, b, t):
    nqb = t // Q_BLOCK
    past = cache_k.shape[2]
    qb0 = row0 // Q_BLOCK
    tb0 = row0 // t
    kern = functools.partial(_attn_lat_kernel, t=t)
    return pl.pallas_call(
        kern,
        out_shape=jax.ShapeDtypeStruct((b * t, A_WIDTH), BF16),
        grid=(b, nqb),
        in_specs=[
            pl.BlockSpec(memory_space=pltpu.SMEM),
            pl.BlockSpec((Q_BLOCK, A_WIDTH), lambda bi, i: (qb0 + bi * nqb + i, _C_Q // A_WIDTH)),
            pl.BlockSpec((t, KV_WIDTH), lambda bi, i: (tb0 + bi, _C_K // KV_WIDTH)),
            pl.BlockSpec((t, KV_WIDTH), lambda bi, i: (tb0 + bi, _C_V // KV_WIDTH)),
            pl.BlockSpec((None, None, past, KV_WIDTH), lambda bi, i: (bi, layer, 0, 0)),
            pl.BlockSpec((None, None, past, KV_WIDTH), lambda bi, i: (bi, layer, 0, 0)),
            pl.BlockSpec((Q_BLOCK, HEAD_DIM), lambda bi, i: (i, 0)),
            pl.BlockSpec((Q_BLOCK, HEAD_DIM), lambda bi, i: (i, 0)),
            pl.BlockSpec((t, HEAD_DIM), lambda bi, i: (0, 0)),
            pl.BlockSpec((t, HEAD_DIM), lambda bi, i: (0, 0)),
        ],
        out_specs=pl.BlockSpec((Q_BLOCK, A_WIDTH), lambda bi, i: (bi * nqb + i, 0)),
        scratch_shapes=[pltpu.VMEM((t + 2 * WINDOW, KV_WIDTH), BF16), pltpu.VMEM((t + 2 * WINDOW, KV_WIDTH), BF16)],
        compiler_params=_cparams(("arbitrary", "arbitrary")),
        name="attn_lat",
    )(sink, z, z, z, cache_k, cache_v, rope_c, rope_s, rope_c, rope_s)


def rope_tables(t):
    pos = jnp.arange(t)
    inv = ROPE_BASE ** (-jnp.arange(0, ROPE_AXIS, 2, dtype=F32) / ROPE_AXIS)

    def cs(p):
        ang = p.astype(F32)[:, None] * inv[None, :]
        return jnp.cos(ang), jnp.sin(ang)

    cr, sr = cs(pos // GRID_W)
    cc, sc = cs(pos % GRID_W)
    return (jnp.concatenate([cr, cr, cc, cc], axis=-1), jnp.concatenate([-sr, sr, -sc, sc], axis=-1))


class _Seqs:
    def __init__(self, n_ctx, t_ctx, n_lat, t_lat, chunk, reverse):
        self.n_ctx, self.t_ctx, self.n_lat, self.t_lat = n_ctx, t_ctx, n_lat, t_lat
        self.chunk, self.reverse = chunk, reverse
        self.cpc, self.cpl = t_ctx // chunk, t_lat // chunk
        self.ctx_chunks = n_ctx * self.cpc
        self.n_chunks = self.ctx_chunks + n_lat * self.cpl

    def chunk_of_step(self, s):
        return (self.n_chunks - 1 - s) if self.reverse else s

    def info(self, g):
        is_ctx = g < self.ctx_chunks
        gl = jnp.maximum(g - self.ctx_chunks, 0)
        gc = jnp.minimum(g, self.ctx_chunks - 1)
        pos = jnp.where(is_ctx, gc % self.cpc, gl % self.cpl)
        per = jnp.where(is_ctx, self.cpc, self.cpl)
        lat = gl // self.cpl
        seq = jnp.where(is_ctx, gc // self.cpc, self.n_ctx + lat)
        head, tail = pos == 0, pos == per - 1
        return is_ctx, seq, lat, (tail if self.reverse else head), (head if self.reverse else tail)


def _lane_scan(x, op, reverse):
    n = x.shape[1]
    lane = lax.broadcasted_iota(I32, x.shape, 1)
    s = 1
    while s < n:
        if reverse:
            x = jnp.where(lane < n - s, op(x, pltpu.roll(x, n - s, 1)), x)
        else:
            x = jnp.where(lane >= s, op(x, pltpu.roll(x, s, 1)), x)
        s *= 2
    return x


_GROWS = 2 * SUBLANES


def _mlstm_gate_kernel(g_ref, gb_ref, k_ref, rows_ref, tiles_ref, kt_ref):
    L = M_CHUNK
    nck = g_ref.shape[0] // L
    g = g_ref[...] + gb_ref[...]
    s = jnp.concatenate([g[c * L:(c + 1) * L].T[0:_GROWS] for c in range(nck)], axis=0)
    row = lax.broadcasted_iota(I32, s.shape, 0)
    is_rev = (row & SUBLANES) != 0
    lf = jax.nn.log_sigmoid(s)
    f = jnp.where(is_rev, _lane_scan(lf, jnp.add, True), _lane_scan(lf, jnp.add, False))
    f = pltpu.roll(f, s.shape[0] - M_HEADS, 0)
    r = s - f
    cm = jnp.where(is_rev, _lane_scan(r, jnp.maximum, True), _lane_scan(r, jnp.maximum, False))
    rows_ref[...] = r
    pad = jnp.zeros((LANES - 2 * _GROWS, L), F32)
    scale = M_DK ** -0.5
    for c in range(nck):
        sl = slice(c * _GROWS, (c + 1) * _GROWS)
        cols = jnp.concatenate([cm[sl], f[sl], pad], axis=0).T
        for d in range(2):
            for h in range(M_HEADS):
                gi = d * SUBLANES + h
                tiles_ref[c, d, h] = jnp.broadcast_to(cols[:, gi:gi + 1], (L, LANES))
                tiles_ref[c, d, M_HEADS + h] = jnp.broadcast_to(cols[:, _GROWS + gi:_GROWS + gi + 1], (L, LANES))
        for h in range(M_HEADS):
            kt_ref[c, h] = (k_ref[c * L:(c + 1) * L, h * M_DK:(h + 1) * M_DK] * scale).T


def mlstm_gates(z, zg, gate_b, tile=512):
    n = zg.shape[0]
    L = M_CHUNK
    nck = tile // L
    return pl.pallas_call(
        _mlstm_gate_kernel,
        out_shape=(jax.ShapeDtypeStruct((n // L * _GROWS, L), F32),
                   jax.ShapeDtypeStruct((n // L, 2, 2 * M_HEADS, L, LANES), F32),
                   jax.ShapeDtypeStruct((n // L, M_HEADS, M_DK, L), F32)),
        grid=(n // tile,),
        in_specs=[pl.BlockSpec((tile, LANES), lambda i: (i, 0)), pl.BlockSpec((1, LANES), lambda i: (0, 0)),
                  pl.BlockSpec((tile, M_WIDTH), lambda i: (i, _C_KM // M_WIDTH))],
        out_specs=(pl.BlockSpec((nck * _GROWS, L), lambda i: (i, 0)),
                   pl.BlockSpec((nck, 2, 2 * M_HEADS, L, LANES), lambda i: (i, 0, 0, 0, 0)),
                   pl.BlockSpec((nck, M_HEADS, M_DK, L), lambda i: (i, 0, 0, 0))),
        compiler_params=_cparams(("arbitrary",)),
        name="mlstm_gates",
    )(zg, gate_b, z)


def _mlstm_kernel(m0_ref, qf_ref, vf_ref, ktf_ref, rf_ref, tf_ref, qb_ref, vb_ref, ktb_ref, rb_ref, tb_ref,
                  cn0f_ref, cn0b_ref, hf_ref, hb_ref, ocnf_ref, omf_ref, ocnb_ref, omb_ref,
                  cn_s, m_s, *, seqs, layer, depth):
    L = M_CHUNK
    step = pl.program_id(0)
    ti = lax.broadcasted_iota(I32, (L, L), 0)
    si = lax.broadcasted_iota(I32, (L, L), 1)
    ones = jnp.ones((L, M_DK), BF16)
    streams = (
        (0, seqs[0], qf_ref, vf_ref, ktf_ref, rf_ref, tf_ref, cn0f_ref, hf_ref, ocnf_ref, omf_ref),
        (1, seqs[1], qb_ref, vb_ref, ktb_ref, rb_ref, tb_ref, cn0b_ref, hb_ref, ocnb_ref, omb_ref),
    )
    for d, sq, q_ref, v_ref, kt_ref, rows_ref, tiles_ref, cn0_ref, h_ref, ocn_ref, om_ref in streams:
        rev = d == 1
        is_ctx, _, lat, first, last = sq.info(sq.chunk_of_step(step))

        @pl.when(first)
        def _(d=d, is_ctx=is_ctx, lat=lat, cn0_ref=cn0_ref):
            keep = jnp.where(is_ctx, 0.0, 1.0).astype(F32)
            cn_s[d] = cn0_ref[...] * keep
            m_s[d] = jnp.zeros((SUBLANES, LANES), F32)
            for h in range(M_HEADS):
                m0 = m0_ref[((lat * depth + layer) * 2 + d) * M_HEADS + h]
                m_s[d, h:h + 1, :] = jnp.full((1, LANES), m0, F32) * keep

        end = 0 if rev else L - 1
        causal = (si >= ti) if rev else (si <= ti)
        for h in range(M_HEADS):
            hs = slice(h * M_DK, (h + 1) * M_DK)
            gi = d * SUBLANES + h
            q = q_ref[:, hs].astype(BF16)
            vaug = jnp.concatenate([v_ref[:, hs].astype(BF16), ones], axis=1)
            kt = kt_ref[h]
            r_row = rows_ref[gi:gi + 1, :]
            f_b = tiles_ref[M_HEADS + h]
            m_old = m_s[d, h:h + 1, :]
            m_b = jnp.maximum(m_old, tiles_ref[h])
            dmat = jnp.where(causal, jnp.exp(jnp.where(causal, r_row - m_b, 0.0)), 0.0)
            s = jnp.dot(q, kt.astype(BF16), preferred_element_type=F32) * dmat
            wpq = (jnp.exp(m_old - m_b) * q.astype(F32)).astype(BF16)
            cn_old = cn_s[d, h]
            out = jnp.dot(jnp.concatenate([s.astype(BF16), wpq], axis=1),
                          jnp.concatenate([vaug, cn_old.astype(BF16)], axis=0), preferred_element_type=F32)
            h_ref[:, hs] = out[:, :M_DK] / jnp.maximum(jnp.abs(out[:, M_DK:]), jnp.exp(-(f_b + m_b)))
            m_end = m_b[end:end + 1, :]
            dec = jnp.exp(m_old - m_end)
            kwt = (kt * jnp.exp(r_row - m_end)).astype(BF16)
            cn_s[d, h] = jnp.concatenate([dec, dec], axis=1) * cn_old + jnp.dot(kwt, vaug, preferred_element_type=F32)
            m_s[d, h:h + 1, :] = f_b[end:end + 1, :] + m_end

        @pl.when(last)
        def _(d=d, ocn_ref=ocn_ref, om_ref=om_ref):
            ocn_ref[...] = cn_s[d]
            om_ref[...] = m_s[d]


def mlstm_scan(z, rows, tiles, kt, cn0, st_m, layer, n_ctx, t_ctx, n_lat, t_lat):
    L = M_CHUNK
    seqs = (_Seqs(n_ctx, t_ctx, n_lat, t_lat, L, False), _Seqs(n_ctx, t_ctx, n_lat, t_lat, L, True))
    n_seq = n_ctx + n_lat
    n_chunks = seqs[0].n_chunks
    kern = functools.partial(_mlstm_kernel, seqs=seqs, layer=layer, depth=st_m.shape[1])
    in_specs = [pl.BlockSpec(memory_space=pltpu.SMEM)]
    for d, sq in enumerate(seqs):
        cm = lambda s, sq=sq: sq.chunk_of_step(s)
        in_specs += [
            pl.BlockSpec((L, M_WIDTH), lambda s, cm=cm: (cm(s), _C_QM // M_WIDTH)),
            pl.BlockSpec((L, M_WIDTH), lambda s, cm=cm: (cm(s), _C_VM // M_WIDTH)),
            pl.BlockSpec((None, M_HEADS, M_DK, L), lambda s, cm=cm: (cm(s), 0, 0, 0)),
            pl.BlockSpec((_GROWS, L), lambda s, cm=cm: (cm(s), 0)),
            pl.BlockSpec((None, None, 2 * M_HEADS, L, LANES), lambda s, cm=cm, d=d: (cm(s), d, 0, 0, 0)),
        ]
    for d, sq in enumerate(seqs):
        lat_of = lambda s, sq=sq: sq.info(sq.chunk_of_step(s))[2]
        in_specs.append(pl.BlockSpec((None, None, None, M_HEADS, M_DK, 2 * M_DK),
                                     lambda s, f=lat_of, d=d: (f(s), layer, d, 0, 0, 0)))
    out_shape = [jax.ShapeDtypeStruct((n_chunks * L, M_WIDTH), F32)] * 2
    out_specs = [pl.BlockSpec((L, M_WIDTH), lambda s, sq=sq: (sq.chunk_of_step(s), 0)) for sq in seqs]
    for sq in seqs:
        seq_of = lambda s, sq=sq: sq.info(sq.chunk_of_step(s))[1]
        out_shape += [jax.ShapeDtypeStruct((n_seq, M_HEADS, M_DK, 2 * M_DK), F32),
                      jax.ShapeDtypeStruct((n_seq, SUBLANES, LANES), F32)]
        out_specs += [pl.BlockSpec((None, M_HEADS, M_DK, 2 * M_DK), lambda s, f=seq_of: (f(s), 0, 0, 0)),
                      pl.BlockSpec((None, SUBLANES, LANES), lambda s, f=seq_of: (f(s), 0, 0))]
    args = [st_m.reshape(-1), z, z, kt, rows, tiles, z, z, kt, rows, tiles, cn0, cn0]
    return pl.pallas_call(
        kern,
        out_shape=tuple(out_shape),
        grid=(n_chunks,),
        in_specs=in_specs,
        out_specs=tuple(out_specs),
        scratch_shapes=[pltpu.VMEM((2, M_HEADS, M_DK, 2 * M_DK), F32), pltpu.VMEM((2, SUBLANES, LANES), F32)],
        compiler_params=_cparams(("arbitrary",)),
        name="mlstm_scan",
    )(*args)


def _rglru_kernel(x_ref, xp_ref, xn_ref, cw_ref, cb_ref, wr_ref, br_ref, wi_ref, bi_ref, lam_ref, h0_ref,
                  h_ref, hf_ref, xpad, carry, *, seqs):
    L = seqs.chunk
    rev = seqs.reverse
    g = seqs.chunk_of_step(pl.program_id(0))
    is_ctx, _, _, first, last = seqs.info(g)
    head = last if rev else first
    tail = first if rev else last
    halo = SUBLANES
    xpad[0:halo, :] = xp_ref[...] * jnp.where(head, 0.0, 1.0).astype(F32)
    xpad[halo:halo + L, :] = x_ref[...]
    xpad[halo + L:2 * halo + L, :] = xn_ref[...] * jnp.where(tail, 0.0, 1.0).astype(F32)
    x = cb_ref[...]
    for j in range(CONV_W):
        x = x + cw_ref[j:j + 1, :] * xpad[halo - 2 + j:halo - 2 + j + L, :]

    rs, is_ = [], []
    for n in range(R_BLOCKS):
        xb = x[:, n * R_BW:(n + 1) * R_BW].astype(BF16)
        rs.append(jnp.dot(xb, wr_ref[n].astype(BF16), preferred_element_type=F32))
        is_.append(jnp.dot(xb, wi_ref[n].astype(BF16), preferred_element_type=F32))
    rg = jax.nn.sigmoid(jnp.concatenate(rs, axis=-1) + br_ref[...])
    ig = jax.nn.sigmoid(jnp.concatenate(is_, axis=-1) + bi_ref[...])
    log_a = -RG_C * rg * jax.nn.softplus(-lam_ref[...])
    a = jnp.exp(log_a)
    u = jnp.sqrt(1.0 - jnp.exp(2.0 * log_a)) * (ig * x)

    sub = lax.broadcasted_iota(I32, (L, R_WIDTH), 0) & (SUBLANES - 1)
    s = 1
    while s < SUBLANES:
        if rev:
            ok = sub < SUBLANES - s
            a_sh, u_sh = pltpu.roll(a, L - s, 0), pltpu.roll(u, L - s, 0)
        else:
            ok = sub >= s
            a_sh, u_sh = pltpu.roll(a, s, 0), pltpu.roll(u, s, 0)
        u = jnp.where(ok, a * u_sh + u, u)
        a = jnp.where(ok, a * a_sh, a)
        s *= 2

    @pl.when(first)
    def _():
        carry[...] = h0_ref[...] * jnp.where(is_ctx, 0.0, 1.0).astype(F32)

    n_grp = L // SUBLANES
    edge = 0 if rev else SUBLANES - 1
    hprev = carry[...]
    for k in range(n_grp):
        gidx = (n_grp - 1 - k) if rev else k
        rows = slice(gidx * SUBLANES, (gidx + 1) * SUBLANES)
        hg = a[rows] * hprev + u[rows]
        h_ref[rows, :] = hg
        hprev = hg[edge:edge + 1, :]
    carry[...] = hprev

    @pl.when(last)
    def _():
        hf_ref[...] = hprev


def rglru_direction(z, conv_w, conv_b, wr, br, wi, bi, lam, h0, layer, direction, n_ctx, t_ctx, n_lat, t_lat):
    seqs = _Seqs(n_ctx, t_ctx, n_lat, t_lat, R_CHUNK, direction == 1)
    n_seq = n_ctx + n_lat
    L = R_CHUNK
    n_rows = seqs.n_chunks * L
    hb = L // SUBLANES
    nb8 = n_rows // SUBLANES
    cm = lambda s: seqs.chunk_of_step(s)
    lat_of = lambda s: seqs.info(cm(s))[2]
    seq_of = lambda s: seqs.info(cm(s))[1]
    xcol = _C_XR // R_WIDTH
    vec_ld = lambda: pl.BlockSpec((None, None, 1, R_WIDTH), lambda s: (layer, direction, 0, 0))
    mat_ld = lambda: pl.BlockSpec((None, None, R_BLOCKS, R_BW, R_BW), lambda s: (layer, direction, 0, 0, 0))
    kern = functools.partial(_rglru_kernel, seqs=seqs)
    depth = conv_w.shape[0]
    r4 = lambda a: a.reshape(depth, 2, 1, R_WIDTH)
    return pl.pallas_call(
        kern,
        out_shape=(jax.ShapeDtypeStruct((n_rows, R_WIDTH), F32), jax.ShapeDtypeStruct((n_seq, 1, R_WIDTH), F32)),
        grid=(seqs.n_chunks,),
        in_specs=[
            pl.BlockSpec((L, R_WIDTH), lambda s: (cm(s), xcol)),
            pl.BlockSpec((SUBLANES, R_WIDTH), lambda s: (jnp.maximum(cm(s) * hb - 1, 0), xcol)),
            pl.BlockSpec((SUBLANES, R_WIDTH), lambda s: (jnp.minimum((cm(s) + 1) * hb, nb8 - 1), xcol)),
            pl.BlockSpec((None, CONV_W, R_WIDTH), lambda s: (layer, 0, 0)),
            pl.BlockSpec((None, 1, R_WIDTH), lambda s: (layer, 0, 0)),
            mat_ld(), vec_ld(), mat_ld(), vec_ld(), vec_ld(),
            pl.BlockSpec((None, None, None, 1, R_WIDTH), lambda s: (lat_of(s), layer, direction, 0, 0)),
        ],
        out_specs=(pl.BlockSpec((L, R_WIDTH), lambda s: (cm(s), 0)),
                   pl.BlockSpec((None, 1, R_WIDTH), lambda s: (seq_of(s), 0, 0))),
        scratch_shapes=[pltpu.VMEM((L + 2 * SUBLANES, R_WIDTH), F32), pltpu.VMEM((1, R_WIDTH), F32)],
        compiler_params=_cparams(("arbitrary",)),
        name=f"rglru_d{direction}",
    )(z, z, z, conv_w, conv_b.reshape(depth, 1, R_WIDTH), wr, r4(br), wi, r4(bi), r4(lam),
      h0.reshape(h0.shape[0], depth, 2, 1, R_WIDTH))


def _gelu_tanh(x):
    return 0.5 * x * (1.0 + jnp.tanh(math.sqrt(2.0 / math.pi) * (x + 0.044715 * (x * x * x))))


def _out_kernel(x_ref, ac_ref, al_ref, mf_ref, mb_ref, om_ref, rf_ref, rb_ref, yr_ref, mg_ref, gate_ref, w_ref,
                o_ref, *, ctx_tiles, sub):
    is_ctx = pl.program_id(0) < ctx_tiles
    for r0 in range(0, x_ref.shape[0], sub):
        rows = slice(r0, r0 + sub)
        att = jnp.where(is_ctx, ac_ref[rows, :], al_ref[rows, :])
        hs = mf_ref[rows, :] + mb_ref[rows, :]
        parts = [att]
        for h in range(M_HEADS):
            cs = slice(h * M_DK, (h + 1) * M_DK)
            hh = hs[:, cs]
            hn = hh * lax.rsqrt(jnp.mean(hh * hh, axis=-1, keepdims=True) + NORM_EPS) * mg_ref[:, cs]
            parts.append((hn * jax.nn.sigmoid(om_ref[rows, cs])).astype(BF16))
        parts.append(((rf_ref[rows, :] + rb_ref[rows, :]) * _gelu_tanh(yr_ref[rows, :])).astype(BF16))
        mix = jnp.concatenate(parts, axis=1)
        o_ref[rows, :] = x_ref[rows, :] + gate_ref[...] * jnp.dot(mix, w_ref[...], preferred_element_type=F32)


def out_proj(x, a_ctx, a_lat, mf, mb, rf, rb, z, mnorm_g, mod, w_out, nc, tl, tm=512, sub=256):
    n, d = x.shape
    grp = lambda i: _group_of_tile(i * tm, nc, tl)
    ctx_tiles = nc // tm
    lat_tiles = (n - nc) // tm
    rowblk = lambda w, col: pl.BlockSpec((tm, w), lambda i: (i, col))
    kern = functools.partial(_out_kernel, ctx_tiles=ctx_tiles, sub=sub)
    return pl.pallas_call(
        kern,
        out_shape=jax.ShapeDtypeStruct((n, d), F32),
        grid=(n // tm,),
        in_specs=[
            rowblk(d, 0),
            pl.BlockSpec((tm, A_WIDTH), lambda i: (jnp.minimum(i, ctx_tiles - 1), 0)),
            pl.BlockSpec((tm, A_WIDTH), lambda i: (jnp.clip(i - ctx_tiles, 0, lat_tiles - 1), 0)),
            rowblk(M_WIDTH, 0), rowblk(M_WIDTH, 0), rowblk(M_WIDTH, _C_OM // M_WIDTH),
            rowblk(R_WIDTH, 0), rowblk(R_WIDTH, 0), rowblk(R_WIDTH, _C_YR // R_WIDTH),
            pl.BlockSpec((1, M_WIDTH), lambda i: (0, 0)),
            pl.BlockSpec((None, None, 1, d), lambda i: (grp(i), 2, 0, 0)),
            pl.BlockSpec((d, d), lambda i: (0, 0), pipeline_mode=pl.Buffered(1)),
        ],
        out_specs=rowblk(d, 0),
        compiler_params=_cparams(("arbitrary",)),
        name="out_proj",
    )(x, a_ctx, a_lat, mf, mb, z, rf, rb, z, mnorm_g.reshape(1, -1), mod, w_out)


def _bits(x):
    return lax.bitcast_convert_type(x, U32)


def _moe_pre_kernel(x_ref, g_ref, sh_ref, sc_ref, wh_ref, wl_ref, br_ref, xp_ref, rg_ref, re_ref):
    x = x_ref[...]
    y = x * lax.rsqrt(jnp.mean(x * x, axis=-1, keepdims=True) + NORM_EPS) * g_ref[...]
    xn = y * (1.0 + sc_ref[...]) + sh_ref[...]
    xb = xn.astype(BF16)
    xb32 = xb.astype(F32)
    xp_ref[...] = (_bits(xb32[:, HALF:]) & jnp.uint32(0xFFFF0000)) | (_bits(xb32[:, :HALF]) >> 16)

    xl = (xn - xb32).astype(BF16)
    lg = (jnp.dot(xb, wh_ref[...], preferred_element_type=F32) + jnp.dot(xb, wl_ref[...], preferred_element_type=F32)
          + jnp.dot(xl, wh_ref[...], preferred_element_type=F32) + br_ref[...])
    lane = lax.broadcasted_iota(I32, lg.shape, 1).astype(F32)
    ninf = jnp.float32(-jnp.inf)

    def top(mask):
        val = jnp.max(jnp.where(mask, lg, ninf), axis=-1, keepdims=True)
        idx = jnp.min(jnp.where(mask & (lg == val), lane, float(LANES)), axis=-1, keepdims=True)
        return val, idx

    is_g = lane < N_GROUPS
    g_val, g_idx = top(is_g)
    g_w = 1.0 / jnp.sum(jnp.where(is_g, jnp.exp(lg - g_val), 0.0), axis=-1, keepdims=True)
    e_lo = N_GROUPS + g_idx * EXPERTS_PER_GROUP
    in_grp = (lane >= e_lo) & (lane < e_lo + EXPERTS_PER_GROUP)
    v1, i1 = top(in_grp)
    v2, i2 = top(in_grp & (lane != i1))
    t = jnp.exp(v2 - v1)
    w1 = g_w / (1.0 + t)
    rg_ref[...] = jnp.where(lane == 0, w1, jnp.where(lane == 1, w1 * t, 0.0))
    re_ref[...] = jnp.where(lane == 0, i1 - N_GROUPS, jnp.where(lane == 1, i2 - N_GROUPS, 0.0)).astype(I32)


def moe_pre(x, norm_g, mod, w_router, b_router, nc, tl, tm=512):
    n, d = x.shape
    grp = lambda i: _group_of_tile(i * tm, nc, tl)
    lane_out = lambda: pl.BlockSpec((tm, LANES), lambda i: (i, 0))
    w_hi = w_router.astype(BF16)
    return pl.pallas_call(
        _moe_pre_kernel,
        out_shape=(jax.ShapeDtypeStruct((n, HALF), U32), jax.ShapeDtypeStruct((n, LANES), F32),
                   jax.ShapeDtypeStruct((n, LANES), I32)),
        grid=(n // tm,),
        in_specs=[
            pl.BlockSpec((tm, d), lambda i: (i, 0)),
            pl.BlockSpec((1, d), lambda i: (0, 0)),
            pl.BlockSpec((None, None, 1, d), lambda i: (grp(i), 3, 0, 0)),
            pl.BlockSpec((None, None, 1, d), lambda i: (grp(i), 4, 0, 0)),
            pl.BlockSpec((d, LANES), lambda i: (0, 0)),
            pl.BlockSpec((d, LANES), lambda i: (0, 0)),
            pl.BlockSpec((1, LANES), lambda i: (0, 0)),
        ],
        out_specs=(pl.BlockSpec((tm, HALF), lambda i: (i, 0)), lane_out(), lane_out()),
        compiler_params=_cparams(("arbitrary",)),
        name="moe_pre",
    )(x, norm_g.reshape(1, d), mod, mod, w_hi, (w_router - w_hi.astype(F32)).astype(BF16), b_router)


def _expert_changed(be_ref, j):
    return (j == 0) | (be_ref[j] != be_ref[jnp.maximum(j - 1, 0)])


_ROW_STEP = LANES
WEIGHT_DMA_PRIORITY = 1


def _for_row_count(nv, tb, body, out_ref):
    for r in range(_ROW_STEP, tb + 1, _ROW_STEP):
        @pl.when((nv > r - _ROW_STEP) & (nv <= r))
        def _(r=r):
            body(r)

    @pl.when(nv == 0)
    def _():
        out_ref[...] = jnp.zeros(out_ref.shape, out_ref.dtype)


def _moe_up_kernel(be_ref, bs_ref, nv_ref, nx_ref, x_ref, wg_hbm, wu_hbm, h_ref, wbuf, sem, wg_s, wu_s, *, layer):
    j = pl.program_id(0)
    nv = nv_ref[j]
    tb = x_ref.shape[0]

    def weight_copies(e):
        return (pltpu.make_async_copy(wg_hbm.at[layer, e], wbuf.at[0], sem.at[0]),
                pltpu.make_async_copy(wu_hbm.at[layer, e], wbuf.at[1], sem.at[1]))

    @pl.when(j == 0)
    def _():
        for cp in weight_copies(be_ref[0]):
            cp.start(priority=WEIGHT_DMA_PRIORITY)

    @pl.when((nv > 0) & _expert_changed(be_ref, j))
    def _():
        for cp in weight_copies(be_ref[j]):
            cp.wait()
        wg_s[...] = wbuf[0].astype(BF16)
        wu_s[...] = wbuf[1].astype(BF16)

        @pl.when(nx_ref[j] >= 0)
        def _():
            for cp in weight_copies(nx_ref[j]):
                cp.start(priority=WEIGHT_DMA_PRIORITY)

    def body(r):
        w = x_ref[0:r, :]
        lo = lax.bitcast_convert_type(w << 16, F32).astype(BF16)
        hi = lax.bitcast_convert_type(w & jnp.uint32(0xFFFF0000), F32).astype(BF16)

        def mm(w_s):
            return (jnp.dot(lo, w_s[0:HALF, :], preferred_element_type=F32)
                    + jnp.dot(hi, w_s[HALF:, :], preferred_element_type=F32))

        gt, up = mm(wg_s), mm(wu_s)
        h_ref[0:r, :] = (gt * jax.nn.sigmoid(gt) * up).astype(BF16)
        if r < tb:
            h_ref[r:tb, :] = jnp.zeros((tb - r, h_ref.shape[1]), BF16)

    _for_row_count(nv, tb, body, h_ref)


def _pack_halves(y):
    yb = y.astype(BF16).astype(F32)
    return (_bits(yb[:, HALF:]) & jnp.uint32(0xFFFF0000)) | (_bits(yb[:, :HALF]) >> 16)


def _unpack_halves(w):
    return (lax.bitcast_convert_type(w << 16, F32), lax.bitcast_convert_type(w & jnp.uint32(0xFFFF0000), F32))


def _moe_down_kernel(be_ref, bs_ref, nv_ref, nx_ref, h_ref, wd_hbm, y_ref, wbuf, sem, wd_s, *, layer):
    j = pl.program_id(0)
    nv = nv_ref[j]
    tb = h_ref.shape[0]

    def weight_copy(e):
        return pltpu.make_async_copy(wd_hbm.at[layer, e], wbuf, sem.at[0])

    @pl.when(j == 0)
    def _():
        weight_copy(be_ref[0]).start(priority=WEIGHT_DMA_PRIORITY)

    @pl.when((nv > 0) & _expert_changed(be_ref, j))
    def _():
        weight_copy(be_ref[j]).wait()
        wd_s[...] = wbuf[...].astype(BF16)

        @pl.when(nx_ref[j] >= 0)
        def _():
            weight_copy(nx_ref[j]).start(priority=WEIGHT_DMA_PRIORITY)

    def body(r):
        y_ref[0:r, :] = _pack_halves(jnp.dot(h_ref[0:r, :], wd_s[...], preferred_element_type=F32))
        if r < tb:
            y_ref[r:tb, :] = jnp.zeros((tb - r, y_ref.shape[1]), U32)

    _for_row_count(nv, tb, body, y_ref)


def moe_experts(xb, blk_e, blk_src, n_valid, nxt_e, w_gate, w_up, w_down, layer, tb=MOE_TB):
    rows = xb.shape[0]
    n_blocks = rows // tb
    d, de = w_gate.shape[-2:]
    any_spec = pl.BlockSpec(memory_space=pl.ANY)
    h = pl.pallas_call(
        functools.partial(_moe_up_kernel, layer=layer),
        out_shape=jax.ShapeDtypeStruct((rows, de), BF16),
        grid_spec=pltpu.PrefetchScalarGridSpec(
            num_scalar_prefetch=4,
            grid=(n_blocks,),
            in_specs=[pl.BlockSpec((tb, HALF), lambda j, be, bs, nv, nx: (bs[j], 0)), any_spec, any_spec],
            out_specs=pl.BlockSpec((tb, de), lambda j, be, bs, nv, nx: (j, 0)),
            scratch_shapes=[pltpu.VMEM((2, d, de), F32), pltpu.SemaphoreType.DMA((2,)),
                            pltpu.VMEM((d, de), BF16), pltpu.VMEM((d, de), BF16)],
        ),
        compiler_params=_cparams(("arbitrary",)),
        name="moe_up",
    )(blk_e, blk_src, n_valid, nxt_e, xb, w_gate, w_up)
    return pl.pallas_call(
        functools.partial(_moe_down_kernel, layer=layer),
        out_shape=jax.ShapeDtypeStruct((rows, HALF), U32),
        grid_spec=pltpu.PrefetchScalarGridSpec(
            num_scalar_prefetch=4,
            grid=(n_blocks,),
            in_specs=[pl.BlockSpec((tb, de), lambda j, be, bs, nv, nx: (bs[j], 0)), any_spec],
            out_specs=pl.BlockSpec((tb, HALF), lambda j, be, bs, nv, nx: (j, 0)),
            scratch_shapes=[pltpu.VMEM((de, d), F32), pltpu.SemaphoreType.DMA((1,)), pltpu.VMEM((de, d), BF16)],
        ),
        compiler_params=_cparams(("arbitrary",)),
        name="moe_down",
    )(blk_e, blk_src, n_valid, nxt_e, h, w_down)


def moe_layout(eid, tb=MOE_TB):
    n = eid.shape[0]
    s_len = n * TOP_K
    flat_e = eid.reshape(s_len)
    onehot = (flat_e[:, None] == jnp.arange(N_EXPERTS, dtype=I32)[None, :]).astype(I32)
    rank = jnp.sum((jnp.cumsum(onehot, axis=0) - onehot) * onehot, axis=1)
    counts = jnp.sum(onehot, axis=0)
    nblk_e = (counts + tb - 1) // tb
    blk_end = jnp.cumsum(nblk_e)
    blk_start = blk_end - nblk_e
    dest = jnp.sum(onehot * blk_start[None, :], axis=1) * tb + rank
    n_blocks = s_len // tb + N_EXPERTS
    n_used = blk_end[-1]
    jc = jnp.minimum(jnp.arange(n_blocks, dtype=I32), n_used - 1)
    blk_e = jnp.minimum(jnp.sum((blk_end[None, :] <= jc[:, None]).astype(I32), axis=1), N_EXPERTS - 1).astype(I32)
    used = jnp.arange(n_blocks, dtype=I32) < n_used
    nxt_blk = blk_end[blk_e]
    nxt_e = jnp.where(used & (nxt_blk < n_used), blk_e[jnp.minimum(nxt_blk, n_blocks - 1)], -1).astype(I32)
    n_valid = jnp.where(used, jnp.clip(counts[blk_e] - (jc - blk_start[blk_e]) * tb, 0, tb), 0).astype(I32)
    pad_e = nblk_e * tb - counts
    pad_incl = jnp.cumsum(pad_e)
    n_fill = n_blocks * tb - s_len
    k = jnp.arange(n_fill, dtype=I32)
    e_k = jnp.minimum(jnp.sum((pad_incl[None, :] <= k[:, None]).astype(I32), axis=1), N_EXPERTS - 1)
    oh_k = (e_k[:, None] == jnp.arange(N_EXPERTS, dtype=I32)[None, :]).astype(I32)
    pick = lambda v: jnp.sum(oh_k * v[None, :], axis=1)
    in_used = k < pad_incl[-1]
    fill_pos = jnp.where(in_used, pick(blk_start * tb + counts) + k - pick(pad_incl - pad_e),
                         n_used * tb + k - pad_incl[-1])
    keys = jnp.concatenate([dest, fill_pos])
    vals = jnp.concatenate([jnp.arange(s_len, dtype=I32) // TOP_K, k % n])
    slot_tok = lax.sort_key_val(keys, vals)[1]
    return dest.reshape(n, TOP_K), slot_tok, blk_e, jc, n_valid, nxt_e


def _take_rows(a, idx):
    return a.at[idx].get(mode="promise_in_bounds")


def _final_norm_kernel(x_ref, y0_ref, y1_ref, gt_ref, g2_ref, g_ref, oc_ref, ol_ref, xs_ref, *, ctx_tiles):
    _combine_into(xs_ref, x_ref, y0_ref, y1_ref, gt_ref, g2_ref)
    x = xs_ref[...]
    y = x * lax.rsqrt(jnp.mean(x * x, axis=-1, keepdims=True) + NORM_EPS) * g_ref[...]

    @pl.when(pl.program_id(0) < ctx_tiles)
    def _():
        oc_ref[...] = y

    @pl.when(pl.program_id(0) >= ctx_tiles)
    def _():
        ol_ref[...] = y


def final_norm(x, pend, g, nc, tl, tm=512):
    n, d = x.shape
    y0, y1, gates, mod_prev = pend
    grp = lambda i: _group_of_tile(i * tm, nc, tl)
    ctx_tiles, lat_tiles = nc // tm, (n - nc) // tm
    row = lambda w: pl.BlockSpec((tm, w), lambda i: (i, 0))
    return pl.pallas_call(
        functools.partial(_final_norm_kernel, ctx_tiles=ctx_tiles),
        out_shape=(jax.ShapeDtypeStruct((nc, d), F32), jax.ShapeDtypeStruct((n - nc, d), F32)),
        grid=(n // tm,),
        in_specs=[row(d), row(HALF), row(HALF), row(LANES),
                  pl.BlockSpec((None, None, 1, d), lambda i: (grp(i), 5, 0, 0)),
                  pl.BlockSpec((1, d), lambda i: (0, 0))],
        out_specs=(pl.BlockSpec((tm, d), lambda i: (jnp.minimum(i, ctx_tiles - 1), 0)),
                   pl.BlockSpec((tm, d), lambda i: (jnp.clip(i - ctx_tiles, 0, lat_tiles - 1), 0))),
        scratch_shapes=[pltpu.VMEM((tm, d), F32)],
        compiler_params=_cparams(("arbitrary",)),
        name="final_norm",
    )(x, y0, y1, gates, mod_prev, g.reshape(1, d))


def _pad_lanes(a):
    return jnp.pad(a, ((0, 0), (0, LANES - a.shape[1])))


def kernel(x_prompt, x_sample, cache_k, cache_v, state_mlstm_C, state_mlstm_n, state_mlstm_m, state_rglru_h, c, c_ctx, ada_w, ada_b, norm1_g, w_in, attn_sink, mlstm_gate_b, mlstm_norm_g, rg_conv_w, rg_conv_b, rg_wr, rg_br, rg_wi, rg_bi, rg_lam, w_out, norm2_g, router_wg, router_bg, router_we, router_be, exp_w_gate, exp_w_up, exp_w_down, final_norm_g):
    bc, tc, d = x_prompt.shape
    bl, tl, _ = x_sample.shape
    depth = w_in.shape[0]
    past = cache_k.shape[2]
    nc, nl = bc * tc, bl * tl
    tm = math.gcd(1024, math.gcd(tl, nc))

    x = jnp.concatenate([x_prompt.reshape(nc, d), x_sample.reshape(nl, d)], axis=0)
    cvec = jnp.concatenate([c_ctx[None, :], c, jnp.zeros((SUBLANES - 1 - bl, d), F32)], axis=0)
    mods = adaln_all(cvec, ada_w, ada_b).reshape(depth, SUBLANES, 6, 1, d)
    rope_c, rope_s = rope_tables(tl)
    gsplit = Z_MAIN - 2 * R_WIDTH
    w_main = jnp.concatenate([w_in[:, :, :gsplit], w_in[:, :, gsplit + M_GATES:]], axis=2).astype(BF16)
    w_gcol = jnp.pad(w_in[:, :, gsplit:gsplit + M_GATES], ((0, 0), (0, 0), (0, LANES - M_GATES))).astype(BF16)
    w_out_b = w_out.astype(BF16)
    ck = cache_k.reshape(bl, depth, past, KV_WIDTH)
    cv = cache_v.reshape(bl, depth, past, KV_WIDTH)
    cn0 = jnp.concatenate([state_mlstm_C, jnp.broadcast_to(state_mlstm_n[..., None], state_mlstm_C.shape)], axis=-1)

    new_k, new_v, new_c, new_n, new_m, new_h = [], [], [], [], [], []
    pend = None
    for l in range(depth):
        mod = mods[l]
        if pend is None:
            z, zg = in_proj(x, None, norm1_g[l], mod, w_main[l], w_gcol[l], nc, tl, tm)
        else:
            z, zg, x = in_proj(x, pend, norm1_g[l], mod, w_main[l], w_gcol[l], nc, tl, tm)

        a_ctx = attention_context(z, attn_sink[l], bc, tc)
        a_lat = attention_latent(z, ck, cv, l, attn_sink[l], rope_c, rope_s, nc, bl, tl)

        rows, tiles, kt = mlstm_gates(z, zg, _pad_lanes(mlstm_gate_b[l].reshape(1, M_GATES)), tile=tm)
        res = mlstm_scan(z, rows, tiles, kt, cn0, state_mlstm_m, l, bc, tc, bl, tl)
        mh = res[0:2]
        for dr in range(2):
            cnf, mf = res[2 + 2 * dr:4 + 2 * dr]
            new_c.append(cnf[:bc, :, :, :M_DK])
            new_n.append(cnf[:bc, :, :, M_DK])
            new_m.append(mf[:bc, :M_HEADS, 0])
        rh = []
        for dr in range(2):
            rd, hf = rglru_direction(z, rg_conv_w, rg_conv_b, rg_wr, rg_br, rg_wi, rg_bi, rg_lam, state_rglru_h,
                                     l, dr, bc, tc, bl, tl)
            rh.append(rd)
            new_h.append(hf[:bc, 0])
        new_k.append(z[:nc, _C_K:_C_K + KV_WIDTH].reshape(bc, tc, KV_HEADS, HEAD_DIM))
        new_v.append(z[:nc, _C_V:_C_V + KV_WIDTH].reshape(bc, tc, KV_HEADS, HEAD_DIM))

        x = out_proj(x, a_ctx, a_lat, mh[0], mh[1], rh[0], rh[1], z, mlstm_norm_g[l], mod, w_out_b[l], nc, tl)

        w_router = _pad_lanes(jnp.concatenate([router_wg[l], router_we[l]], axis=1))
        b_router = _pad_lanes(jnp.concatenate([router_bg[l], router_be[l]])[None, :])
        xp, route_g, route_e = moe_pre(x, norm2_g[l], mod, w_router, b_router, nc, tl)
        dest, slot_tok, blk_e, blk_src, n_valid, nxt_e = moe_layout(route_e[:, :TOP_K])
        yb = moe_experts(_take_rows(xp, slot_tok), blk_e, blk_src, n_valid, nxt_e, exp_w_gate, exp_w_up, exp_w_down, l)
        pend = (_take_rows(yb, dest[:, 0]), _take_rows(yb, dest[:, 1]), route_g, mod)

    y_ctx, y_lat = final_norm(x, pend, final_norm_g, nc, tl)
    stack2 = lambda parts: jnp.stack([jnp.stack(parts[2 * l:2 * l + 2], axis=1) for l in range(depth)], axis=1)
    return (y_ctx.reshape(bc, tc, d), y_lat.reshape(bl, tl, d),
            jnp.stack(new_k, axis=1), jnp.stack(new_v, axis=1),
            stack2(new_c), stack2(new_n), stack2(new_m), stack2(new_h))
```
